```python
import jax
import jax.numpy as jnp
from jax import lax
import numpy as np

D_MODEL = 4096
BATCH = 1
SEQ = 8192
DEPTH = 2

GRID_W = 64
CTX_LEN = 256
D_MIX = D_MODEL
GROUP_W = D_MIX // 4
HEAD_DIM = 128
A_HEADS = GROUP_W // HEAD_DIM
B_HEADS = GROUP_W // HEAD_DIM
D_HEADS = GROUP_W // HEAD_DIM
C_GROUPS = 4
C_GW = GROUP_W // C_GROUPS
POOL_WINDOWS = (2, 4, 8, 16)
NA_MAX_ROWS = 8
NA_COLS = 16
A_CHUNK = 32
D_CHUNK = 64
ROPE_BASE = 10000.0
MLSTM_F_BIAS = 3.0
N_GROUPS = 4
EXP_PER_GROUP = 8
N_EXPERTS = N_GROUPS * EXP_PER_GROUP
TOP_K = 2
D_FF = D_MODEL // 8
EPS = 1e-6
IN_WIDTHS = (GROUP_W,) * 13 + (D_HEADS,) * 4
IN_COLS = sum(IN_WIDTHS)
D_GATE_OFF = 13 * GROUP_W

kernel_name = "hybrid_dit_hgrn2_natten_pool_mlstm_hmoe"


def rmsnorm(x, w):
    xf = x.astype(jnp.float32)
    y = xf * lax.rsqrt(jnp.mean(xf * xf, axis=-1, keepdims=True) + EPS)
    return (y * w.astype(jnp.float32)).astype(x.dtype)


def modulate(h, shift, scale):
    return h * (1 + scale) + shift


def split_heads(t, n_heads):
    b, l, _ = t.shape
    return t.reshape(b, l, n_heads, -1).transpose(0, 2, 1, 3)


def merge_heads(t):
    b, h, l, d = t.shape
    return t.transpose(0, 2, 1, 3).reshape(b, l, h * d)


def head_rmsnorm(o, w):
    o = o * lax.rsqrt(jnp.mean(o * o, axis=-1, keepdims=True) + EPS)
    return merge_heads(o) * w.astype(jnp.float32)


def rev(t):
    return jnp.flip(t, axis=2)


def split_proj(p):
    idx = np.cumsum(IN_WIDTHS)[:-1].tolist()
    return jnp.split(p, idx, axis=-1)


def rope_2d(t, row, col):
    quarter = HEAD_DIM // 4
    half = HEAD_DIM // 2
    inv_freq = ROPE_BASE ** (-jnp.arange(quarter, dtype=jnp.float32) / quarter)

    def rot(u, p):
        ang = p.astype(jnp.float32)[:, None] * inv_freq
        cos, sin = jnp.cos(ang), jnp.sin(ang)
        u1, u2 = u[..., :quarter], u[..., quarter:]
        return jnp.concatenate([u1 * cos - u2 * sin, u2 * cos + u1 * sin], axis=-1)

    return jnp.concatenate([rot(t[..., :half], row), rot(t[..., half:], col)], axis=-1)


def hgrn_lower_bounds(p):
    cs = jnp.cumsum(jax.nn.softmax(p.astype(jnp.float32), axis=1), axis=1)
    return cs - cs[:, :1]


def hgrn2_gates(f_logits, lb):
    log_f = jnp.logaddexp(jnp.log(lb), jnp.log1p(-lb) + jax.nn.log_sigmoid(f_logits))
    k = (1.0 - lb) * jax.nn.sigmoid(-f_logits)
    return split_heads(log_f, A_HEADS), split_heads(k, A_HEADS)


def gla_chunked(q, k, v, log_f, s0, need_out):
    bsz, h, l, dk = q.shape
    n = l // A_CHUNK
    blk = lambda t: t.reshape(bsz, h, n, A_CHUNK, t.shape[-1])
    q, k, v, log_f = blk(q), blk(k), blk(v), blk(log_f)
    b = jnp.cumsum(log_f, axis=3)
    b_end = b[:, :, :, -1:, :]
    delta = jnp.einsum('bhncd,bhnce->bhnde', k * jnp.exp(b_end - b), v)
    decay = jnp.exp(b_end[:, :, :, 0, :])

    def step(s, inp):
        dec, dlt = inp
        return dec[..., None] * s + dlt, s

    s_fin, s_prev = lax.scan(step, s0, (jnp.moveaxis(decay, 2, 0), jnp.moveaxis(delta, 2, 0)))
    if not need_out:
        return None, s_fin
    s_prev = jnp.moveaxis(s_prev, 0, 2)
    q_dec = q * jnp.exp(b)
    scores = jnp.einsum('bhnid,bhnjd->bhnij', q_dec, k * jnp.exp(-b))
    lower = jnp.tril(jnp.ones((A_CHUNK, A_CHUNK), dtype=bool))
    scores = jnp.where(lower, scores, 0.0)
    o = jnp.einsum('bhnij,bhnje->bhnie', scores, v) + jnp.einsum('bhnid,bhnde->bhnie', q_dec, s_prev)
    return o.reshape(bsz, h, l, -1), s_fin


def mlstm_chunked(q, k, v, log_f, log_i, state0, need_out):
    bsz, h, l, dk = q.shape
    n = l // D_CHUNK
    blk = lambda t: t.reshape(bsz, h, n, D_CHUNK, *t.shape[3:])
    q, k, v, log_f, log_i = blk(q), blk(k), blk(v), blk(log_f), blk(log_i)
    b = jnp.cumsum(log_f, axis=-1)
    b_end = b[..., -1]
    g = b_end[..., None] - b + log_i
    m_loc = jnp.max(g, axis=-1)
    w_end = jnp.exp(g - m_loc[..., None])
    d_s = jnp.einsum('bhnc,bhncd,bhnce->bhnde', w_end, k, v)
    d_n = jnp.einsum('bhnc,bhncd->bhnd', w_end, k)

    def step(carry, inp):
        s, nv, m = carry
        be, ml, ds, dn = inp
        m_new = jnp.maximum(be + m, ml)
        a = jnp.exp(be + m - m_new)
        cc = jnp.exp(ml - m_new)
        new = (a[..., None, None] * s + cc[..., None, None] * ds, a[..., None] * nv + cc[..., None] * dn, m_new)
        return new, (s, nv, m)

    mv = lambda t: jnp.moveaxis(t, 2, 0)
    fin, prev = lax.scan(step, state0, (mv(b_end), mv(m_loc), mv(d_s), mv(d_n)))
    if not need_out:
        return None, fin
    s_prev, n_prev, m_prev = (jnp.moveaxis(t, 0, 2) for t in prev)
    lower = jnp.tril(jnp.ones((D_CHUNK, D_CHUNK), dtype=bool))
    dmat = jnp.where(lower, b[..., :, None] - b[..., None, :] + log_i[..., None, :], -jnp.inf)
    m_inter = b + m_prev[..., None]
    m_t = jnp.maximum(m_inter, jnp.max(dmat, axis=-1))
    w = jnp.exp(dmat - m_t[..., None]) * jnp.einsum('bhnid,bhnjd->bhnij', q, k)
    a = jnp.exp(m_inter - m_t)
    num = jnp.einsum('bhnij,bhnje->bhnie', w, v) + a[..., None] * jnp.einsum('bhnid,bhnde->bhnie', q, s_prev)
    den = jnp.sum(w, axis=-1) + a * jnp.einsum('bhnid,bhnd->bhni', q, n_prev)
    hh = num / jnp.maximum(jnp.abs(den), jnp.exp(-m_t))[..., None]
    return hh.reshape(bsz, h, l, -1), fin


def context_attention(q, k, v):
    s = jnp.einsum('bhid,bhjd->bhij', q, k).astype(jnp.float32) * HEAD_DIM ** -0.5
    p = jax.nn.softmax(s, axis=-1).astype(v.dtype)
    return jnp.einsum('bhij,bhjd->bhid', p, v)


def neighborhood_attention(q, k, v, k_ctx, v_ctx, rpb):
    bsz, h, s, hd = q.shape
    rows = s // GRID_W
    kh = min(NA_MAX_ROWS, rows)
    grid = lambda t: t.reshape(bsz, h, rows, GRID_W, hd)
    qg, kg, vg = grid(q), grid(k), grid(v)
    r = jnp.arange(rows)
    key_rows = jnp.clip(r - kh // 2, 0, rows - kh)[:, None] + jnp.arange(kh)[None, :]
    kb, vb = kg[:, :, key_rows], vg[:, :, key_rows]
    cols = jnp.arange(GRID_W)
    c0 = jnp.clip(cols - NA_COLS // 2, 0, GRID_W - NA_COLS)
    in_win = (cols[None, :] >= c0[:, None]) & (cols[None, :] < c0[:, None] + NA_COLS)
    dr = key_rows - r[:, None] + (NA_MAX_ROWS - 1)
    dc = jnp.clip(cols[None, :] - cols[:, None] + (NA_COLS - 1), 0, 2 * NA_COLS - 2)
    bias = rpb[:, dr[:, None, :, None], dc[None, :, None, :]]
    scale = hd ** -0.5
    s_loc = jnp.einsum('bhrqd,bhrkwd->bhrqkw', qg, kb).astype(jnp.float32) * scale + bias.astype(jnp.float32)
    s_loc = jnp.where(in_win[:, None, :], s_loc, -jnp.inf)
    s_loc = s_loc.reshape(bsz, h, rows, GRID_W, kh * GRID_W)
    s_ctx = jnp.einsum('bhrqd,bhcd->bhrqc', qg, k_ctx).astype(jnp.float32) * scale
    p = jax.nn.softmax(jnp.concatenate([s_loc, s_ctx], axis=-1), axis=-1).astype(v.dtype)
    p_loc = p[..., :kh * GRID_W].reshape(bsz, h, rows, GRID_W, kh, GRID_W)
    p_ctx = p[..., kh * GRID_W:]
    o = jnp.einsum('bhrqkw,bhrkwd->bhrqd', p_loc, vb) + jnp.einsum('bhrqc,bhcd->bhrqd', p_ctx, v_ctx)
    return o.reshape(bsz, h, s, hd)


def pool_mixer(u, w_pool, scale):
    b, l, _ = u.shape
    us = u.astype(jnp.float32).reshape(b, l, C_GROUPS, C_GW)
    csum = jnp.concatenate([jnp.zeros((b, 1, C_GROUPS, C_GW), jnp.float32), jnp.cumsum(us, axis=1)], axis=1)
    t = jnp.arange(l)[:, None]
    half = jnp.array(POOL_WINDOWS, dtype=jnp.int32)[None, :] // 2
    lo = jnp.clip(t - half, 0, l)
    hi = jnp.clip(t + half, 0, l)
    g = jnp.arange(C_GROUPS)[None, :]
    mean = (csum[:, hi, g] - csum[:, lo, g]) / (hi - lo).astype(jnp.float32)[None, :, :, None]
    y = jnp.einsum('blgc,gce->blge', mean - us, w_pool.astype(jnp.float32))
    return y.reshape(b, l, GROUP_W) * scale.astype(jnp.float32)


def token_mixers(p, lb_f, lb_b, gn_a, rpb, w_pool, pool_s, gn_d, w_out, init, ctx_kv, grid_pos, need_out):
    (a_q, a_i, a_g, a_ff, a_fb, b_q, b_k, b_v, c_x,
     d_q, d_k, d_v, d_o, d_if, d_ff, d_ib, d_fb) = split_proj(p)
    bsz = p.shape[0]
    f32 = lambda t: t.astype(jnp.float32)
    if init is None:
        za = jnp.zeros((bsz, A_HEADS, HEAD_DIM, HEAD_DIM), jnp.float32)
        zd = (jnp.zeros((bsz, D_HEADS, HEAD_DIM, HEAD_DIM), jnp.float32),
              jnp.zeros((bsz, D_HEADS, HEAD_DIM), jnp.float32),
              jnp.zeros((bsz, D_HEADS), jnp.float32))
        init = ((za, za), (zd, zd))
    (a0_f, a0_b), (d0_f, d0_b) = init

    q_a = split_heads(jax.nn.silu(f32(a_q)), A_HEADS)
    v_a = split_heads(f32(a_i), A_HEADS)
    lf_f, ka_f = hgrn2_gates(f32(a_ff), lb_f)
    lf_b, ka_b = hgrn2_gates(f32(a_fb), lb_b)
    oa_f, sa_f = gla_chunked(q_a, ka_f, v_a, lf_f, a0_f, need_out)
    oa_b, sa_b = gla_chunked(rev(q_a), rev(ka_b), rev(v_a), rev(lf_b), a0_b, need_out)

    q_d = split_heads(f32(d_q), D_HEADS)
    k_d = split_heads(f32(d_k), D_HEADS)
    if grid_pos is not None:
        q_d, k_d = rope_2d(q_d, *grid_pos), rope_2d(k_d, *grid_pos)
    q_d = q_d * HEAD_DIM ** -0.5
    v_d = split_heads(f32(d_v), D_HEADS)
    gate = lambda t: jnp.transpose(f32(t), (0, 2, 1))
    hd_f, sd_f = mlstm_chunked(q_d, k_d, v_d, jax.nn.log_sigmoid(gate(d_ff)), gate(d_if), d0_f, need_out)
    hd_b, sd_b = mlstm_chunked(rev(q_d), rev(k_d), rev(v_d), rev(jax.nn.log_sigmoid(gate(d_fb))),
                               rev(gate(d_ib)), d0_b, need_out)

    qn, kn, vn = split_heads(b_q, B_HEADS), split_heads(b_k, B_HEADS), split_heads(b_v, B_HEADS)
    states = ((sa_f, sa_b), (sd_f, sd_b))
    if not need_out:
        return None, states, (kn, vn)

    if ctx_kv is None:
        o_attn = context_attention(qn, kn, vn)
    else:
        o_attn = neighborhood_attention(qn, kn, vn, ctx_kv[0], ctx_kv[1], rpb)
    out_a = head_rmsnorm(oa_f + rev(oa_b), gn_a) * jax.nn.silu(f32(a_g))
    out_b = f32(merge_heads(o_attn))
    out_c = pool_mixer(c_x, w_pool, pool_s)
    out_d = jax.nn.sigmoid(f32(d_o)) * head_rmsnorm(hd_f + rev(hd_b), gn_d)
    y = jnp.concatenate([out_a, out_b, out_c, out_d], axis=-1).astype(p.dtype)
    return y @ w_out, states, (kn, vn)


def hier_moe(h, wg, bg, we, be, w1, w3, w2):
    b, l, d = h.shape
    hf = h.reshape(b * l, d)
    tok = jnp.arange(b * l)
    g_logits = (hf @ wg).astype(jnp.float32) + bg.astype(jnp.float32)
    grp = jnp.argmax(g_logits, axis=-1)
    p_grp = jax.nn.softmax(g_logits, axis=-1)[tok, grp][:, None]
    e_logits = ((hf @ we).astype(jnp.float32) + be.astype(jnp.float32)).reshape(-1, N_GROUPS, EXP_PER_GROUP)
    e_in = e_logits[tok, grp]
    top_v, top_i = lax.top_k(e_in, TOP_K)
    top_w = jax.nn.softmax(top_v, axis=-1) * p_grp
    eid = grp[:, None] * EXP_PER_GROUP + top_i
    comb = jnp.einsum('tk,tke->te', top_w, jax.nn.one_hot(eid, N_EXPERTS, dtype=jnp.float32)).astype(h.dtype)
    out = jnp.zeros_like(hf)
    for gi in range(N_GROUPS):
        sl = slice(gi * EXP_PER_GROUP, (gi + 1) * EXP_PER_GROUP)
        act = jax.nn.silu(jnp.einsum('td,edf->tef', hf, w1[sl])) * jnp.einsum('td,edf->tef', hf, w3[sl])
        out = out + jnp.einsum('tef,efd->td', act * comb[:, sl, None], w2[sl])
    return out.reshape(b, l, d)


def setup_inputs(seed: int = 0) -> dict:
    key = jax.random.key(seed)
    ks = jax.random.split(key, 25)

    def nrm(k, shape, scale):
        return jax.random.normal(k, shape, jnp.float32) * scale

    def gain(k, shape):
        return 1.0 + nrm(k, shape, 0.05)

    f_cols = D_GATE_OFF + jnp.concatenate([D_HEADS + jnp.arange(D_HEADS), 3 * D_HEADS + jnp.arange(D_HEADS)])
    return {
        "x": nrm(ks[0], (BATCH, SEQ, D_MODEL), 1.0),
        "c": nrm(ks[1], (BATCH, D_MODEL), 1.0),
        "ctx": nrm(ks[2], (BATCH, CTX_LEN, D_MODEL), 1.0),
        "c_ctx": nrm(ks[3], (D_MODEL,), 1.0),
        "ada_w": nrm(ks[4], (DEPTH, D_MODEL, 6 * D_MODEL), 0.5 * D_MODEL ** -0.5),
        "ada_b": nrm(ks[5], (DEPTH, 6 * D_MODEL), 0.05),
        "norm1_w": gain(ks[6], (DEPTH, D_MODEL)),
        "norm2_w": gain(ks[7], (DEPTH, D_MODEL)),
        "mix_w_in": nrm(ks[8], (DEPTH, D_MODEL, IN_COLS), D_MODEL ** -0.5),
        "mix_b_in": nrm(ks[9], (DEPTH, IN_COLS), 0.02).at[:, f_cols].add(MLSTM_F_BIAS),
        "hgrn_lb": nrm(ks[10], (2, DEPTH, GROUP_W), 0.5),
        "hgrn_gn": gain(ks[11], (DEPTH, GROUP_W)),
        "natten_rpb": nrm(ks[12], (DEPTH, B_HEADS, 2 * NA_MAX_ROWS - 1, 2 * NA_COLS - 1), 0.1),
        "pool_w": nrm(ks[13], (DEPTH, C_GROUPS, C_GW, C_GW), C_GW ** -0.5),
        "pool_scale": gain(ks[14], (DEPTH, GROUP_W)),
        "mlstm_gn": gain(ks[15], (DEPTH, GROUP_W)),
        "mix_w_out": nrm(ks[16], (DEPTH, D_MIX, D_MODEL), D_MIX ** -0.5),
        "moe_wg": nrm(ks[17], (DEPTH, D_MODEL, N_GROUPS), D_MODEL ** -0.5),
        "moe_bg": nrm(ks[18], (DEPTH, N_GROUPS), 0.01),
        "moe_we": nrm(ks[19], (DEPTH, D_MODEL, N_EXPERTS), D_MODEL ** -0.5),
        "moe_be": nrm(ks[20], (DEPTH, N_EXPERTS), 0.01),
        "moe_w1": nrm(ks[21], (DEPTH, N_EXPERTS, D_MODEL, D_FF), D_MODEL ** -0.5),
        "moe_w3": nrm(ks[22], (DEPTH, N_EXPERTS, D_MODEL, D_FF), D_MODEL ** -0.5),
        "moe_w2": nrm(ks[23], (DEPTH, N_EXPERTS, D_FF, D_MODEL), D_FF ** -0.5),
        "final_w": gain(ks[24], (D_MODEL,)),
    }


def reference(x, c, ctx, c_ctx, ada_w, ada_b, norm1_w, norm2_w, mix_w_in, mix_b_in,
              hgrn_lb, hgrn_gn, natten_rpb, pool_w, pool_scale, mlstm_gn, mix_w_out,
              moe_wg, moe_bg, moe_we, moe_be, moe_w1, moe_w3, moe_w2, final_w):
    n_tok = x.shape[1]
    pos = jnp.arange(n_tok)
    grid_pos = (pos // GRID_W, pos % GRID_W)
    lbs = hgrn_lower_bounds(hgrn_lb)
    cond_lat = jax.nn.silu(c)
    cond_ctx = jax.nn.silu(c_ctx)
    xl, xc = x, ctx
    for l in range(DEPTH):
        ctx_continues = l < DEPTH - 1
        lw = (lbs[0, l], lbs[1, l], hgrn_gn[l], natten_rpb[l], pool_w[l], pool_scale[l], mlstm_gn[l], mix_w_out[l])
        mw = (moe_wg[l], moe_bg[l], moe_we[l], moe_be[l], moe_w1[l], moe_w3[l], moe_w2[l])
        mod_l = jnp.split((cond_lat @ ada_w[l] + ada_b[l])[:, None, :], 6, axis=-1)
        mod_c = jnp.split(cond_ctx @ ada_w[l] + ada_b[l], 6, axis=-1)
        hc = modulate(rmsnorm(xc, norm1_w[l]), mod_c[0], mod_c[1])
        y_c, ctx_states, ctx_kv = token_mixers(hc @ mix_w_in[l] + mix_b_in[l], *lw, init=None, ctx_kv=None,
                                               grid_pos=None, need_out=ctx_continues)
        hl = modulate(rmsnorm(xl, norm1_w[l]), mod_l[0], mod_l[1])
        y_l, _, _ = token_mixers(hl @ mix_w_in[l] + mix_b_in[l], *lw, init=ctx_states, ctx_kv=ctx_kv,
                                 grid_pos=grid_pos, need_out=True)
        xl = xl + mod_l[2] * y_l
        xl = xl + mod_l[5] * hier_moe(modulate(rmsnorm(xl, norm2_w[l]), mod_l[3], mod_l[4]), *mw)
        if ctx_continues:
            xc = xc + mod_c[2] * y_c
            xc = xc + mod_c[5] * hier_moe(modulate(rmsnorm(xc, norm2_w[l]), mod_c[3], mod_c[4]), *mw)
    return rmsnorm(xl, final_w)
```

```python
import functools

import jax
import jax.numpy as jnp
import numpy as np
from jax import lax
from jax.experimental import pallas as pl
from jax.experimental.pallas import tpu as pltpu

D_MODEL = 4096
SEQ = 8192
DEPTH = 2
GRID_W = 64
CTX_LEN = 256
N_TOK = CTX_LEN + SEQ
GROUP_W = 1024
HEAD_DIM = 128
N_HEADS = GROUP_W // HEAD_DIM
C_GROUPS = 4
C_GW = GROUP_W // C_GROUPS
POOL_WINDOWS = (2, 4, 8, 16)
NA_MAX_ROWS = 8
NA_COLS = 16
A_CHUNK = 32
D_CHUNK = 64
ROPE_BASE = 10000.0
N_GROUPS = 4
EXP_PER_GROUP = 8
N_EXPERTS = N_GROUPS * EXP_PER_GROUP
TOP_K = 2
D_FF = D_MODEL // 8
EPS = 1e-6
N_MAIN = 13 * GROUP_W
N_GATE = 4 * N_HEADS
LANES = 128
VMEM_LIMIT = 56 * 1024 * 1024

ROW_TILE = 256
MM_TM = 768
MM_TN = 512
MOE_TM = 256
MOE_TILES = (TOP_K * N_TOK + N_EXPERTS * (MOE_TM - 1)) // MOE_TM + 1
CMB_TB = 128


def _cparams(*sem):
    return pltpu.CompilerParams(dimension_semantics=sem, vmem_limit_bytes=VMEM_LIMIT)


def _ada_kernel(c_ref, w_ref, b_ref, o_ref):
    acc = jnp.dot(c_ref[...], w_ref[...].astype(jnp.bfloat16), preferred_element_type=jnp.float32)
    o_ref[...] = acc + b_ref[...]


def ada_modulation(cond, ada_w, ada_b):
    tn = 512
    n = ada_w.shape[-1]
    rows = cond.shape[0]
    return pl.pallas_call(
        _ada_kernel,
        grid=(DEPTH, n // tn),
        in_specs=[pl.BlockSpec((rows, D_MODEL), lambda l, j: (0, 0)),
                  pl.BlockSpec((None, D_MODEL, tn), lambda l, j: (l, 0, j)),
                  pl.BlockSpec((None, 1, tn), lambda l, j: (l, 0, j))],
        out_specs=pl.BlockSpec((None, rows, tn), lambda l, j: (l, 0, j)),
        out_shape=jax.ShapeDtypeStruct((DEPTH, rows, n), jnp.float32),
        compiler_params=_cparams("arbitrary", "arbitrary"),
        name="ada_modulation",
    )(cond, ada_w, ada_b.reshape(DEPTH, 1, n))


def _norm_kernel(x_ref, w_ref, mod_ref, o_ref, *, shift_row, scale_row):
    x = x_ref[...]
    y = x * lax.rsqrt(jnp.mean(x * x, axis=-1, keepdims=True) + EPS) * w_ref[...]
    if shift_row is not None:
        y = y * (1.0 + mod_ref[scale_row:scale_row + 1, :]) + mod_ref[shift_row:shift_row + 1, :]
    o_ref[...] = y.astype(o_ref.dtype)


def _norm_router_kernel(x_ref, w_ref, mod_ref, wr_ref, o_ref, lg_ref, *, shift_row, scale_row):
    x = x_ref[...]
    y = x * lax.rsqrt(jnp.mean(x * x, axis=-1, keepdims=True) + EPS) * w_ref[...]
    y = y * (1.0 + mod_ref[scale_row:scale_row + 1, :]) + mod_ref[shift_row:shift_row + 1, :]
    o_ref[...] = y.astype(o_ref.dtype)
    y_hi = y.astype(jnp.bfloat16)
    y_lo = (y - y_hi.astype(jnp.float32)).astype(jnp.bfloat16)
    w = wr_ref[...]
    w_hi = w.astype(jnp.bfloat16)
    w_lo = (w - w_hi.astype(jnp.float32)).astype(jnp.bfloat16)
    dot = functools.partial(jnp.dot, preferred_element_type=jnp.float32)
    lg_ref[...] = dot(y_hi, w_hi) + (dot(y_lo, w_hi) + dot(y_hi, w_lo))


def _tok_type(i):
    return jnp.where(i * ROW_TILE >= CTX_LEN, 1, 0)


def norm_modulate(x, w, mod, which, out_dtype=jnp.bfloat16):
    n = x.shape[0]
    rows = (None, None) if which is None else (3 * which, 3 * which + 1)
    return pl.pallas_call(
        functools.partial(_norm_kernel, shift_row=rows[0], scale_row=rows[1]),
        grid=(n // ROW_TILE,),
        in_specs=[pl.BlockSpec((ROW_TILE, D_MODEL), lambda i: (i, 0)),
                  pl.BlockSpec((1, D_MODEL), lambda i: (0, 0)),
                  pl.BlockSpec((None, 6, D_MODEL), lambda i: (_tok_type(i), 0, 0))],
        out_specs=pl.BlockSpec((ROW_TILE, D_MODEL), lambda i: (i, 0)),
        out_shape=jax.ShapeDtypeStruct((n, D_MODEL), out_dtype),
        compiler_params=_cparams("arbitrary"),
        name="norm_modulate",
    )(x, w.reshape(1, D_MODEL), mod)


def norm_modulate_router(x, w, mod, w_router):
    n = x.shape[0]
    return pl.pallas_call(
        functools.partial(_norm_router_kernel, shift_row=3, scale_row=4),
        grid=(n // ROW_TILE,),
        in_specs=[pl.BlockSpec((ROW_TILE, D_MODEL), lambda i: (i, 0)),
                  pl.BlockSpec((1, D_MODEL), lambda i: (0, 0)),
                  pl.BlockSpec((None, 6, D_MODEL), lambda i: (_tok_type(i), 0, 0)),
                  pl.BlockSpec((D_MODEL, LANES), lambda i: (0, 0))],
        out_specs=[pl.BlockSpec((ROW_TILE, D_MODEL), lambda i: (i, 0)),
                   pl.BlockSpec((ROW_TILE, LANES), lambda i: (i, 0))],
        out_shape=[jax.ShapeDtypeStruct((n, D_MODEL), jnp.float32),
                   jax.ShapeDtypeStruct((n, LANES), jnp.float32)],
        compiler_params=_cparams("arbitrary"),
        name="norm_modulate_router",
    )(x, w.reshape(1, D_MODEL), mod, w_router)


def _mm_bias_kernel(a_ref, b_ref, bias_ref, o_ref):
    o_ref[...] = jnp.dot(a_ref[...], b_ref[...], preferred_element_type=jnp.float32) + bias_ref[...]


def _mm_resid_kernel(a_ref, b_ref, x_ref, mod_ref, o_ref, *, gate_row):
    acc = jnp.dot(a_ref[...], b_ref[...], preferred_element_type=jnp.float32)
    tm = a_ref.shape[0]
    row = pl.program_id(0) * tm + lax.broadcasted_iota(jnp.int32, (tm, 1), 0)
    gate = jnp.where(row < CTX_LEN, mod_ref[0, gate_row:gate_row + 1, :], mod_ref[1, gate_row:gate_row + 1, :])
    o_ref[...] = x_ref[...] + gate * acc


def matmul_bias(a, b, bias, tm, tn):
    m, k = a.shape
    n = b.shape[1]
    return pl.pallas_call(
        _mm_bias_kernel,
        grid=(m // tm, n // tn),
        in_specs=[pl.BlockSpec((tm, k), lambda i, j: (i, 0)),
                  pl.BlockSpec((k, tn), lambda i, j: (0, j)),
                  pl.BlockSpec((1, tn), lambda i, j: (0, j))],
        out_specs=pl.BlockSpec((tm, tn), lambda i, j: (i, j)),
        out_shape=jax.ShapeDtypeStruct((m, n), jnp.float32),
        compiler_params=_cparams("arbitrary", "arbitrary"),
        name="matmul_bias",
    )(a, b, bias.reshape(1, n))


def matmul_gated_residual(a, b, x, mod, gate_row):
    m, k = a.shape
    n = b.shape[1]
    tm, tn = MM_TM, MM_TN
    return pl.pallas_call(
        functools.partial(_mm_resid_kernel, gate_row=gate_row),
        grid=(m // tm, n // tn),
        in_specs=[pl.BlockSpec((tm, k), lambda i, j: (i, 0)),
                  pl.BlockSpec((k, tn), lambda i, j: (0, j)),
                  pl.BlockSpec((tm, tn), lambda i, j: (i, j)),
                  pl.BlockSpec((2, 6, tn), lambda i, j: (0, 0, j))],
        out_specs=pl.BlockSpec((tm, tn), lambda i, j: (i, j)),
        out_shape=jax.ShapeDtypeStruct((m, n), jnp.float32),
        compiler_params=_cparams("arbitrary", "arbitrary"),
        name="matmul_gated_residual",
    )(a, b, x, mod)


def _moe_kernel(te_ref, tv_ref, tb_ref, rt_ref, h_hbm, rw_ref, w1_ref, w3_ref, w2_ref, y_ref, xg_ref, sem_ref):
    del te_ref, tb_ref
    i = pl.program_id(0)
    n_tiles = pl.num_programs(0)

    def row_copy(tile, slot, r):
        tok = rt_ref[tile * MOE_TM + r]
        return pltpu.make_async_copy(h_hbm.at[pl.ds(tok, 1), :], xg_ref.at[slot, pl.ds(r, 1), :], sem_ref.at[slot])

    def start_gather(tile, slot):
        @pl.when(tv_ref[tile] != 0)
        def _():
            def body(r, carry):
                row_copy(tile, slot, r).start()
                return carry
            lax.fori_loop(0, MOE_TM, body, 0)

    @pl.when(i == 0)
    def _():
        start_gather(0, 0)

    @pl.when(i + 1 < n_tiles)
    def _():
        start_gather(i + 1, (i + 1) % 2)

    @pl.when(tv_ref[i] != 0)
    def _():
        slot = i % 2

        def wait_body(r, carry):
            row_copy(i, slot, r).wait()
            return carry
        lax.fori_loop(0, MOE_TM, wait_body, 0)
        xg = xg_ref[slot].astype(jnp.bfloat16)
        a = jnp.dot(xg, w1_ref[...], preferred_element_type=jnp.float32)
        b = jnp.dot(xg, w3_ref[...], preferred_element_type=jnp.float32)
        act = (a * jax.nn.sigmoid(a)) * b * rw_ref[...]
        y_ref[...] = jnp.dot(act.astype(jnp.bfloat16), w2_ref[...], preferred_element_type=jnp.float32)

    @pl.when(tv_ref[i] == 0)
    def _():
        y_ref[...] = jnp.zeros_like(y_ref)


def moe_grouped(h, tile_expert, tile_valid, tile_block, row_token, row_w, w1, w3, w2):
    grid_spec = pltpu.PrefetchScalarGridSpec(
        num_scalar_prefetch=4,
        grid=(MOE_TILES,),
        in_specs=[pl.BlockSpec(memory_space=pl.ANY),
                  pl.BlockSpec((MOE_TM, 1), lambda i, te, tv, tb, rt: (tb[i], 0)),
                  pl.BlockSpec((None, D_MODEL, D_FF), lambda i, te, tv, tb, rt: (te[i], 0, 0)),
                  pl.BlockSpec((None, D_MODEL, D_FF), lambda i, te, tv, tb, rt: (te[i], 0, 0)),
                  pl.BlockSpec((None, D_FF, D_MODEL), lambda i, te, tv, tb, rt: (te[i], 0, 0))],
        out_specs=pl.BlockSpec((MOE_TM, D_MODEL), lambda i, te, tv, tb, rt: (i, 0)),
        scratch_shapes=[pltpu.VMEM((2, MOE_TM, D_MODEL), jnp.float32), pltpu.SemaphoreType.DMA((2,))],
    )
    return pl.pallas_call(
        _moe_kernel,
        grid_spec=grid_spec,
        out_shape=jax.ShapeDtypeStruct((MOE_TILES * MOE_TM, D_MODEL), jnp.float32),
        compiler_params=_cparams("arbitrary"),
        name="moe_grouped",
    )(tile_expert, tile_valid, tile_block, row_token, h, row_w, w1, w3, w2)


def _combine_kernel(pos_ref, y_hbm, x_ref, mod_ref, o_ref, buf_ref, sem_ref, *, gate_row):
    i = pl.program_id(0)
    n_steps = pl.num_programs(0)

    def row_copy(step, slot, r):
        src = pos_ref[step * (TOP_K * CMB_TB) + r]
        return pltpu.make_async_copy(y_hbm.at[pl.ds(src, 1), :], buf_ref.at[slot, pl.ds(r, 1), :], sem_ref.at[slot])

    def start_gather(step, slot):
        def body(r, carry):
            row_copy(step, slot, r).start()
            return carry
        lax.fori_loop(0, TOP_K * CMB_TB, body, 0)

    @pl.when(i == 0)
    def _():
        start_gather(0, 0)

    @pl.when(i + 1 < n_steps)
    def _():
        start_gather(i + 1, (i + 1) % 2)

    slot = i % 2

    def wait_body(r, carry):
        row_copy(i, slot, r).wait()
        return carry
    lax.fori_loop(0, TOP_K * CMB_TB, wait_body, 0)
    moe = buf_ref[slot, 0:CMB_TB, :] + buf_ref[slot, CMB_TB:2 * CMB_TB, :]
    o_ref[...] = x_ref[...] + mod_ref[gate_row:gate_row + 1, :] * moe


def moe_combine(pos, ys, x, mod, gate_row):
    n = x.shape[0]
    grid_spec = pltpu.PrefetchScalarGridSpec(
        num_scalar_prefetch=1,
        grid=(n // CMB_TB,),
        in_specs=[pl.BlockSpec(memory_space=pl.ANY),
                  pl.BlockSpec((CMB_TB, D_MODEL), lambda i, pos: (i, 0)),
                  pl.BlockSpec((None, 6, D_MODEL), lambda i, pos: (jnp.where(i * CMB_TB >= CTX_LEN, 1, 0), 0, 0))],
        out_specs=pl.BlockSpec((CMB_TB, D_MODEL), lambda i, pos: (i, 0)),
        scratch_shapes=[pltpu.VMEM((2, TOP_K * CMB_TB, D_MODEL), jnp.float32), pltpu.SemaphoreType.DMA((2,))],
    )
    return pl.pallas_call(
        functools.partial(_combine_kernel, gate_row=gate_row),
        grid_spec=grid_spec,
        out_shape=jax.ShapeDtypeStruct((n, D_MODEL), jnp.float32),
        compiler_params=_cparams("arbitrary"),
        name="moe_combine",
    )(pos, ys, x, mod)


def moe_route(logits, bg, be):
    n = logits.shape[0]
    tok = jnp.arange(n)
    g_logits = logits[:, :N_GROUPS] + bg
    grp = jnp.argmax(g_logits, axis=-1)
    p_grp = jax.nn.softmax(g_logits, axis=-1)[tok, grp][:, None]
    e_logits = (logits[:, N_GROUPS:N_GROUPS + N_EXPERTS] + be).reshape(n, N_GROUPS, EXP_PER_GROUP)
    top_v, top_i = lax.top_k(e_logits[tok, grp], TOP_K)
    top_w = jax.nn.softmax(top_v, axis=-1) * p_grp
    eid = grp[:, None] * EXP_PER_GROUP + top_i
    return eid.astype(jnp.int32), top_w


def moe_schedule(eid, top_w):
    n = eid.shape[0]
    flat_e = eid.reshape(-1)
    order = jnp.argsort(flat_e, stable=True).astype(jnp.int32)
    e_sorted = flat_e[order]
    counts = jnp.zeros((N_EXPERTS,), jnp.int32).at[flat_e].add(1)
    padded = ((counts + MOE_TM - 1) // MOE_TM) * MOE_TM
    pad_end = jnp.cumsum(padded)
    pad_start = pad_end - padded
    start = jnp.cumsum(counts) - counts
    dest = pad_start[e_sorted] + (jnp.arange(TOP_K * n, dtype=jnp.int32) - start[e_sorted])
    n_rows = MOE_TILES * MOE_TM
    row_token = jnp.zeros((n_rows,), jnp.int32).at[dest].set(order // TOP_K)
    row_w = jnp.zeros((n_rows,), jnp.float32).at[dest].set(top_w.reshape(-1)[order])
    pos = jnp.zeros((TOP_K * n,), jnp.int32).at[order].set(dest)
    tile_row0 = jnp.arange(MOE_TILES, dtype=jnp.int32) * MOE_TM
    n_valid = pad_end[-1] // MOE_TM
    tile_valid = (tile_row0 < pad_end[-1]).astype(jnp.int32)
    tile_expert = jnp.minimum(jnp.searchsorted(pad_end, tile_row0, side="right"), N_EXPERTS - 1).astype(jnp.int32)
    last_expert = tile_expert[jnp.maximum(n_valid - 1, 0)]
    tile_expert = jnp.where(tile_valid != 0, tile_expert, last_expert)
    tile_block = jnp.minimum(jnp.arange(MOE_TILES, dtype=jnp.int32), n_valid - 1)
    pos = pos.reshape(n // CMB_TB, CMB_TB, TOP_K).transpose(0, 2, 1).reshape(-1)
    return tile_expert, tile_valid, tile_block, row_token, row_w.reshape(n_rows, 1), pos


def split_heads(t, n_heads):
    b, l, _ = t.shape
    return t.reshape(b, l, n_heads, -1).transpose(0, 2, 1, 3)


def merge_heads(t):
    b, h, l, d = t.shape
    return t.transpose(0, 2, 1, 3).reshape(b, l, h * d)


def head_rmsnorm(o, w):
    o = o * lax.rsqrt(jnp.mean(o * o, axis=-1, keepdims=True) + EPS)
    return merge_heads(o) * w.astype(jnp.float32)


def rev(t):
    return jnp.flip(t, axis=2)


def rope_2d(t, row, col):
    quarter = HEAD_DIM // 4
    half = HEAD_DIM // 2
    inv_freq = ROPE_BASE ** (-jnp.arange(quarter, dtype=jnp.float32) / quarter)

    def rot(u, p):
        ang = p.astype(jnp.float32)[:, None] * inv_freq
        cos, sin = jnp.cos(ang), jnp.sin(ang)
        u1, u2 = u[..., :quarter], u[..., quarter:]
        return jnp.concatenate([u1 * cos - u2 * sin, u2 * cos + u1 * sin], axis=-1)

    return jnp.concatenate([rot(t[..., :half], row), rot(t[..., half:], col)], axis=-1)


def hgrn_lower_bounds(p):
    cs = jnp.cumsum(jax.nn.softmax(p.astype(jnp.float32), axis=1), axis=1)
    return cs - cs[:, :1]


def hgrn2_gates(f_logits, lb):
    log_f = jnp.logaddexp(jnp.log(lb), jnp.log1p(-lb) + jax.nn.log_sigmoid(f_logits))
    k = (1.0 - lb) * jax.nn.sigmoid(-f_logits)
    return split_heads(log_f, N_HEADS), split_heads(k, N_HEADS)


def gla_chunked(q, k, v, log_f, s0, need_out):
    bsz, h, l, dk = q.shape
    n = l // A_CHUNK
    blk = lambda t: t.reshape(bsz, h, n, A_CHUNK, t.shape[-1])
    q, k, v, log_f = blk(q), blk(k), blk(v), blk(log_f)
    b = jnp.cumsum(log_f, axis=3)
    b_end = b[:, :, :, -1:, :]
    delta = jnp.einsum('bhncd,bhnce->bhnde', k * jnp.exp(b_end - b), v)
    decay = jnp.exp(b_end[:, :, :, 0, :])

    def step(s, inp):
        dec, dlt = inp
        return dec[..., None] * s + dlt, s

    s_fin, s_prev = lax.scan(step, s0, (jnp.moveaxis(decay, 2, 0), jnp.moveaxis(delta, 2, 0)))
    if not need_out:
        return None, s_fin
    s_prev = jnp.moveaxis(s_prev, 0, 2)
    q_dec = q * jnp.exp(b)
    scores = jnp.einsum('bhnid,bhnjd->bhnij', q_dec, k * jnp.exp(-b))
    lower = jnp.tril(jnp.ones((A_CHUNK, A_CHUNK), dtype=bool))
    scores = jnp.where(lower, scores, 0.0)
    o = jnp.einsum('bhnij,bhnje->bhnie', scores, v) + jnp.einsum('bhnid,bhnde->bhnie', q_dec, s_prev)
    return o.reshape(bsz, h, l, -1), s_fin


def mlstm_chunked(q, k, v, log_f, log_i, state0, need_out):
    bsz, h, l, dk = q.shape
    n = l // D_CHUNK
    blk = lambda t: t.reshape(bsz, h, n, D_CHUNK, *t.shape[3:])
    q, k, v, log_f, log_i = blk(q), blk(k), blk(v), blk(log_f), blk(log_i)
    b = jnp.cumsum(log_f, axis=-1)
    b_end = b[..., -1]
    g = b_end[..., None] - b + log_i
    m_loc = jnp.max(g, axis=-1)
    w_end = jnp.exp(g - m_loc[..., None])
    d_s = jnp.einsum('bhnc,bhncd,bhnce->bhnde', w_end, k, v)
    d_n = jnp.einsum('bhnc,bhncd->bhnd', w_end, k)

    def step(carry, inp):
        s, nv, m = carry
        be, ml, ds, dn = inp
        m_new = jnp.maximum(be + m, ml)
        a = jnp.exp(be + m - m_new)
        cc = jnp.exp(ml - m_new)
        new = (a[..., None, None] * s + cc[..., None, None] * ds, a[..., None] * nv + cc[..., None] * dn, m_new)
        return new, (s, nv, m)

    mv = lambda t: jnp.moveaxis(t, 2, 0)
    fin, prev = lax.scan(step, state0, (mv(b_end), mv(m_loc), mv(d_s), mv(d_n)))
    if not need_out:
        return None, fin
    s_prev, n_prev, m_prev = (jnp.moveaxis(t, 0, 2) for t in prev)
    lower = jnp.tril(jnp.ones((D_CHUNK, D_CHUNK), dtype=bool))
    dmat = jnp.where(lower, b[..., :, None] - b[..., None, :] + log_i[..., None, :], -jnp.inf)
    m_inter = b + m_prev[..., None]
    m_t = jnp.maximum(m_inter, jnp.max(dmat, axis=-1))
    w = jnp.exp(dmat - m_t[..., None]) * jnp.einsum('bhnid,bhnjd->bhnij', q, k)
    a = jnp.exp(m_inter - m_t)
    num = jnp.einsum('bhnij,bhnje->bhnie', w, v) + a[..., None] * jnp.einsum('bhnid,bhnde->bhnie', q, s_prev)
    den = jnp.sum(w, axis=-1) + a * jnp.einsum('bhnid,bhnd->bhni', q, n_prev)
    hh = num / jnp.maximum(jnp.abs(den), jnp.exp(-m_t))[..., None]
    return hh.reshape(bsz, h, l, -1), fin


def context_attention(q, k, v):
    s = jnp.einsum('bhid,bhjd->bhij', q, k).astype(jnp.float32) * HEAD_DIM ** -0.5
    p = jax.nn.softmax(s, axis=-1).astype(v.dtype)
    return jnp.einsum('bhij,bhjd->bhid', p, v)


def neighborhood_attention(q, k, v, k_ctx, v_ctx, rpb):
    bsz, h, s, hd = q.shape
    rows = s // GRID_W
    kh = min(NA_MAX_ROWS, rows)
    grid = lambda t: t.reshape(bsz, h, rows, GRID_W, hd)
    qg, kg, vg = grid(q), grid(k), grid(v)
    r = jnp.arange(rows)
    key_rows = jnp.clip(r - kh // 2, 0, rows - kh)[:, None] + jnp.arange(kh)[None, :]
    kb, vb = kg[:, :, key_rows], vg[:, :, key_rows]
    cols = jnp.arange(GRID_W)
    c0 = jnp.clip(cols - NA_COLS // 2, 0, GRID_W - NA_COLS)
    in_win = (cols[None, :] >= c0[:, None]) & (cols[None, :] < c0[:, None] + NA_COLS)
    dr = key_rows - r[:, None] + (NA_MAX_ROWS - 1)
    dc = jnp.clip(cols[None, :] - cols[:, None] + (NA_COLS - 1), 0, 2 * NA_COLS - 2)
    bias = rpb[:, dr[:, None, :, None], dc[None, :, None, :]]
    scale = hd ** -0.5
    s_loc = jnp.einsum('bhrqd,bhrkwd->bhrqkw', qg, kb).astype(jnp.float32) * scale + bias.astype(jnp.float32)
    s_loc = jnp.where(in_win[:, None, :], s_loc, -jnp.inf)
    s_loc = s_loc.reshape(bsz, h, rows, GRID_W, kh * GRID_W)
    s_ctx = jnp.einsum('bhrqd,bhcd->bhrqc', qg, k_ctx).astype(jnp.float32) * scale
    p = jax.nn.softmax(jnp.concatenate([s_loc, s_ctx], axis=-1), axis=-1).astype(v.dtype)
    p_loc = p[..., :kh * GRID_W].reshape(bsz, h, rows, GRID_W, kh, GRID_W)
    p_ctx = p[..., kh * GRID_W:]
    o = jnp.einsum('bhrqkw,bhrkwd->bhrqd', p_loc, vb) + jnp.einsum('bhrqc,bhcd->bhrqd', p_ctx, v_ctx)
    return o.reshape(bsz, h, s, hd)


def pool_mixer(u, w_pool, scale):
    b, l, _ = u.shape
    us = u.astype(jnp.float32).reshape(b, l, C_GROUPS, C_GW)
    csum = jnp.concatenate([jnp.zeros((b, 1, C_GROUPS, C_GW), jnp.float32), jnp.cumsum(us, axis=1)], axis=1)
    t = jnp.arange(l)[:, None]
    half = jnp.array(POOL_WINDOWS, dtype=jnp.int32)[None, :] // 2
    lo = jnp.clip(t - half, 0, l)
    hi = jnp.clip(t + half, 0, l)
    g = jnp.arange(C_GROUPS)[None, :]
    mean = (csum[:, hi, g] - csum[:, lo, g]) / (hi - lo).astype(jnp.float32)[None, :, :, None]
    y = jnp.einsum('blgc,gce->blge', mean - us, w_pool.astype(jnp.float32))
    return y.reshape(b, l, GROUP_W) * scale.astype(jnp.float32)


def token_mixers(p, pg, lb_f, lb_b, gn_a, rpb, w_pool, pool_s, gn_d, init, ctx_kv, grid_pos, need_out):
    (a_q, a_i, a_g, a_ff, a_fb, b_q, b_k, b_v, c_x, d_q, d_k, d_v, d_o) = jnp.split(p, 13, axis=-1)
    d_if, d_ff, d_ib, d_fb = jnp.split(pg, 4, axis=-1)
    bsz = p.shape[0]
    f32 = lambda t: t.astype(jnp.float32)
    if init is None:
        za = jnp.zeros((bsz, N_HEADS, HEAD_DIM, HEAD_DIM), jnp.float32)
        zd = (jnp.zeros((bsz, N_HEADS, HEAD_DIM, HEAD_DIM), jnp.float32),
              jnp.zeros((bsz, N_HEADS, HEAD_DIM), jnp.float32),
              jnp.zeros((bsz, N_HEADS), jnp.float32))
        init = ((za, za), (zd, zd))
    (a0_f, a0_b), (d0_f, d0_b) = init

    q_a = split_heads(jax.nn.silu(f32(a_q)), N_HEADS)
    v_a = split_heads(f32(a_i), N_HEADS)
    lf_f, ka_f = hgrn2_gates(f32(a_ff), lb_f)
    lf_b, ka_b = hgrn2_gates(f32(a_fb), lb_b)
    oa_f, sa_f = gla_chunked(q_a, ka_f, v_a, lf_f, a0_f, need_out)
    oa_b, sa_b = gla_chunked(rev(q_a), rev(ka_b), rev(v_a), rev(lf_b), a0_b, need_out)

    q_d = split_heads(f32(d_q), N_HEADS)
    k_d = split_heads(f32(d_k), N_HEADS)
    if grid_pos is not None:
        q_d, k_d = rope_2d(q_d, *grid_pos), rope_2d(k_d, *grid_pos)
    q_d = q_d * HEAD_DIM ** -0.5
    v_d = split_heads(f32(d_v), N_HEADS)
    gate = lambda t: jnp.transpose(f32(t), (0, 2, 1))
    hd_f, sd_f = mlstm_chunked(q_d, k_d, v_d, jax.nn.log_sigmoid(gate(d_ff)), gate(d_if), d0_f, need_out)
    hd_b, sd_b = mlstm_chunked(rev(q_d), rev(k_d), rev(v_d), rev(jax.nn.log_sigmoid(gate(d_fb))),
                               rev(gate(d_ib)), d0_b, need_out)

    qn, kn, vn = split_heads(b_q, N_HEADS), split_heads(b_k, N_HEADS), split_heads(b_v, N_HEADS)
    states = ((sa_f, sa_b), (sd_f, sd_b))
    if not need_out:
        return None, states, (kn, vn)

    if ctx_kv is None:
        o_attn = context_attention(qn, kn, vn)
    else:
        o_attn = neighborhood_attention(qn, kn, vn, ctx_kv[0], ctx_kv[1], rpb)
    out_a = head_rmsnorm(oa_f + rev(oa_b), gn_a) * jax.nn.silu(f32(a_g))
    out_b = f32(merge_heads(o_attn))
    out_c = pool_mixer(c_x, w_pool, pool_s)
    out_d = jax.nn.sigmoid(f32(d_o)) * head_rmsnorm(hd_f + rev(hd_b), gn_d)
    y = jnp.concatenate([out_a, out_b, out_c, out_d], axis=-1).astype(jnp.bfloat16)
    return y, states, (kn, vn)


def kernel(x, c, ctx, c_ctx, ada_w, ada_b, norm1_w, norm2_w, mix_w_in, mix_b_in, hgrn_lb, hgrn_gn, natten_rpb,
           pool_w, pool_scale, mlstm_gn, mix_w_out, moe_wg, moe_bg, moe_we, moe_be, moe_w1, moe_w3, moe_w2, final_w):
    assert x.shape == (1, SEQ, D_MODEL) and ctx.shape == (1, CTX_LEN, D_MODEL)
    bf16 = jnp.bfloat16
    pos = jnp.arange(SEQ)
    grid_pos = (pos // GRID_W, pos % GRID_W)
    lbs = hgrn_lower_bounds(hgrn_lb)

    cond = jnp.zeros((16, D_MODEL), jnp.float32).at[0].set(jax.nn.silu(c_ctx)).at[1].set(jax.nn.silu(c[0]))
    mods = ada_modulation(cond.astype(bf16), ada_w, ada_b)[:, :2].reshape(DEPTH, 2, 6, D_MODEL)

    xs = jnp.concatenate([ctx[0], x[0]], axis=0)
    for l in range(DEPTH):
        ctx_continues = l < DEPTH - 1
        mod = mods[l]
        w_in = mix_w_in[l].astype(bf16)
        h = norm_modulate(xs, norm1_w[l], mod, 0)
        p_main = matmul_bias(h, w_in[:, :N_MAIN], mix_b_in[l, :N_MAIN], MM_TM, MM_TN)
        w_gate = jnp.pad(w_in[:, N_MAIN:], ((0, 0), (0, LANES - N_GATE)))
        b_gate = jnp.pad(mix_b_in[l, N_MAIN:], (0, LANES - N_GATE))
        p_gate = matmul_bias(h, w_gate, b_gate, MM_TM, LANES)[:, :N_GATE]

        lw = (lbs[0, l], lbs[1, l], hgrn_gn[l], natten_rpb[l], pool_w[l], pool_scale[l], mlstm_gn[l])
        y_c, ctx_states, ctx_kv = token_mixers(p_main[None, :CTX_LEN], p_gate[None, :CTX_LEN], *lw, init=None,
                                               ctx_kv=None, grid_pos=None, need_out=ctx_continues)
        y_l, _, _ = token_mixers(p_main[None, CTX_LEN:], p_gate[None, CTX_LEN:], *lw, init=ctx_states,
                                 ctx_kv=ctx_kv, grid_pos=grid_pos, need_out=True)
        if y_c is None:
            y_c = jnp.zeros((1, CTX_LEN, D_MODEL), bf16)
        y = jnp.concatenate([y_c[0], y_l[0]], axis=0)
        xs = matmul_gated_residual(y, mix_w_out[l].astype(bf16), xs, mod, 2)

        w_router = jnp.pad(jnp.concatenate([moe_wg[l], moe_we[l]], axis=1),
                           ((0, 0), (0, LANES - N_GROUPS - N_EXPERTS)))
        h2, logits = norm_modulate_router(xs, norm2_w[l], mod, w_router)
        eid, top_w = moe_route(logits, moe_bg[l], moe_be[l])
        tile_expert, tile_valid, tile_block, row_token, row_w, cpos = moe_schedule(eid, top_w)
        ys = moe_grouped(h2, tile_expert, tile_valid, tile_block, row_token, row_w,
                         moe_w1[l].astype(bf16), moe_w3[l].astype(bf16), moe_w2[l].astype(bf16))
        xs = moe_combine(cpos, ys, xs, mod, 5)

    out = norm_modulate(xs, final_w, mods[0], None, out_dtype=jnp.float32)
    return out[CTX_LEN:][None]
```

```python
import functools

import jax
import jax.numpy as jnp
from jax import lax
from jax.experimental import pallas as pl
from jax.experimental.pallas import tpu as pltpu

D_MODEL = 4096
SEQ = 8192
DEPTH = 2
GRID_W = 64
CTX_LEN = 256
N_TOK = CTX_LEN + SEQ
GROUP_W = 1024
HEAD_DIM = 128
N_HEADS = GROUP_W // HEAD_DIM
C_GROUPS = 4
C_GW = GROUP_W // C_GROUPS
POOL_WINDOWS = (2, 4, 8, 16)
NA_MAX_ROWS = 8
NA_COLS = 16
A_CHUNK = 32
D_CHUNK = 64
ROPE_BASE = 10000.0
N_GROUPS = 4
EXP_PER_GROUP = 8
N_EXPERTS = N_GROUPS * EXP_PER_GROUP
TOP_K = 2
D_FF = D_MODEL // 8
EPS = 1e-6
N_MAIN = 13 * GROUP_W
N_GATE = 4 * N_HEADS
LANES = 128
VMEM_LIMIT = 56 * 1024 * 1024

ROW_TILE = 256
MM_TM = 768
MM_TN = 512
MOE_TM = 256
MOE_TILES = (TOP_K * N_TOK + N_EXPERTS * (MOE_TM - 1)) // MOE_TM + 1
CMB_TB = 128

COL_AQ, COL_AI, COL_AG, COL_AFF, COL_AFB = 0, 1, 2, 3, 4
COL_Q, COL_K, COL_V, COL_C = 5, 6, 7, 8
COL_DQ, COL_DK, COL_DV, COL_DO = 9, 10, 11, 12
GATE_I_F, GATE_F_F, GATE_I_B, GATE_F_B = 0, 1, 2, 3
HPG = GROUP_W // HEAD_DIM


def _cparams(*sem):
    return pltpu.CompilerParams(dimension_semantics=sem, vmem_limit_bytes=VMEM_LIMIT)


def _ada_kernel(c_ref, w_ref, b_ref, o_ref):
    acc = jnp.dot(c_ref[...], w_ref[...].astype(jnp.bfloat16), preferred_element_type=jnp.float32)
    o_ref[...] = acc + b_ref[...]


def ada_modulation(cond, ada_w, ada_b):
    tn = 512
    n = ada_w.shape[-1]
    rows = cond.shape[0]
    return pl.pallas_call(
        _ada_kernel,
        grid=(DEPTH, n // tn),
        in_specs=[pl.BlockSpec((rows, D_MODEL), lambda l, j: (0, 0)),
                  pl.BlockSpec((None, D_MODEL, tn), lambda l, j: (l, 0, j)),
                  pl.BlockSpec((None, 1, tn), lambda l, j: (l, 0, j))],
        out_specs=pl.BlockSpec((None, rows, tn), lambda l, j: (l, 0, j)),
        out_shape=jax.ShapeDtypeStruct((DEPTH, rows, n), jnp.float32),
        compiler_params=_cparams("arbitrary", "arbitrary"),
        name="ada_modulation",
    )(cond, ada_w, ada_b.reshape(DEPTH, 1, n))


def _norm_kernel(x_ref, w_ref, mod_ref, o_ref, *, shift_row, scale_row):
    x = x_ref[...]
    y = x * lax.rsqrt(jnp.mean(x * x, axis=-1, keepdims=True) + EPS) * w_ref[...]
    if shift_row is not None:
        y = y * (1.0 + mod_ref[scale_row:scale_row + 1, :]) + mod_ref[shift_row:shift_row + 1, :]
    o_ref[...] = y.astype(o_ref.dtype)


def _norm_router_kernel(x_ref, w_ref, mod_ref, wr_ref, o_ref, lg_ref, *, shift_row, scale_row):
    x = x_ref[...]
    y = x * lax.rsqrt(jnp.mean(x * x, axis=-1, keepdims=True) + EPS) * w_ref[...]
    y = y * (1.0 + mod_ref[scale_row:scale_row + 1, :]) + mod_ref[shift_row:shift_row + 1, :]
    o_ref[...] = y.astype(o_ref.dtype)
    y_hi = y.astype(jnp.bfloat16)
    y_lo = (y - y_hi.astype(jnp.float32)).astype(jnp.bfloat16)
    w = wr_ref[...]
    w_hi = w.astype(jnp.bfloat16)
    w_lo = (w - w_hi.astype(jnp.float32)).astype(jnp.bfloat16)
    dot = functools.partial(jnp.dot, preferred_element_type=jnp.float32)
    lg_ref[...] = dot(y_hi, w_hi) + (dot(y_lo, w_hi) + dot(y_hi, w_lo))


def _tok_type(i):
    return jnp.where(i * ROW_TILE >= CTX_LEN, 1, 0)


def norm_modulate(x, w, mod, which, out_dtype=jnp.bfloat16):
    n = x.shape[0]
    rows = (None, None) if which is None else (3 * which, 3 * which + 1)
    return pl.pallas_call(
        functools.partial(_norm_kernel, shift_row=rows[0], scale_row=rows[1]),
        grid=(n // ROW_TILE,),
        in_specs=[pl.BlockSpec((ROW_TILE, D_MODEL), lambda i: (i, 0)),
                  pl.BlockSpec((1, D_MODEL), lambda i: (0, 0)),
                  pl.BlockSpec((None, 6, D_MODEL), lambda i: (_tok_type(i), 0, 0))],
        out_specs=pl.BlockSpec((ROW_TILE, D_MODEL), lambda i: (i, 0)),
        out_shape=jax.ShapeDtypeStruct((n, D_MODEL), out_dtype),
        compiler_params=_cparams("arbitrary"),
        name="norm_modulate",
    )(x, w.reshape(1, D_MODEL), mod)


def norm_modulate_router(x, w, mod, w_router):
    n = x.shape[0]
    return pl.pallas_call(
        functools.partial(_norm_router_kernel, shift_row=3, scale_row=4),
        grid=(n // ROW_TILE,),
        in_specs=[pl.BlockSpec((ROW_TILE, D_MODEL), lambda i: (i, 0)),
                  pl.BlockSpec((1, D_MODEL), lambda i: (0, 0)),
                  pl.BlockSpec((None, 6, D_MODEL), lambda i: (_tok_type(i), 0, 0)),
                  pl.BlockSpec((D_MODEL, LANES), lambda i: (0, 0))],
        out_specs=[pl.BlockSpec((ROW_TILE, D_MODEL), lambda i: (i, 0)),
                   pl.BlockSpec((ROW_TILE, LANES), lambda i: (i, 0))],
        out_shape=[jax.ShapeDtypeStruct((n, D_MODEL), jnp.float32),
                   jax.ShapeDtypeStruct((n, LANES), jnp.float32)],
        compiler_params=_cparams("arbitrary"),
        name="norm_modulate_router",
    )(x, w.reshape(1, D_MODEL), mod, w_router)


def _mm_bias_kernel(a_ref, b_ref, bias_ref, o_ref):
    o_ref[...] = jnp.dot(a_ref[...], b_ref[...], preferred_element_type=jnp.float32) + bias_ref[...]


def _mm_resid_kernel(a0_ref, a1_ref, a2_ref, a3_ref, b_ref, x_ref, mod_ref, o_ref, *, gate_row):
    acc = None
    for g, a_ref in enumerate((a0_ref, a1_ref, a2_ref, a3_ref)):
        part = jnp.dot(a_ref[...], b_ref[g * GROUP_W:(g + 1) * GROUP_W, :], preferred_element_type=jnp.float32)
        acc = part if acc is None else acc + part
    tm = x_ref.shape[0]
    row = pl.program_id(0) * tm + lax.broadcasted_iota(jnp.int32, (tm, 1), 0)
    gate = jnp.where(row < CTX_LEN, mod_ref[0, gate_row:gate_row + 1, :], mod_ref[1, gate_row:gate_row + 1, :])
    o_ref[...] = x_ref[...] + gate * acc


def matmul_bias(a, b, bias, tm, tn):
    m, k = a.shape
    n = b.shape[1]
    return pl.pallas_call(
        _mm_bias_kernel,
        grid=(m // tm, n // tn),
        in_specs=[pl.BlockSpec((tm, k), lambda i, j: (i, 0)),
                  pl.BlockSpec((k, tn), lambda i, j: (0, j)),
                  pl.BlockSpec((1, tn), lambda i, j: (0, j))],
        out_specs=pl.BlockSpec((tm, tn), lambda i, j: (i, j)),
        out_shape=jax.ShapeDtypeStruct((m, n), jnp.float32),
        compiler_params=_cparams("arbitrary", "arbitrary"),
        name="matmul_bias",
    )(a, b, bias.reshape(1, n))


def matmul_gated_residual(parts, b, x, mod, gate_row):
    m, n = x.shape
    tm, tn = MM_TM, MM_TN
    a_spec = pl.BlockSpec((tm, GROUP_W), lambda i, j: (i, 0))
    return pl.pallas_call(
        functools.partial(_mm_resid_kernel, gate_row=gate_row),
        grid=(m // tm, n // tn),
        in_specs=[a_spec, a_spec, a_spec, a_spec,
                  pl.BlockSpec((b.shape[0], tn), lambda i, j: (0, j)),
                  pl.BlockSpec((tm, tn), lambda i, j: (i, j)),
                  pl.BlockSpec((2, 6, tn), lambda i, j: (0, 0, j))],
        out_specs=pl.BlockSpec((tm, tn), lambda i, j: (i, j)),
        out_shape=jax.ShapeDtypeStruct((m, n), jnp.float32),
        compiler_params=_cparams("arbitrary", "arbitrary"),
        name="matmul_gated_residual",
    )(*parts, b, x, mod)


def _moe_kernel(te_ref, tv_ref, tb_ref, rt_ref, h_hbm, rw_ref, w1_ref, w3_ref, w2_ref, y_ref, xg_ref, sem_ref):
    del te_ref, tb_ref
    i = pl.program_id(0)
    n_tiles = pl.num_programs(0)

    def row_copy(tile, slot, r):
        tok = rt_ref[tile * MOE_TM + r]
        return pltpu.make_async_copy(h_hbm.at[pl.ds(tok, 1), :], xg_ref.at[slot, pl.ds(r, 1), :], sem_ref.at[slot])

    def start_gather(tile, slot):
        @pl.when(tv_ref[tile] != 0)
        def _():
            def body(r, carry):
                row_copy(tile, slot, r).start()
                return carry
            lax.fori_loop(0, MOE_TM, body, 0)

    @pl.when(i == 0)
    def _():
        start_gather(0, 0)

    @pl.when(i + 1 < n_tiles)
    def _():
        start_gather(i + 1, (i + 1) % 2)

    @pl.when(tv_ref[i] != 0)
    def _():
        slot = i % 2

        def wait_body(r, carry):
            row_copy(i, slot, r).wait()
            return carry
        lax.fori_loop(0, MOE_TM, wait_body, 0)
        xg = xg_ref[slot].astype(jnp.bfloat16)
        a = jnp.dot(xg, w1_ref[...], preferred_element_type=jnp.float32)
        b = jnp.dot(xg, w3_ref[...], preferred_element_type=jnp.float32)
        act = (a * jax.nn.sigmoid(a)) * b * rw_ref[...]
        y_ref[...] = jnp.dot(act.astype(jnp.bfloat16), w2_ref[...], preferred_element_type=jnp.float32)

    @pl.when(tv_ref[i] == 0)
    def _():
        y_ref[...] = jnp.zeros_like(y_ref)


def moe_grouped(h, tile_expert, tile_valid, tile_block, row_token, row_w, w1, w3, w2):
    grid_spec = pltpu.PrefetchScalarGridSpec(
        num_scalar_prefetch=4,
        grid=(MOE_TILES,),
        in_specs=[pl.BlockSpec(memory_space=pl.ANY),
                  pl.BlockSpec((MOE_TM, 1), lambda i, te, tv, tb, rt: (tb[i], 0)),
                  pl.BlockSpec((None, D_MODEL, D_FF), lambda i, te, tv, tb, rt: (te[i], 0, 0)),
                  pl.BlockSpec((None, D_MODEL, D_FF), lambda i, te, tv, tb, rt: (te[i], 0, 0)),
                  pl.BlockSpec((None, D_FF, D_MODEL), lambda i, te, tv, tb, rt: (te[i], 0, 0))],
        out_specs=pl.BlockSpec((MOE_TM, D_MODEL), lambda i, te, tv, tb, rt: (i, 0)),
        scratch_shapes=[pltpu.VMEM((2, MOE_TM, D_MODEL), jnp.float32), pltpu.SemaphoreType.DMA((2,))],
    )
    return pl.pallas_call(
        _moe_kernel,
        grid_spec=grid_spec,
        out_shape=jax.ShapeDtypeStruct((MOE_TILES * MOE_TM, D_MODEL), jnp.float32),
        compiler_params=_cparams("arbitrary"),
        name="moe_grouped",
    )(tile_expert, tile_valid, tile_block, row_token, h, row_w, w1, w3, w2)


def _combine_kernel(pos_ref, y_hbm, x_ref, mod_ref, o_ref, buf_ref, sem_ref, *, gate_row):
    i = pl.program_id(0)
    n_steps = pl.num_programs(0)

    def row_copy(step, slot, r):
        src = pos_ref[step * (TOP_K * CMB_TB) + r]
        return pltpu.make_async_copy(y_hbm.at[pl.ds(src, 1), :], buf_ref.at[slot, pl.ds(r, 1), :], sem_ref.at[slot])

    def start_gather(step, slot):
        def body(r, carry):
            row_copy(step, slot, r).start()
            return carry
        lax.fori_loop(0, TOP_K * CMB_TB, body, 0)

    @pl.when(i == 0)
    def _():
        start_gather(0, 0)

    @pl.when(i + 1 < n_steps)
    def _():
        start_gather(i + 1, (i + 1) % 2)

    slot = i % 2

    def wait_body(r, carry):
        row_copy(i, slot, r).wait()
        return carry
    lax.fori_loop(0, TOP_K * CMB_TB, wait_body, 0)
    moe = buf_ref[slot, 0:CMB_TB, :] + buf_ref[slot, CMB_TB:2 * CMB_TB, :]
    o_ref[...] = x_ref[...] + mod_ref[gate_row:gate_row + 1, :] * moe


def moe_combine(pos, ys, x, mod, gate_row):
    n = x.shape[0]
    grid_spec = pltpu.PrefetchScalarGridSpec(
        num_scalar_prefetch=1,
        grid=(n // CMB_TB,),
        in_specs=[pl.BlockSpec(memory_space=pl.ANY),
                  pl.BlockSpec((CMB_TB, D_MODEL), lambda i, pos: (i, 0)),
                  pl.BlockSpec((None, 6, D_MODEL), lambda i, pos: (jnp.where(i * CMB_TB >= CTX_LEN, 1, 0), 0, 0))],
        out_specs=pl.BlockSpec((CMB_TB, D_MODEL), lambda i, pos: (i, 0)),
        scratch_shapes=[pltpu.VMEM((2, TOP_K * CMB_TB, D_MODEL), jnp.float32), pltpu.SemaphoreType.DMA((2,))],
    )
    return pl.pallas_call(
        functools.partial(_combine_kernel, gate_row=gate_row),
        grid_spec=grid_spec,
        out_shape=jax.ShapeDtypeStruct((n, D_MODEL), jnp.float32),
        compiler_params=_cparams("arbitrary"),
        name="moe_combine",
    )(pos, ys, x, mod)


def moe_route(logits, bg, be):
    n = logits.shape[0]
    g_logits = logits[:, :N_GROUPS] + bg
    grp = jnp.argmax(g_logits, axis=-1)
    g_sel = grp[:, None] == jnp.arange(N_GROUPS)[None, :]
    p_grp = jnp.sum(jnp.where(g_sel, jax.nn.softmax(g_logits, axis=-1), 0.0), axis=-1, keepdims=True)
    e_logits = (logits[:, N_GROUPS:N_GROUPS + N_EXPERTS] + be).reshape(n, N_GROUPS, EXP_PER_GROUP)
    e_in = jnp.sum(jnp.where(g_sel[:, :, None], e_logits, 0.0), axis=1)
    lane = jnp.arange(EXP_PER_GROUP)[None, :]
    i1 = jnp.argmax(e_in, axis=-1)
    v1 = jnp.max(e_in, axis=-1)
    rest = jnp.where(lane == i1[:, None], -jnp.inf, e_in)
    i2 = jnp.argmax(rest, axis=-1)
    v2 = jnp.max(rest, axis=-1)
    top_w = jax.nn.softmax(jnp.stack([v1, v2], axis=-1), axis=-1) * p_grp
    eid = grp[:, None] * EXP_PER_GROUP + jnp.stack([i1, i2], axis=-1)
    return eid.astype(jnp.int32), top_w


def moe_schedule(eid, top_w):
    n = eid.shape[0]
    flat_e = eid.reshape(-1)
    onehot = (flat_e[:, None] == jnp.arange(N_EXPERTS)[None, :]).astype(jnp.int32)
    rank = jnp.sum(onehot * (jnp.cumsum(onehot, axis=0) - 1), axis=1)
    counts = jnp.sum(onehot, axis=0)
    padded = ((counts + MOE_TM - 1) // MOE_TM) * MOE_TM
    pad_end = jnp.cumsum(padded)
    pad_start = pad_end - padded
    dest = (jnp.sum(onehot * pad_start[None, :], axis=1) + rank).astype(jnp.int32)
    n_rows = MOE_TILES * MOE_TM
    row_token = jnp.zeros((n_rows,), jnp.int32).at[dest].set(jnp.arange(TOP_K * n, dtype=jnp.int32) // TOP_K)
    row_w = jnp.zeros((n_rows,), jnp.float32).at[dest].set(top_w.reshape(-1))
    tile_row0 = jnp.arange(MOE_TILES, dtype=jnp.int32) * MOE_TM
    n_valid = pad_end[-1] // MOE_TM
    tile_valid = (tile_row0 < pad_end[-1]).astype(jnp.int32)
    tile_expert = jnp.sum((tile_row0[:, None] >= pad_end[None, :]).astype(jnp.int32), axis=1)
    last_expert = jnp.sum(jnp.where(jnp.arange(MOE_TILES) == n_valid - 1, tile_expert, 0))
    tile_expert = jnp.where(tile_valid != 0, tile_expert, last_expert).astype(jnp.int32)
    tile_block = jnp.minimum(jnp.arange(MOE_TILES, dtype=jnp.int32), n_valid - 1).astype(jnp.int32)
    pos = dest.reshape(n // CMB_TB, CMB_TB, TOP_K).transpose(0, 2, 1).reshape(-1)
    return tile_expert, tile_valid, tile_block, row_token, row_w.reshape(n_rows, 1), pos


NA_QR = 4
NA_KR = NA_QR + NA_MAX_ROWS - 1
NA_TQ = NA_QR * GRID_W
NA_TK = NA_KR * GRID_W


def natten_bias_tiles(rpb):
    hi = lax.Precision.HIGHEST
    cols = jnp.arange(GRID_W)
    dc = jnp.clip(cols[None, :] - cols[:, None] + (NA_COLS - 1), 0, 2 * NA_COLS - 2)
    c0 = jnp.clip(cols - NA_COLS // 2, 0, GRID_W - NA_COLS)
    in_win = (cols[None, :] >= c0[:, None]) & (cols[None, :] < c0[:, None] + NA_COLS)
    oh_c = (dc[..., None] == jnp.arange(2 * NA_COLS - 1)).astype(jnp.float32)
    toep = jnp.einsum('qkc,hdc->hdqk', oh_c, rpb.astype(jnp.float32), precision=hi)
    rl = jnp.arange(NA_QR)[:, None]
    kl = jnp.arange(NA_KR)[None, :]
    half = NA_MAX_ROWS // 2
    pats = [(kl - rl + NA_MAX_ROWS - 1, (kl < NA_MAX_ROWS) & (rl >= 0)),
            (kl - rl + NA_MAX_ROWS - 1 - half, (kl - rl >= 0) & (kl - rl < NA_MAX_ROWS)),
            (kl - rl, (kl >= NA_KR - NA_MAX_ROWS) & (rl >= 0))]
    tiles = []
    for dr, valid in pats:
        oh_r = ((dr[..., None] == jnp.arange(2 * NA_MAX_ROWS - 1)) & valid[..., None]).astype(jnp.float32)
        t = jnp.einsum('rkd,hdqc->hrqkc', oh_r, toep, precision=hi)
        ok = valid[None, :, None, :, None] & in_win[None, None, :, None, :]
        tiles.append(jnp.where(ok, t, -jnp.inf).reshape(rpb.shape[0], NA_TQ, NA_TK))
    return jnp.stack(tiles, axis=1)


def _natten_kernel(q_ref, k_ref, v_ref, bias_ref, o_ref, *, n_rows):
    j = pl.program_id(1)
    bf16 = jnp.bfloat16
    nt = (((1,), (1,)), ((), ()))
    q = (q_ref[...] * HEAD_DIM ** -0.5).astype(bf16)
    kc = k_ref[0:CTX_LEN, :].astype(bf16)
    vc = v_ref[0:CTX_LEN, :].astype(bf16)
    s_ctx = lax.dot_general(q, kc, nt, preferred_element_type=jnp.float32)

    @pl.when(j == 0)
    def _():
        m = jnp.max(s_ctx, axis=-1, keepdims=True)
        p = jnp.exp(s_ctx - m)
        l = jnp.sum(p, axis=-1, keepdims=True)
        o = jnp.dot(p.astype(bf16), vc, preferred_element_type=jnp.float32)
        o_ref[...] = (o / l).astype(o_ref.dtype)

    @pl.when(j > 0)
    def _():
        r0 = (j - 1) * NA_QR
        kbase = jnp.clip(r0 - NA_MAX_ROWS // 2, 0, n_rows - NA_KR)
        start = pl.multiple_of(CTX_LEN + kbase * GRID_W, GRID_W)
        kl = k_ref[pl.ds(start, NA_TK), :].astype(bf16)
        vl = v_ref[pl.ds(start, NA_TK), :].astype(bf16)
        s_loc = lax.dot_general(q, kl, nt, preferred_element_type=jnp.float32) + bias_ref[...]
        m = jnp.maximum(jnp.max(s_loc, axis=-1, keepdims=True), jnp.max(s_ctx, axis=-1, keepdims=True))
        p_loc = jnp.exp(s_loc - m)
        p_ctx = jnp.exp(s_ctx - m)
        l = jnp.sum(p_loc, axis=-1, keepdims=True) + jnp.sum(p_ctx, axis=-1, keepdims=True)
        o = (jnp.dot(p_loc.astype(bf16), vl, preferred_element_type=jnp.float32)
             + jnp.dot(p_ctx.astype(bf16), vc, preferred_element_type=jnp.float32))
        o_ref[...] = (o / l).astype(o_ref.dtype)


def natten(p_main, bias_tiles):
    n = p_main.shape[0]
    n_rows = (n - CTX_LEN) // GRID_W
    n_blocks = n_rows // NA_QR
    assert n_rows % NA_QR == 0 and n_rows >= NA_KR and CTX_LEN == NA_TQ

    def pat(j):
        return jnp.where(j <= 1, 0, jnp.where(j == n_blocks, 2, 1))

    return pl.pallas_call(
        functools.partial(_natten_kernel, n_rows=n_rows),
        grid=(N_HEADS, n_blocks + 1),
        in_specs=[pl.BlockSpec((NA_TQ, HEAD_DIM), lambda h, j: (j, COL_Q * HPG + h)),
                  pl.BlockSpec((n, HEAD_DIM), lambda h, j: (0, COL_K * HPG + h)),
                  pl.BlockSpec((n, HEAD_DIM), lambda h, j: (0, COL_V * HPG + h)),
                  pl.BlockSpec((None, None, NA_TQ, NA_TK), lambda h, j: (h, pat(j), 0, 0))],
        out_specs=pl.BlockSpec((NA_TQ, HEAD_DIM), lambda h, j: (j, h)),
        out_shape=jax.ShapeDtypeStruct((n, GROUP_W), jnp.bfloat16),
        compiler_params=_cparams("arbitrary", "arbitrary"),
        name="natten",
    )(p_main, p_main, p_main, bias_tiles)


POOL_HALO = max(POOL_WINDOWS) // 2


def _pool_kernel(prev_ref, cur_ref, next_ref, w_ref, s_ref, o_ref, ext_ref, *, n_tok):
    i = pl.program_id(0)
    t0 = i * ROW_TILE
    seq_lo = jnp.where(t0 < CTX_LEN, 0, CTX_LEN)
    seq_hi = jnp.where(t0 < CTX_LEN, CTX_LEN, n_tok)
    ext_ref[0:POOL_HALO, :] = jnp.where(t0 - POOL_HALO >= seq_lo, prev_ref[...], 0.0)
    ext_ref[POOL_HALO:POOL_HALO + ROW_TILE, :] = cur_ref[...]
    ext_ref[POOL_HALO + ROW_TILE:, :] = jnp.where(t0 + ROW_TILE < seq_hi, next_ref[...], 0.0)
    t = t0 + lax.broadcasted_iota(jnp.int32, (ROW_TILE, 1), 0)
    for g, win in enumerate(POOL_WINDOWS):
        half = win // 2
        cs = slice(g * C_GW, (g + 1) * C_GW)
        acc = ext_ref[POOL_HALO - half:POOL_HALO - half + ROW_TILE, cs]
        for d in range(-half + 1, half):
            acc = acc + ext_ref[POOL_HALO + d:POOL_HALO + d + ROW_TILE, cs]
        cnt = (jnp.minimum(t + half, seq_hi) - jnp.maximum(t - half, seq_lo)).astype(jnp.float32)
        diff = acc / cnt - cur_ref[:, cs]
        y = jnp.dot(diff.astype(jnp.bfloat16), w_ref[g], preferred_element_type=jnp.float32)
        o_ref[:, cs] = (y * s_ref[:, cs]).astype(o_ref.dtype)


def pool_mix(p_main, w_pool, pool_scale):
    n = p_main.shape[0]
    hb = ROW_TILE // POOL_HALO
    n_hblk = n // POOL_HALO
    return pl.pallas_call(
        functools.partial(_pool_kernel, n_tok=n),
        grid=(n // ROW_TILE,),
        in_specs=[pl.BlockSpec((POOL_HALO, GROUP_W), lambda i: (jnp.maximum(i * hb - 1, 0), COL_C)),
                  pl.BlockSpec((ROW_TILE, GROUP_W), lambda i: (i, COL_C)),
                  pl.BlockSpec((POOL_HALO, GROUP_W), lambda i: (jnp.minimum((i + 1) * hb, n_hblk - 1), COL_C)),
                  pl.BlockSpec((C_GROUPS, C_GW, C_GW), lambda i: (0, 0, 0)),
                  pl.BlockSpec((1, GROUP_W), lambda i: (0, 0))],
        out_specs=pl.BlockSpec((ROW_TILE, GROUP_W), lambda i: (i, 0)),
        out_shape=jax.ShapeDtypeStruct((n, GROUP_W), jnp.bfloat16),
        scratch_shapes=[pltpu.VMEM((ROW_TILE + 2 * POOL_HALO, GROUP_W), jnp.float32)],
        compiler_params=_cparams("arbitrary"),
        name="pool_mix",
    )(p_main, p_main, p_main, w_pool, pool_scale.reshape(1, GROUP_W))


def chunk_masks(chunk, reverse):
    i = jnp.arange(ROW_TILE)[:, None]
    j = jnp.arange(ROW_TILE)[None, :]
    same = (i // chunk) == (j // chunk)
    tri = same & ((j >= i) if reverse else (j <= i))
    return jnp.concatenate([tri, same], axis=0).astype(jnp.bfloat16)


def _split3(x):
    hi = x.astype(jnp.bfloat16)
    r = x - hi.astype(jnp.float32)
    mid = r.astype(jnp.bfloat16)
    lo = (r - mid.astype(jnp.float32)).astype(jnp.bfloat16)
    return hi, mid, lo


def _dot3(m, x):
    hi, mid, lo = _split3(x)
    d = functools.partial(jnp.dot, preferred_element_type=jnp.float32)
    return d(m, hi) + (d(m, mid) + d(m, lo))


def _scan_block(n_blk, reverse):
    if reverse:
        return lambda j: jnp.where(j == 0, 0, n_blk - j)
    return lambda j: j


def _hgrn_kernel(q_ref, v_ref, z_ref, lb_ref, mask_ref, *rest, reverse):
    if reverse:
        g_ref, of_ref, gn_ref, o_ref, st_ref = rest
    else:
        o_ref, st_ref = rest
    j = pl.program_id(1)
    bf16 = jnp.bfloat16
    f32 = jnp.float32
    T, C = ROW_TILE, A_CHUNK
    nt = (((1,), (1,)), ((), ()))
    tn = (((0,), (0,)), ((), ()))

    @pl.when(j == 0)
    def _():
        st_ref[...] = jnp.zeros_like(st_ref)

    z = z_ref[...]
    log_lb = lb_ref[0:1, :]
    log1m_lb = lb_ref[1:2, :]
    one_m_lb = lb_ref[2:3, :]
    log_sig = jnp.minimum(z, 0.0) - jnp.log1p(jnp.exp(-jnp.abs(z)))
    bb = log1m_lb + log_sig
    mx = jnp.maximum(log_lb, bb)
    lf = mx + jnp.log1p(jnp.exp(-jnp.abs(log_lb - bb)))
    k = one_m_lb / (1.0 + jnp.exp(z))
    aq = q_ref[...]
    q = aq / (1.0 + jnp.exp(-aq))
    v = v_ref[...].astype(bf16)

    cum = _dot3(mask_ref[...], lf)
    b = cum[0:T]
    b_end = cum[T:2 * T]
    q_dec = (q * jnp.exp(b)).astype(bf16)
    k_inv = (k * jnp.exp(-b)).astype(bf16)
    k_dec = (k * jnp.exp(b_end - b)).astype(bf16)
    dec = jnp.exp(b_end)

    scores = lax.dot_general(q_dec, k_inv, nt, preferred_element_type=f32)
    scores = jnp.where(mask_ref[0:T, :] > 0, scores, 0.0)
    o_intra = jnp.dot(scores.astype(bf16), v, preferred_element_type=f32)

    st = st_ref[...]
    outs = [None] * (T // C)
    order = range(T // C - 1, -1, -1) if reverse else range(T // C)
    for c in order:
        rows = slice(c * C, (c + 1) * C)
        outs[c] = lax.dot_general(q_dec[rows], st.astype(bf16), nt, preferred_element_type=f32)
        delta_t = lax.dot_general(v[rows], k_dec[rows], tn, preferred_element_type=f32)
        st = st * dec[c * C:c * C + 1, :] + delta_t
    st_ref[...] = st
    o = o_intra + jnp.concatenate(outs, axis=0)
    if reverse:
        o = o + of_ref[...]
        o = o * lax.rsqrt(jnp.mean(o * o, axis=-1, keepdims=True) + EPS) * gn_ref[...]
        ag = g_ref[...]
        o_ref[...] = (o * (ag / (1.0 + jnp.exp(-ag)))).astype(o_ref.dtype)
    else:
        o_ref[...] = o


def hgrn_scan(p_main, lb_rows, masks, reverse, o_fwd=None, gn=None):
    n = p_main.shape[0]
    n_blk = n // ROW_TILE
    tb = _scan_block(n_blk, reverse)
    blk = lambda c: pl.BlockSpec((ROW_TILE, HEAD_DIM), lambda h, j: (tb(j), c * HPG + h))
    in_specs = [blk(COL_AQ), blk(COL_AI), blk(COL_AFB if reverse else COL_AFF),
                pl.BlockSpec((3, HEAD_DIM), lambda h, j: (0, h)),
                pl.BlockSpec((2 * ROW_TILE, ROW_TILE), lambda h, j: (0, 0))]
    args = [p_main, p_main, p_main, lb_rows, masks]
    if reverse:
        in_specs += [blk(COL_AG), pl.BlockSpec((ROW_TILE, HEAD_DIM), lambda h, j: (tb(j), h)),
                     pl.BlockSpec((1, HEAD_DIM), lambda h, j: (0, h))]
        args += [p_main, o_fwd, gn.reshape(1, GROUP_W)]
    return pl.pallas_call(
        functools.partial(_hgrn_kernel, reverse=reverse),
        grid=(N_HEADS, n_blk),
        in_specs=in_specs,
        out_specs=pl.BlockSpec((ROW_TILE, HEAD_DIM), lambda h, j: (tb(j), h)),
        out_shape=jax.ShapeDtypeStruct((n, GROUP_W), jnp.bfloat16 if reverse else jnp.float32),
        scratch_shapes=[pltpu.VMEM((HEAD_DIM, HEAD_DIM), jnp.float32)],
        compiler_params=_cparams("arbitrary", "arbitrary"),
        name="hgrn_bwd" if reverse else "hgrn_fwd",
    )(*args)


def hgrn_mix(p_main, lb_f, lb_b, gn):
    rows = lambda lb: jnp.stack([jnp.log(lb), jnp.log1p(-lb), 1.0 - lb])
    o_f = hgrn_scan(p_main, rows(lb_f), chunk_masks(A_CHUNK, False), False)
    return hgrn_scan(p_main, rows(lb_b), chunk_masks(A_CHUNK, True), True, o_f, gn)


def hgrn_lower_bounds(p):
    cs = jnp.cumsum(jax.nn.softmax(p.astype(jnp.float32), axis=1), axis=1)
    return cs - cs[:, :1]


def rope_tables(n_tok):
    quarter = HEAD_DIM // 4
    pos = jnp.arange(n_tok - CTX_LEN)
    inv_freq = ROPE_BASE ** (-jnp.arange(quarter, dtype=jnp.float32) / quarter)
    lane = jnp.arange(HEAD_DIM)
    p = jnp.where(lane[None, :] < HEAD_DIM // 2, (pos // GRID_W)[:, None], (pos % GRID_W)[:, None]).astype(jnp.float32)
    ang = p * inv_freq[lane % quarter][None, :]
    sign = jnp.where((lane % (2 * quarter)) < quarter, -1.0, 1.0)
    cos = jnp.concatenate([jnp.ones((CTX_LEN, HEAD_DIM), jnp.float32), jnp.cos(ang)], axis=0)
    sin = jnp.concatenate([jnp.zeros((CTX_LEN, HEAD_DIM), jnp.float32), jnp.sin(ang) * sign], axis=0)
    return cos, sin


def _rope(x, cos, sin):
    quarter = HEAD_DIM // 4
    lane = lax.broadcasted_iota(jnp.int32, x.shape, 1)
    partner = jnp.where((lane % (2 * quarter)) < quarter,
                        pltpu.roll(x, HEAD_DIM - quarter, axis=1), pltpu.roll(x, quarter, axis=1))
    return x * cos + partner * sin


def _mlstm_kernel(q_ref, k_ref, v_ref, g_ref, cos_ref, sin_ref, mask_ref, maskt_ref, *rest, reverse):
    if reverse:
        do_ref, of_ref, gn_ref, o_ref, s_ref, n_ref, m_ref, cumt_ref, lit_ref = rest
    else:
        o_ref, s_ref, n_ref, m_ref, cumt_ref, lit_ref = rest
    h = pl.program_id(0)
    j = pl.program_id(1)
    bf16 = jnp.bfloat16
    f32 = jnp.float32
    T, C = ROW_TILE, D_CHUNK
    nt = (((1,), (1,)), ((), ()))
    tn = (((0,), (0,)), ((), ()))
    col_i = (GATE_I_B if reverse else GATE_I_F) * N_HEADS + h
    col_f = (GATE_F_B if reverse else GATE_F_F) * N_HEADS + h

    @pl.when(j == 0)
    def _():
        s_ref[...] = jnp.zeros_like(s_ref)
        n_ref[...] = jnp.zeros_like(n_ref)
        m_ref[...] = jnp.zeros_like(m_ref)

    g = g_ref[...]
    lsg = jnp.minimum(g, 0.0) - jnp.log1p(jnp.exp(-jnp.abs(g)))
    cum = _dot3(mask_ref[...], lsg)
    lane = lax.broadcasted_iota(jnp.int32, (T, LANES), 1)
    pick = lambda x, c: jnp.sum(jnp.where(lane == c, x, 0.0), axis=1, keepdims=True)
    b_col = pick(cum[0:T], col_f)
    bend_col = pick(cum[T:2 * T], col_f)
    li_col = pick(g, col_i)
    hi, mid, lo = _split3(lsg.T)
    d = functools.partial(jnp.dot, preferred_element_type=f32)
    cumt_ref[...] = d(hi, maskt_ref[...]) + (d(mid, maskt_ref[...]) + d(lo, maskt_ref[...]))
    lit_ref[...] = g.T
    b_row = cumt_ref[pl.ds(col_f, 1), :]
    li_row = lit_ref[pl.ds(col_i, 1), :]

    cos = cos_ref[...]
    sin = sin_ref[...]
    q = _rope(q_ref[...], cos, sin) * HEAD_DIM ** -0.5
    k = _rope(k_ref[...], cos, sin)
    qb = q.astype(bf16)
    kb = k.astype(bf16)
    vb = v_ref[...].astype(bf16)

    s = s_ref[...]
    nv = n_ref[0:1, :]
    m = m_ref[0:1, 0:1]
    ii = lax.broadcasted_iota(jnp.int32, (C, C), 0)
    jj = lax.broadcasted_iota(jnp.int32, (C, C), 1)
    causal = (jj >= ii) if reverse else (jj <= ii)
    outs = [None] * (T // C)
    order = range(T // C - 1, -1, -1) if reverse else range(T // C)
    for c in order:
        rows = slice(c * C, (c + 1) * C)
        bc, lic = b_col[rows], li_col[rows]
        bend = bend_col[c * C:c * C + 1]
        br, lir = b_row[:, rows], li_row[:, rows]
        dmat = jnp.where(causal, bc - br + lir, -jnp.inf)
        m_inter = bc + m
        m_t = jnp.maximum(m_inter, jnp.max(dmat, axis=-1, keepdims=True))
        w = jnp.exp(dmat - m_t) * lax.dot_general(qb[rows], kb[rows], nt, preferred_element_type=f32)
        a_col = jnp.exp(m_inter - m_t)
        num = d(w.astype(bf16), vb[rows]) + a_col * d(qb[rows], s.astype(bf16))
        den = jnp.sum(w, axis=-1, keepdims=True) + a_col * jnp.sum(q[rows] * nv, axis=-1, keepdims=True)
        outs[c] = num / jnp.maximum(jnp.abs(den), jnp.exp(-m_t))
        gcol = bend - bc + lic
        m_loc = jnp.max(gcol, axis=0, keepdims=True)
        kw = k[rows] * jnp.exp(gcol - m_loc)
        d_s = lax.dot_general(kw.astype(bf16), vb[rows], tn, preferred_element_type=f32)
        d_n = jnp.sum(kw, axis=0, keepdims=True)
        m_new = jnp.maximum(bend + m, m_loc)
        a = jnp.exp(bend + m - m_new)
        cc = jnp.exp(m_loc - m_new)
        s = a * s + cc * d_s
        nv = a * nv + cc * d_n
        m = m_new
    s_ref[...] = s
    n_ref[...] = jnp.broadcast_to(nv, n_ref.shape)
    m_ref[...] = jnp.broadcast_to(m, m_ref.shape)
    o = jnp.concatenate(outs, axis=0)
    if reverse:
        o = o + of_ref[...]
        o = o * lax.rsqrt(jnp.mean(o * o, axis=-1, keepdims=True) + EPS) * gn_ref[...]
        o_ref[...] = (o / (1.0 + jnp.exp(-do_ref[...]))).astype(o_ref.dtype)
    else:
        o_ref[...] = o


def mlstm_scan(p_main, p_gate, cos, sin, masks, reverse, o_fwd=None, gn=None):
    n = p_main.shape[0]
    n_blk = n // ROW_TILE
    tb = _scan_block(n_blk, reverse)
    blk = lambda c: pl.BlockSpec((ROW_TILE, HEAD_DIM), lambda h, j: (tb(j), c * HPG + h))
    tok = pl.BlockSpec((ROW_TILE, LANES), lambda h, j: (tb(j), 0))
    in_specs = [blk(COL_DQ), blk(COL_DK), blk(COL_DV), tok, tok, tok,
                pl.BlockSpec((2 * ROW_TILE, ROW_TILE), lambda h, j: (0, 0)),
                pl.BlockSpec((ROW_TILE, ROW_TILE), lambda h, j: (0, 0))]
    args = [p_main, p_main, p_main, p_gate, cos, sin, masks, masks[:ROW_TILE].T]
    if reverse:
        in_specs += [blk(COL_DO), pl.BlockSpec((ROW_TILE, HEAD_DIM), lambda h, j: (tb(j), h)),
                     pl.BlockSpec((1, HEAD_DIM), lambda h, j: (0, h))]
        args += [p_main, o_fwd, gn.reshape(1, GROUP_W)]
    return pl.pallas_call(
        functools.partial(_mlstm_kernel, reverse=reverse),
        grid=(N_HEADS, n_blk),
        in_specs=in_specs,
        out_specs=pl.BlockSpec((ROW_TILE, HEAD_DIM), lambda h, j: (tb(j), h)),
        out_shape=jax.ShapeDtypeStruct((n, GROUP_W), jnp.bfloat16 if reverse else jnp.float32),
        scratch_shapes=[pltpu.VMEM((HEAD_DIM, HEAD_DIM), jnp.float32), pltpu.VMEM((8, HEAD_DIM), jnp.float32),
                        pltpu.VMEM((8, LANES), jnp.float32), pltpu.VMEM((LANES, ROW_TILE), jnp.float32),
                        pltpu.VMEM((LANES, ROW_TILE), jnp.float32)],
        compiler_params=_cparams("arbitrary", "arbitrary"),
        name="mlstm_bwd" if reverse else "mlstm_fwd",
    )(*args)


def mlstm_mix(p_main, p_gate, gn):
    cos, sin = rope_tables(p_main.shape[0])
    o_f = mlstm_scan(p_main, p_gate, cos, sin, chunk_masks(D_CHUNK, False), False)
    return mlstm_scan(p_main, p_gate, cos, sin, chunk_masks(D_CHUNK, True), True, o_f, gn)


def kernel(x, c, ctx, c_ctx, ada_w, ada_b, norm1_w, norm2_w, mix_w_in, mix_b_in, hgrn_lb, hgrn_gn, natten_rpb,
           pool_w, pool_scale, mlstm_gn, mix_w_out, moe_wg, moe_bg, moe_we, moe_be, moe_w1, moe_w3, moe_w2, final_w):
    assert x.shape == (1, SEQ, D_MODEL) and ctx.shape == (1, CTX_LEN, D_MODEL)
    bf16 = jnp.bfloat16
    lbs = hgrn_lower_bounds(hgrn_lb)

    cond = jnp.zeros((16, D_MODEL), jnp.float32).at[0].set(jax.nn.silu(c_ctx)).at[1].set(jax.nn.silu(c[0]))
    mods = ada_modulation(cond.astype(bf16), ada_w, ada_b)[:, :2].reshape(DEPTH, 2, 6, D_MODEL)

    xs = jnp.concatenate([ctx[0], x[0]], axis=0)
    for l in range(DEPTH):
        mod = mods[l]
        w_in = mix_w_in[l].astype(bf16)
        h = norm_modulate(xs, norm1_w[l], mod, 0)
        p_main = matmul_bias(h, w_in[:, :N_MAIN], mix_b_in[l, :N_MAIN], MM_TM, MM_TN)
        w_gate = jnp.pad(w_in[:, N_MAIN:], ((0, 0), (0, LANES - N_GATE)))
        b_gate = jnp.pad(mix_b_in[l, N_MAIN:], (0, LANES - N_GATE))
        p_gate = matmul_bias(h, w_gate, b_gate, MM_TM, LANES)

        out_a = hgrn_mix(p_main, lbs[0, l], lbs[1, l], hgrn_gn[l])
        out_b = natten(p_main, natten_bias_tiles(natten_rpb[l]))
        out_c = pool_mix(p_main, pool_w[l].astype(bf16), pool_scale[l])
        out_d = mlstm_mix(p_main, p_gate, mlstm_gn[l])
        xs = matmul_gated_residual((out_a, out_b, out_c, out_d), mix_w_out[l].astype(bf16), xs, mod, 2)

        w_router = jnp.pad(jnp.concatenate([moe_wg[l], moe_we[l]], axis=1),
                           ((0, 0), (0, LANES - N_GROUPS - N_EXPERTS)))
        h2, logits = norm_modulate_router(xs, norm2_w[l], mod, w_router)
        eid, top_w = moe_route(logits, moe_bg[l], moe_be[l])
        tile_expert, tile_valid, tile_block, row_token, row_w, cpos = moe_schedule(eid, top_w)
        ys = moe_grouped(h2, tile_expert, tile_valid, tile_block, row_token, row_w,
                         moe_w1[l].astype(bf16), moe_w3[l].astype(bf16), moe_w2[l].astype(bf16))
        xs = moe_combine(cpos, ys, xs, mod, 5)

    out = norm_modulate(xs, final_w, mods[0], None, out_dtype=jnp.float32)
    return out[CTX_LEN:][None]
```

```python
import functools

import jax
import jax.numpy as jnp
from jax import lax
from jax.experimental import pallas as pl
from jax.experimental.pallas import tpu as pltpu

D_MODEL = 4096
SEQ = 8192
DEPTH = 2
GRID_W = 64
CTX_LEN = 256
N_TOK = CTX_LEN + SEQ
GROUP_W = 1024
HEAD_DIM = 128
N_HEADS = GROUP_W // HEAD_DIM
C_GROUPS = 4
C_GW = GROUP_W // C_GROUPS
POOL_WINDOWS = (2, 4, 8, 16)
NA_MAX_ROWS = 8
NA_COLS = 16
A_CHUNK = 32
D_CHUNK = 64
ROPE_BASE = 10000.0
N_GROUPS = 4
EXP_PER_GROUP = 8
N_EXPERTS = N_GROUPS * EXP_PER_GROUP
TOP_K = 2
D_FF = D_MODEL // 8
EPS = 1e-6
N_MAIN = 13 * GROUP_W
N_GATE = 4 * N_HEADS
LANES = 128
VMEM_LIMIT = 56 * 1024 * 1024

ROW_TILE = 256
MM_TM = 768
MM_TN = 512
MOE_TM = 256
MOE_TILES = (TOP_K * N_TOK + N_EXPERTS * (MOE_TM - 1)) // MOE_TM + 1
CMB_TB = 128
SCAN_HP = 8
DMA_UNROLL = 8

COL_AQ, COL_AI, COL_AG, COL_AFF, COL_AFB = 0, 1, 2, 3, 4
COL_Q, COL_K, COL_V, COL_C = 5, 6, 7, 8
COL_DQ, COL_DK, COL_DV, COL_DO = 9, 10, 11, 12
GATE_I_F, GATE_F_F, GATE_I_B, GATE_F_B = 0, 1, 2, 3
HPG = GROUP_W // HEAD_DIM


def _cparams(*sem):
    return pltpu.CompilerParams(dimension_semantics=sem, vmem_limit_bytes=VMEM_LIMIT)


def _ada_kernel(c_ref, w_ref, b_ref, o_ref):
    acc = jnp.dot(c_ref[...], w_ref[...].astype(jnp.bfloat16), preferred_element_type=jnp.float32)
    o_ref[...] = acc + b_ref[...]


def ada_modulation(cond, ada_w, ada_b):
    tn = 512
    n = ada_w.shape[-1]
    rows = cond.shape[0]
    return pl.pallas_call(
        _ada_kernel,
        grid=(DEPTH, n // tn),
        in_specs=[pl.BlockSpec((rows, D_MODEL), lambda l, j: (0, 0)),
                  pl.BlockSpec((None, D_MODEL, tn), lambda l, j: (l, 0, j)),
                  pl.BlockSpec((None, 1, tn), lambda l, j: (l, 0, j))],
        out_specs=pl.BlockSpec((None, rows, tn), lambda l, j: (l, 0, j)),
        out_shape=jax.ShapeDtypeStruct((DEPTH, rows, n), jnp.float32),
        compiler_params=_cparams("arbitrary", "arbitrary"),
        name="ada_modulation",
    )(cond, ada_w, ada_b.reshape(DEPTH, 1, n))


def _norm_kernel(x_ref, w_ref, mod_ref, o_ref, *, shift_row, scale_row):
    x = x_ref[...]
    y = x * lax.rsqrt(jnp.mean(x * x, axis=-1, keepdims=True) + EPS) * w_ref[...]
    if shift_row is not None:
        y = y * (1.0 + mod_ref[scale_row:scale_row + 1, :]) + mod_ref[shift_row:shift_row + 1, :]
    o_ref[...] = y.astype(o_ref.dtype)


def _norm_router_kernel(x_ref, w_ref, mod_ref, wr_ref, o_ref, lg_ref, *, shift_row, scale_row):
    x = x_ref[...]
    y = x * lax.rsqrt(jnp.mean(x * x, axis=-1, keepdims=True) + EPS) * w_ref[...]
    y = y * (1.0 + mod_ref[scale_row:scale_row + 1, :]) + mod_ref[shift_row:shift_row + 1, :]
    o_ref[...] = y.astype(o_ref.dtype)
    y_hi = y.astype(jnp.bfloat16)
    y_lo = (y - y_hi.astype(jnp.float32)).astype(jnp.bfloat16)
    w = wr_ref[...]
    w_hi = w.astype(jnp.bfloat16)
    w_lo = (w - w_hi.astype(jnp.float32)).astype(jnp.bfloat16)
    dot = functools.partial(jnp.dot, preferred_element_type=jnp.float32)
    lg_ref[...] = dot(y_hi, w_hi) + (dot(y_lo, w_hi) + dot(y_hi, w_lo))


def _tok_type(i):
    return jnp.where(i * ROW_TILE >= CTX_LEN, 1, 0)


def norm_modulate(x, w, mod, which, out_dtype=jnp.bfloat16, skip_rows=0):
    n = x.shape[0] - skip_rows
    off = skip_rows // ROW_TILE
    rows = (None, None) if which is None else (3 * which, 3 * which + 1)
    return pl.pallas_call(
        functools.partial(_norm_kernel, shift_row=rows[0], scale_row=rows[1]),
        grid=(n // ROW_TILE,),
        in_specs=[pl.BlockSpec((ROW_TILE, D_MODEL), lambda i: (i + off, 0)),
                  pl.BlockSpec((1, D_MODEL), lambda i: (0, 0)),
                  pl.BlockSpec((None, 6, D_MODEL), lambda i: (_tok_type(i + off), 0, 0))],
        out_specs=pl.BlockSpec((ROW_TILE, D_MODEL), lambda i: (i, 0)),
        out_shape=jax.ShapeDtypeStruct((n, D_MODEL), out_dtype),
        compiler_params=_cparams("arbitrary"),
        name="norm_modulate",
    )(x, w.reshape(1, D_MODEL), mod)


def norm_modulate_router(x, w, mod, w_router):
    n = x.shape[0]
    return pl.pallas_call(
        functools.partial(_norm_router_kernel, shift_row=3, scale_row=4),
        grid=(n // ROW_TILE,),
        in_specs=[pl.BlockSpec((ROW_TILE, D_MODEL), lambda i: (i, 0)),
                  pl.BlockSpec((1, D_MODEL), lambda i: (0, 0)),
                  pl.BlockSpec((None, 6, D_MODEL), lambda i: (_tok_type(i), 0, 0)),
                  pl.BlockSpec((D_MODEL, LANES), lambda i: (0, 0))],
        out_specs=[pl.BlockSpec((ROW_TILE, D_MODEL), lambda i: (i, 0)),
                   pl.BlockSpec((ROW_TILE, LANES), lambda i: (i, 0))],
        out_shape=[jax.ShapeDtypeStruct((n, D_MODEL), jnp.float32),
                   jax.ShapeDtypeStruct((n, LANES), jnp.float32)],
        compiler_params=_cparams("arbitrary"),
        name="norm_modulate_router",
    )(x, w.reshape(1, D_MODEL), mod, w_router)


def _mm_bias_kernel(a_ref, b_ref, bias_ref, o_ref):
    o_ref[...] = jnp.dot(a_ref[...], b_ref[...], preferred_element_type=jnp.float32) + bias_ref[...]


def _mm_resid_kernel(a0_ref, a1_ref, a2_ref, a3_ref, b_ref, x_ref, mod_ref, o_ref, *, gate_row):
    acc = None
    for g, a_ref in enumerate((a0_ref, a1_ref, a2_ref, a3_ref)):
        part = jnp.dot(a_ref[...], b_ref[g * GROUP_W:(g + 1) * GROUP_W, :], preferred_element_type=jnp.float32)
        acc = part if acc is None else acc + part
    tm = x_ref.shape[0]
    row = pl.program_id(0) * tm + lax.broadcasted_iota(jnp.int32, (tm, 1), 0)
    gate = jnp.where(row < CTX_LEN, mod_ref[0, gate_row:gate_row + 1, :], mod_ref[1, gate_row:gate_row + 1, :])
    o_ref[...] = x_ref[...] + gate * acc


def matmul_bias(a, b, bias, tm, tn, n=None):
    m, k = a.shape
    n = b.shape[1] if n is None else n
    return pl.pallas_call(
        _mm_bias_kernel,
        grid=(m // tm, n // tn),
        in_specs=[pl.BlockSpec((tm, k), lambda i, j: (i, 0)),
                  pl.BlockSpec((k, tn), lambda i, j: (0, j)),
                  pl.BlockSpec((1, tn), lambda i, j: (0, j))],
        out_specs=pl.BlockSpec((tm, tn), lambda i, j: (i, j)),
        out_shape=jax.ShapeDtypeStruct((m, n), jnp.float32),
        compiler_params=_cparams("arbitrary", "arbitrary"),
        name="matmul_bias",
    )(a, b, bias.reshape(1, -1))


def matmul_gated_residual(parts, b, x, mod, gate_row):
    m, n = x.shape
    tm, tn = MM_TM, MM_TN
    a_spec = pl.BlockSpec((tm, GROUP_W), lambda i, j: (i, 0))
    return pl.pallas_call(
        functools.partial(_mm_resid_kernel, gate_row=gate_row),
        grid=(m // tm, n // tn),
        in_specs=[a_spec, a_spec, a_spec, a_spec,
                  pl.BlockSpec((b.shape[0], tn), lambda i, j: (0, j)),
                  pl.BlockSpec((tm, tn), lambda i, j: (i, j)),
                  pl.BlockSpec((2, 6, tn), lambda i, j: (0, 0, j))],
        out_specs=pl.BlockSpec((tm, tn), lambda i, j: (i, j)),
        out_shape=jax.ShapeDtypeStruct((m, n), jnp.float32),
        compiler_params=_cparams("arbitrary", "arbitrary"),
        name="matmul_gated_residual",
    )(*parts, b, x, mod)


def _moe_kernel(te_ref, tv_ref, rt_ref, h_hbm, w1_ref, w3_ref, w2_ref, y_ref, xg_ref, sem_ref):
    del te_ref
    i = pl.program_id(0)
    n_tiles = pl.num_programs(0)

    def row_copy(tile, slot, r):
        tok = rt_ref[tile * MOE_TM + r]
        return pltpu.make_async_copy(h_hbm.at[pl.ds(tok, 1), :], xg_ref.at[slot, pl.ds(r, 1), :], sem_ref.at[slot])

    def start_gather(tile, slot):
        @pl.when(tv_ref[tile] != 0)
        def _():
            def body(r, carry):
                row_copy(tile, slot, r).start()
                return carry
            lax.fori_loop(0, MOE_TM, body, 0, unroll=DMA_UNROLL)

    @pl.when(i == 0)
    def _():
        start_gather(0, 0)

    @pl.when(i + 1 < n_tiles)
    def _():
        start_gather(i + 1, (i + 1) % 2)

    @pl.when(tv_ref[i] != 0)
    def _():
        slot = i % 2

        def wait_body(r, carry):
            row_copy(i, slot, r).wait()
            return carry
        lax.fori_loop(0, MOE_TM, wait_body, 0, unroll=DMA_UNROLL)
        xg = xg_ref[slot].astype(jnp.bfloat16)
        a = jnp.dot(xg, w1_ref[...], preferred_element_type=jnp.float32)
        b = jnp.dot(xg, w3_ref[...], preferred_element_type=jnp.float32)
        act = (a * jax.nn.sigmoid(a)) * b
        y_ref[...] = jnp.dot(act.astype(jnp.bfloat16), w2_ref[...], preferred_element_type=jnp.float32)

    @pl.when(tv_ref[i] == 0)
    def _():
        y_ref[...] = jnp.zeros_like(y_ref)


def moe_grouped(h, tile_expert, tile_valid, row_token, w1, w3, w2):
    grid_spec = pltpu.PrefetchScalarGridSpec(
        num_scalar_prefetch=3,
        grid=(MOE_TILES,),
        in_specs=[pl.BlockSpec(memory_space=pl.ANY),
                  pl.BlockSpec((None, D_MODEL, D_FF), lambda i, te, tv, rt: (te[i], 0, 0)),
                  pl.BlockSpec((None, D_MODEL, D_FF), lambda i, te, tv, rt: (te[i], 0, 0)),
                  pl.BlockSpec((None, D_FF, D_MODEL), lambda i, te, tv, rt: (te[i], 0, 0))],
        out_specs=pl.BlockSpec((MOE_TM, D_MODEL), lambda i, te, tv, rt: (i, 0)),
        scratch_shapes=[pltpu.VMEM((2, MOE_TM, D_MODEL), jnp.float32), pltpu.SemaphoreType.DMA((2,))],
    )
    return pl.pallas_call(
        _moe_kernel,
        grid_spec=grid_spec,
        out_shape=jax.ShapeDtypeStruct((MOE_TILES * MOE_TM, D_MODEL), jnp.float32),
        compiler_params=_cparams("arbitrary"),
        name="moe_grouped",
    )(tile_expert, tile_valid, row_token, h, w1, w3, w2)


def _combine_kernel(pos_ref, y_hbm, x_ref, w_ref, mod_ref, o_ref, buf_ref, sem_ref, *, gate_row):
    i = pl.program_id(0)
    n_steps = pl.num_programs(0)

    def row_copy(step, slot, r):
        src = pos_ref[step * (TOP_K * CMB_TB) + r]
        return pltpu.make_async_copy(y_hbm.at[pl.ds(src, 1), :], buf_ref.at[slot, pl.ds(r, 1), :], sem_ref.at[slot])

    def start_gather(step, slot):
        def body(r, carry):
            row_copy(step, slot, r).start()
            return carry
        lax.fori_loop(0, TOP_K * CMB_TB, body, 0, unroll=DMA_UNROLL)

    @pl.when(i == 0)
    def _():
        start_gather(0, 0)

    @pl.when(i + 1 < n_steps)
    def _():
        start_gather(i + 1, (i + 1) % 2)

    slot = i % 2

    def wait_body(r, carry):
        row_copy(i, slot, r).wait()
        return carry
    lax.fori_loop(0, TOP_K * CMB_TB, wait_body, 0, unroll=DMA_UNROLL)
    w = w_ref[...]
    moe = w[:, 0:1] * buf_ref[slot, 0:CMB_TB, :] + w[:, 1:2] * buf_ref[slot, CMB_TB:2 * CMB_TB, :]
    o_ref[...] = x_ref[...] + mod_ref[gate_row:gate_row + 1, :] * moe


def moe_combine(pos, ys, x, top_w, mod, gate_row):
    n = x.shape[0]
    grid_spec = pltpu.PrefetchScalarGridSpec(
        num_scalar_prefetch=1,
        grid=(n // CMB_TB,),
        in_specs=[pl.BlockSpec(memory_space=pl.ANY),
                  pl.BlockSpec((CMB_TB, D_MODEL), lambda i, pos: (i, 0)),
                  pl.BlockSpec((CMB_TB, TOP_K), lambda i, pos: (i, 0)),
                  pl.BlockSpec((None, 6, D_MODEL), lambda i, pos: (jnp.where(i * CMB_TB >= CTX_LEN, 1, 0), 0, 0))],
        out_specs=pl.BlockSpec((CMB_TB, D_MODEL), lambda i, pos: (i, 0)),
        scratch_shapes=[pltpu.VMEM((2, TOP_K * CMB_TB, D_MODEL), jnp.float32), pltpu.SemaphoreType.DMA((2,))],
    )
    return pl.pallas_call(
        functools.partial(_combine_kernel, gate_row=gate_row),
        grid_spec=grid_spec,
        out_shape=jax.ShapeDtypeStruct((n, D_MODEL), jnp.float32),
        compiler_params=_cparams("arbitrary"),
        name="moe_combine",
    )(pos, ys, x, top_w, mod)


def moe_route(logits, bg, be):
    n = logits.shape[0]
    g_logits = logits[:, :N_GROUPS] + bg
    grp = jnp.argmax(g_logits, axis=-1)
    g_sel = grp[:, None] == jnp.arange(N_GROUPS)[None, :]
    p_grp = jnp.sum(jnp.where(g_sel, jax.nn.softmax(g_logits, axis=-1), 0.0), axis=-1, keepdims=True)
    e_logits = (logits[:, N_GROUPS:N_GROUPS + N_EXPERTS] + be).reshape(n, N_GROUPS, EXP_PER_GROUP)
    e_in = jnp.sum(jnp.where(g_sel[:, :, None], e_logits, 0.0), axis=1)
    lane = jnp.arange(EXP_PER_GROUP)[None, :]
    i1 = jnp.argmax(e_in, axis=-1)
    v1 = jnp.max(e_in, axis=-1)
    rest = jnp.where(lane == i1[:, None], -jnp.inf, e_in)
    i2 = jnp.argmax(rest, axis=-1)
    v2 = jnp.max(rest, axis=-1)
    top_w = jax.nn.softmax(jnp.stack([v1, v2], axis=-1), axis=-1) * p_grp
    eid = grp[:, None] * EXP_PER_GROUP + jnp.stack([i1, i2], axis=-1)
    return eid.astype(jnp.int32), top_w


def moe_schedule(eid):
    n = eid.shape[0]
    flat_e = eid.reshape(-1)
    onehot = (flat_e[:, None] == jnp.arange(N_EXPERTS)[None, :]).astype(jnp.int32)
    rank = jnp.sum(onehot * (jnp.cumsum(onehot, axis=0) - 1), axis=1)
    counts = jnp.sum(onehot, axis=0)
    padded = ((counts + MOE_TM - 1) // MOE_TM) * MOE_TM
    pad_end = jnp.cumsum(padded)
    pad_start = pad_end - padded
    dest = (jnp.sum(onehot * pad_start[None, :], axis=1) + rank).astype(jnp.int32)
    n_rows = MOE_TILES * MOE_TM
    row_token = jnp.zeros((n_rows,), jnp.int32).at[dest].set(jnp.arange(TOP_K * n, dtype=jnp.int32) // TOP_K)
    tile_row0 = jnp.arange(MOE_TILES, dtype=jnp.int32) * MOE_TM
    n_valid = pad_end[-1] // MOE_TM
    tile_valid = (tile_row0 < pad_end[-1]).astype(jnp.int32)
    tile_expert = jnp.sum((tile_row0[:, None] >= pad_end[None, :]).astype(jnp.int32), axis=1)
    last_expert = jnp.sum(jnp.where(jnp.arange(MOE_TILES) == n_valid - 1, tile_expert, 0))
    tile_expert = jnp.where(tile_valid != 0, tile_expert, last_expert).astype(jnp.int32)
    pos = dest.reshape(n // CMB_TB, CMB_TB, TOP_K).transpose(0, 2, 1).reshape(-1)
    return tile_expert, tile_valid, row_token, pos


NA_QR = 4
NA_KR = NA_QR + NA_MAX_ROWS - 1
NA_TQ = NA_QR * GRID_W
NA_TK = NA_KR * GRID_W


def natten_bias_tiles(rpb):
    hi = lax.Precision.HIGHEST
    cols = jnp.arange(GRID_W)
    dc = jnp.clip(cols[None, :] - cols[:, None] + (NA_COLS - 1), 0, 2 * NA_COLS - 2)
    c0 = jnp.clip(cols - NA_COLS // 2, 0, GRID_W - NA_COLS)
    in_win = (cols[None, :] >= c0[:, None]) & (cols[None, :] < c0[:, None] + NA_COLS)
    oh_c = (dc[..., None] == jnp.arange(2 * NA_COLS - 1)).astype(jnp.float32)
    toep = jnp.einsum('qkc,hdc->hdqk', oh_c, rpb.astype(jnp.float32), precision=hi)
    rl = jnp.arange(NA_QR)[:, None]
    kl = jnp.arange(NA_KR)[None, :]
    half = NA_MAX_ROWS // 2
    pats = [(kl - rl + NA_MAX_ROWS - 1, (kl < NA_MAX_ROWS) & (rl >= 0)),
            (kl - rl + NA_MAX_ROWS - 1 - half, (kl - rl >= 0) & (kl - rl < NA_MAX_ROWS)),
            (kl - rl, (kl >= NA_KR - NA_MAX_ROWS) & (rl >= 0))]
    tiles = []
    for dr, valid in pats:
        oh_r = ((dr[..., None] == jnp.arange(2 * NA_MAX_ROWS - 1)) & valid[..., None]).astype(jnp.float32)
        t = jnp.einsum('rkd,hdqc->hrqkc', oh_r, toep, precision=hi)
        ok = valid[None, :, None, :, None] & in_win[None, None, :, None, :]
        tiles.append(jnp.where(ok, t, -jnp.inf).reshape(rpb.shape[0], NA_TQ, NA_TK))
    return jnp.stack(tiles, axis=1)


def _natten_kernel(q_ref, k_ref, v_ref, bias_ref, o_ref, *, n_rows):
    j = pl.program_id(1)
    bf16 = jnp.bfloat16
    nt = (((1,), (1,)), ((), ()))
    q = (q_ref[...] * HEAD_DIM ** -0.5).astype(bf16)
    kc = k_ref[0:CTX_LEN, :].astype(bf16)
    vc = v_ref[0:CTX_LEN, :].astype(bf16)
    s_ctx = lax.dot_general(q, kc, nt, preferred_element_type=jnp.float32)

    @pl.when(j == 0)
    def _():
        m = jnp.max(s_ctx, axis=-1, keepdims=True)
        p = jnp.exp(s_ctx - m)
        l = jnp.sum(p, axis=-1, keepdims=True)
        o = jnp.dot(p.astype(bf16), vc, preferred_element_type=jnp.float32)
        o_ref[...] = (o / l).astype(o_ref.dtype)

    @pl.when(j > 0)
    def _():
        r0 = (j - 1) * NA_QR
        kbase = jnp.clip(r0 - NA_MAX_ROWS // 2, 0, n_rows - NA_KR)
        start = pl.multiple_of(CTX_LEN + kbase * GRID_W, GRID_W)
        kl = k_ref[pl.ds(start, NA_TK), :].astype(bf16)
        vl = v_ref[pl.ds(start, NA_TK), :].astype(bf16)
        s_loc = lax.dot_general(q, kl, nt, preferred_element_type=jnp.float32) + bias_ref[...]
        m = jnp.maximum(jnp.max(s_loc, axis=-1, keepdims=True), jnp.max(s_ctx, axis=-1, keepdims=True))
        p_loc = jnp.exp(s_loc - m)
        p_ctx = jnp.exp(s_ctx - m)
        l = jnp.sum(p_loc, axis=-1, keepdims=True) + jnp.sum(p_ctx, axis=-1, keepdims=True)
        o = (jnp.dot(p_loc.astype(bf16), vl, preferred_element_type=jnp.float32)
             + jnp.dot(p_ctx.astype(bf16), vc, preferred_element_type=jnp.float32))
        o_ref[...] = (o / l).astype(o_ref.dtype)


def natten(p_main, bias_tiles):
    n = p_main.shape[0]
    n_rows = (n - CTX_LEN) // GRID_W
    n_blocks = n_rows // NA_QR
    assert n_rows % NA_QR == 0 and n_rows >= NA_KR and CTX_LEN == NA_TQ

    def pat(j):
        return jnp.where(j <= 1, 0, jnp.where(j == n_blocks, 2, 1))

    return pl.pallas_call(
        functools.partial(_natten_kernel, n_rows=n_rows),
        grid=(N_HEADS, n_blocks + 1),
        in_specs=[pl.BlockSpec((NA_TQ, HEAD_DIM), lambda h, j: (j, COL_Q * HPG + h)),
                  pl.BlockSpec((n, HEAD_DIM), lambda h, j: (0, COL_K * HPG + h)),
                  pl.BlockSpec((n, HEAD_DIM), lambda h, j: (0, COL_V * HPG + h)),
                  pl.BlockSpec((None, None, NA_TQ, NA_TK), lambda h, j: (h, pat(j), 0, 0))],
        out_specs=pl.BlockSpec((NA_TQ, HEAD_DIM), lambda h, j: (j, h)),
        out_shape=jax.ShapeDtypeStruct((n, GROUP_W), jnp.bfloat16),
        compiler_params=_cparams("arbitrary", "arbitrary"),
        name="natten",
    )(p_main, p_main, p_main, bias_tiles)


POOL_HALO = max(POOL_WINDOWS) // 2


def _pool_kernel(prev_ref, cur_ref, next_ref, w_ref, s_ref, o_ref, ext_ref, *, n_tok):
    i = pl.program_id(0)
    t0 = i * ROW_TILE
    seq_lo = jnp.where(t0 < CTX_LEN, 0, CTX_LEN)
    seq_hi = jnp.where(t0 < CTX_LEN, CTX_LEN, n_tok)
    ext_ref[0:POOL_HALO, :] = jnp.where(t0 - POOL_HALO >= seq_lo, prev_ref[...], 0.0)
    ext_ref[POOL_HALO:POOL_HALO + ROW_TILE, :] = cur_ref[...]
    ext_ref[POOL_HALO + ROW_TILE:, :] = jnp.where(t0 + ROW_TILE < seq_hi, next_ref[...], 0.0)
    t = t0 + lax.broadcasted_iota(jnp.int32, (ROW_TILE, 1), 0)
    for g, win in enumerate(POOL_WINDOWS):
        half = win // 2
        cs = slice(g * C_GW, (g + 1) * C_GW)
        acc = ext_ref[POOL_HALO - half:POOL_HALO - half + ROW_TILE, cs]
        for d in range(-half + 1, half):
            acc = acc + ext_ref[POOL_HALO + d:POOL_HALO + d + ROW_TILE, cs]
        cnt = (jnp.minimum(t + half, seq_hi) - jnp.maximum(t - half, seq_lo)).astype(jnp.float32)
        diff = acc / cnt - cur_ref[:, cs]
        y = jnp.dot(diff.astype(jnp.bfloat16), w_ref[g], preferred_element_type=jnp.float32)
        o_ref[:, cs] = (y * s_ref[:, cs]).astype(o_ref.dtype)


def pool_mix(p_main, w_pool, pool_scale):
    n = p_main.shape[0]
    hb = ROW_TILE // POOL_HALO
    n_hblk = n // POOL_HALO
    return pl.pallas_call(
        functools.partial(_pool_kernel, n_tok=n),
        grid=(n // ROW_TILE,),
        in_specs=[pl.BlockSpec((POOL_HALO, GROUP_W), lambda i: (jnp.maximum(i * hb - 1, 0), COL_C)),
                  pl.BlockSpec((ROW_TILE, GROUP_W), lambda i: (i, COL_C)),
                  pl.BlockSpec((POOL_HALO, GROUP_W), lambda i: (jnp.minimum((i + 1) * hb, n_hblk - 1), COL_C)),
                  pl.BlockSpec((C_GROUPS, C_GW, C_GW), lambda i: (0, 0, 0)),
                  pl.BlockSpec((1, GROUP_W), lambda i: (0, 0))],
        out_specs=pl.BlockSpec((ROW_TILE, GROUP_W), lambda i: (i, 0)),
        out_shape=jax.ShapeDtypeStruct((n, GROUP_W), jnp.bfloat16),
        scratch_shapes=[pltpu.VMEM((ROW_TILE + 2 * POOL_HALO, GROUP_W), jnp.float32)],
        compiler_params=_cparams("arbitrary"),
        name="pool_mix",
    )(p_main, p_main, p_main, w_pool, pool_scale.reshape(1, GROUP_W))


def chunk_masks(chunk, reverse):
    i = jnp.arange(ROW_TILE)[:, None]
    j = jnp.arange(ROW_TILE)[None, :]
    same = (i // chunk) == (j // chunk)
    return (same & ((j >= i) if reverse else (j <= i))).astype(jnp.bfloat16)


def _split3(x):
    hi = x.astype(jnp.bfloat16)
    r = x - hi.astype(jnp.float32)
    mid = r.astype(jnp.bfloat16)
    lo = (r - mid.astype(jnp.float32)).astype(jnp.bfloat16)
    return hi, mid, lo


def _dot3(m, x):
    hi, mid, lo = _split3(x)
    d = functools.partial(jnp.dot, preferred_element_type=jnp.float32)
    return d(m, hi) + (d(m, mid) + d(m, lo))


def _scan_block(n_blk, reverse):
    if reverse:
        return lambda j: jnp.where(j == 0, 0, n_blk - j)
    return lambda j: j


def _hgrn_head(q_ref, v_ref, z_ref, lb_ref, mask_ref, st, cols, reverse):
    bf16 = jnp.bfloat16
    f32 = jnp.float32
    T, C = ROW_TILE, A_CHUNK
    nt = (((1,), (1,)), ((), ()))
    tn = (((0,), (0,)), ((), ()))
    z = z_ref[:, cols]
    log_lb = lb_ref[0:1, cols]
    log1m_lb = lb_ref[1:2, cols]
    one_m_lb = lb_ref[2:3, cols]
    log_sig = jnp.minimum(z, 0.0) - jnp.log1p(jnp.exp(-jnp.abs(z)))
    bb = log1m_lb + log_sig
    mx = jnp.maximum(log_lb, bb)
    lf = mx + jnp.log1p(jnp.exp(-jnp.abs(log_lb - bb)))
    k = one_m_lb / (1.0 + jnp.exp(z))
    aq = q_ref[:, cols]
    q = aq / (1.0 + jnp.exp(-aq))
    v = v_ref[:, cols].astype(bf16)

    b = _dot3(mask_ref[...], lf)
    last = 0 if reverse else C - 1
    b_end = jnp.concatenate([jnp.broadcast_to(b[c * C + last:c * C + last + 1, :], (C, HEAD_DIM))
                             for c in range(T // C)], axis=0)
    q_dec = (q * jnp.exp(b)).astype(bf16)
    k_inv = (k * jnp.exp(-b)).astype(bf16)
    k_dec = (k * jnp.exp(b_end - b)).astype(bf16)
    dec = jnp.exp(b_end)

    scores = lax.dot_general(q_dec, k_inv, nt, preferred_element_type=f32)
    scores = jnp.where(mask_ref[...] > 0, scores, 0.0)
    o_intra = jnp.dot(scores.astype(bf16), v, preferred_element_type=f32)

    n_c = T // C
    rows = [slice(c * C, (c + 1) * C) for c in range(n_c)]
    deltas = [lax.dot_general(v[r], k_dec[r], tn, preferred_element_type=f32) for r in rows]
    entering = [None] * n_c
    for c in (range(n_c - 1, -1, -1) if reverse else range(n_c)):
        entering[c] = st
        st = st * dec[c * C:c * C + 1, :] + deltas[c]
    outs = [lax.dot_general(q_dec[r], entering[c].astype(bf16), nt, preferred_element_type=f32)
            for c, r in enumerate(rows)]
    return o_intra + jnp.concatenate(outs, axis=0), st


def _hgrn_kernel(q_ref, v_ref, z_ref, lb_ref, mask_ref, *rest, reverse, hp):
    if reverse:
        g_ref, of_ref, gn_ref, o_ref, st_ref = rest
    else:
        o_ref, st_ref = rest

    @pl.when(pl.program_id(1) == 0)
    def _():
        st_ref[...] = jnp.zeros_like(st_ref)

    for hh in range(hp):
        cols = slice(hh * HEAD_DIM, (hh + 1) * HEAD_DIM)
        o, st = _hgrn_head(q_ref, v_ref, z_ref, lb_ref, mask_ref, st_ref[hh], cols, reverse)
        st_ref[hh] = st
        if reverse:
            o = o + of_ref[:, cols]
            o = o * lax.rsqrt(jnp.mean(o * o, axis=-1, keepdims=True) + EPS) * gn_ref[:, cols]
            ag = g_ref[:, cols]
            o_ref[:, cols] = (o * (ag / (1.0 + jnp.exp(-ag)))).astype(o_ref.dtype)
        else:
            o_ref[:, cols] = o


def hgrn_scan(p_main, lb_rows, mask, reverse, o_fwd=None, gn=None, hp=SCAN_HP):
    n = p_main.shape[0]
    n_blk = n // ROW_TILE
    tb = _scan_block(n_blk, reverse)
    w = hp * HEAD_DIM
    gpc = GROUP_W // w
    blk = lambda c: pl.BlockSpec((ROW_TILE, w), lambda h, j: (tb(j), c * gpc + h))
    in_specs = [blk(COL_AQ), blk(COL_AI), blk(COL_AFB if reverse else COL_AFF),
                pl.BlockSpec((3, w), lambda h, j: (0, h)),
                pl.BlockSpec((ROW_TILE, ROW_TILE), lambda h, j: (0, 0))]
    args = [p_main, p_main, p_main, lb_rows, mask]
    if reverse:
        in_specs += [blk(COL_AG), pl.BlockSpec((ROW_TILE, w), lambda h, j: (tb(j), h)),
                     pl.BlockSpec((1, w), lambda h, j: (0, h))]
        args += [p_main, o_fwd, gn.reshape(1, GROUP_W)]
    return pl.pallas_call(
        functools.partial(_hgrn_kernel, reverse=reverse, hp=hp),
        grid=(gpc, n_blk),
        in_specs=in_specs,
        out_specs=pl.BlockSpec((ROW_TILE, w), lambda h, j: (tb(j), h)),
        out_shape=jax.ShapeDtypeStruct((n, GROUP_W), jnp.bfloat16 if reverse else jnp.float32),
        scratch_shapes=[pltpu.VMEM((hp, HEAD_DIM, HEAD_DIM), jnp.float32)],
        compiler_params=_cparams("arbitrary", "arbitrary"),
        name="hgrn_bwd" if reverse else "hgrn_fwd",
    )(*args)


def hgrn_mix(p_main, lb_f, lb_b, gn):
    rows = lambda lb: jnp.stack([jnp.log(lb), jnp.log1p(-lb), 1.0 - lb])
    o_f = hgrn_scan(p_main, rows(lb_f), chunk_masks(A_CHUNK, False), False)
    return hgrn_scan(p_main, rows(lb_b), chunk_masks(A_CHUNK, True), True, o_f, gn)


def hgrn_lower_bounds(p):
    cs = jnp.cumsum(jax.nn.softmax(p.astype(jnp.float32), axis=1), axis=1)
    return cs - cs[:, :1]


def rope_tables(n_tok):
    quarter = HEAD_DIM // 4
    pos = jnp.arange(n_tok - CTX_LEN)
    inv_freq = ROPE_BASE ** (-jnp.arange(quarter, dtype=jnp.float32) / quarter)
    lane = jnp.arange(HEAD_DIM)
    p = jnp.where(lane[None, :] < HEAD_DIM // 2, (pos // GRID_W)[:, None], (pos % GRID_W)[:, None]).astype(jnp.float32)
    ang = p * inv_freq[lane % quarter][None, :]
    sign = jnp.where((lane % (2 * quarter)) < quarter, -1.0, 1.0)
    cos = jnp.concatenate([jnp.ones((CTX_LEN, HEAD_DIM), jnp.float32), jnp.cos(ang)], axis=0)
    sin = jnp.concatenate([jnp.zeros((CTX_LEN, HEAD_DIM), jnp.float32), jnp.sin(ang) * sign], axis=0)
    return cos, sin


def _rope(x, cos, sin):
    quarter = HEAD_DIM // 4
    lane = lax.broadcasted_iota(jnp.int32, x.shape, 1)
    partner = jnp.where((lane % (2 * quarter)) < quarter,
                        pltpu.roll(x, HEAD_DIM - quarter, axis=1), pltpu.roll(x, quarter, axis=1))
    return x * cos + partner * sin


def _mlstm_head(q_ref, k_ref, v_ref, cos, sin, cols, b_col, bend_col, li_col, b_row, li_row, state, reverse):
    bf16 = jnp.bfloat16
    f32 = jnp.float32
    T, C = ROW_TILE, D_CHUNK
    n_c = T // C
    nt = (((1,), (1,)), ((), ()))
    tn = (((0,), (0,)), ((), ()))
    d = functools.partial(jnp.dot, preferred_element_type=f32)
    q = _rope(q_ref[:, cols], cos, sin) * HEAD_DIM ** -0.5
    k = _rope(k_ref[:, cols], cos, sin)
    qb = q.astype(bf16)
    kb = k.astype(bf16)
    vb = v_ref[:, cols].astype(bf16)
    ii = lax.broadcasted_iota(jnp.int32, (C, C), 0)
    jj = lax.broadcasted_iota(jnp.int32, (C, C), 1)
    causal = (jj >= ii) if reverse else (jj <= ii)
    rows = [slice(c * C, (c + 1) * C) for c in range(n_c)]
    bends, m_locs, d_ss, d_ns = [], [], [], []
    for c, r in enumerate(rows):
        bend = bend_col[c * C:c * C + 1]
        gcol = bend - b_col[r] + li_col[r]
        m_loc = jnp.max(gcol, axis=0, keepdims=True)
        kw = k[r] * jnp.exp(gcol - m_loc)
        bends.append(bend)
        m_locs.append(m_loc)
        d_ss.append(lax.dot_general(kw.astype(bf16), vb[r], tn, preferred_element_type=f32))
        d_ns.append(jnp.sum(kw, axis=0, keepdims=True))
    s, nv, m = state
    entering = [None] * n_c
    for c in (range(n_c - 1, -1, -1) if reverse else range(n_c)):
        entering[c] = (s, nv, m)
        m_new = jnp.maximum(bends[c] + m, m_locs[c])
        a = jnp.exp(bends[c] + m - m_new)
        cc = jnp.exp(m_locs[c] - m_new)
        s = a * s + cc * d_ss[c]
        nv = a * nv + cc * d_ns[c]
        m = m_new
    outs = []
    for c, r in enumerate(rows):
        s_in, n_in, m_in = entering[c]
        dmat = jnp.where(causal, b_col[r] - b_row[:, r] + li_row[:, r], -jnp.inf)
        m_inter = b_col[r] + m_in
        m_t = jnp.maximum(m_inter, jnp.max(dmat, axis=-1, keepdims=True))
        w = jnp.exp(dmat - m_t) * lax.dot_general(qb[r], kb[r], nt, preferred_element_type=f32)
        a_col = jnp.exp(m_inter - m_t)
        num = d(w.astype(bf16), vb[r]) + a_col * d(qb[r], s_in.astype(bf16))
        den = jnp.sum(w, axis=-1, keepdims=True) + a_col * jnp.sum(q[r] * n_in, axis=-1, keepdims=True)
        outs.append(num / jnp.maximum(jnp.abs(den), jnp.exp(-m_t)))
    return jnp.concatenate(outs, axis=0), (s, nv, m)


def _mlstm_kernel(q_ref, k_ref, v_ref, g_ref, cos_ref, sin_ref, mask_ref, maskt_ref, *rest, reverse, hp):
    if reverse:
        do_ref, of_ref, gn_ref, o_ref, s_ref, n_ref, m_ref, cumt_ref, lit_ref = rest
    else:
        o_ref, s_ref, n_ref, m_ref, cumt_ref, lit_ref = rest
    hg = pl.program_id(0)
    f32 = jnp.float32
    T, C = ROW_TILE, D_CHUNK

    @pl.when(pl.program_id(1) == 0)
    def _():
        s_ref[...] = jnp.zeros_like(s_ref)
        n_ref[...] = jnp.zeros_like(n_ref)
        m_ref[...] = jnp.zeros_like(m_ref)

    g = g_ref[...]
    lsg = jnp.minimum(g, 0.0) - jnp.log1p(jnp.exp(-jnp.abs(g)))
    cum = _dot3(mask_ref[...], lsg)
    hi, mid, lo = _split3(lsg.T)
    d = functools.partial(jnp.dot, preferred_element_type=f32)
    cumt_ref[...] = d(hi, maskt_ref[...]) + (d(mid, maskt_ref[...]) + d(lo, maskt_ref[...]))
    lit_ref[...] = g.T
    lane = lax.broadcasted_iota(jnp.int32, (T, LANES), 1)
    pick = lambda x, c: jnp.sum(jnp.where(lane == c, x, 0.0), axis=1, keepdims=True)
    last = 0 if reverse else C - 1
    cos = cos_ref[...]
    sin = sin_ref[...]
    for hh in range(hp):
        h = hg * hp + hh
        col_i = (GATE_I_B if reverse else GATE_I_F) * N_HEADS + h
        col_f = (GATE_F_B if reverse else GATE_F_F) * N_HEADS + h
        b_col = pick(cum, col_f)
        bend_col = jnp.concatenate([jnp.broadcast_to(b_col[c * C + last:c * C + last + 1], (C, 1))
                                    for c in range(T // C)], axis=0)
        li_col = pick(g, col_i)
        b_row = cumt_ref[pl.ds(col_f, 1), :]
        li_row = lit_ref[pl.ds(col_i, 1), :]
        cols = slice(hh * HEAD_DIM, (hh + 1) * HEAD_DIM)
        state = (s_ref[hh], n_ref[hh, 0:1, :], m_ref[hh, 0:1, 0:1])
        o, (s, nv, m) = _mlstm_head(q_ref, k_ref, v_ref, cos, sin, cols, b_col, bend_col, li_col, b_row, li_row,
                                    state, reverse)
        s_ref[hh] = s
        n_ref[hh] = jnp.broadcast_to(nv, n_ref.shape[1:])
        m_ref[hh] = jnp.broadcast_to(m, m_ref.shape[1:])
        if reverse:
            o = o + of_ref[:, cols]
            o = o * lax.rsqrt(jnp.mean(o * o, axis=-1, keepdims=True) + EPS) * gn_ref[:, cols]
            o_ref[:, cols] = (o / (1.0 + jnp.exp(-do_ref[:, cols]))).astype(o_ref.dtype)
        else:
            o_ref[:, cols] = o


def mlstm_scan(p_main, p_gate, cos, sin, mask, reverse, o_fwd=None, gn=None, hp=SCAN_HP):
    n = p_main.shape[0]
    n_blk = n // ROW_TILE
    tb = _scan_block(n_blk, reverse)
    w = hp * HEAD_DIM
    gpc = GROUP_W // w
    blk = lambda c: pl.BlockSpec((ROW_TILE, w), lambda h, j: (tb(j), c * gpc + h))
    tok = pl.BlockSpec((ROW_TILE, LANES), lambda h, j: (tb(j), 0))
    sq = pl.BlockSpec((ROW_TILE, ROW_TILE), lambda h, j: (0, 0))
    in_specs = [blk(COL_DQ), blk(COL_DK), blk(COL_DV), tok, tok, tok, sq, sq]
    args = [p_main, p_main, p_main, p_gate, cos, sin, mask, mask.T]
    if reverse:
        in_specs += [blk(COL_DO), pl.BlockSpec((ROW_TILE, w), lambda h, j: (tb(j), h)),
                     pl.BlockSpec((1, w), lambda h, j: (0, h))]
        args += [p_main, o_fwd, gn.reshape(1, GROUP_W)]
    return pl.pallas_call(
        functools.partial(_mlstm_kernel, reverse=reverse, hp=hp),
        grid=(gpc, n_blk),
        in_specs=in_specs,
        out_specs=pl.BlockSpec((ROW_TILE, w), lambda h, j: (tb(j), h)),
        out_shape=jax.ShapeDtypeStruct((n, GROUP_W), jnp.bfloat16 if reverse else jnp.float32),
        scratch_shapes=[pltpu.VMEM((hp, HEAD_DIM, HEAD_DIM), jnp.float32), pltpu.VMEM((hp, 8, HEAD_DIM), jnp.float32),
                        pltpu.VMEM((hp, 8, LANES), jnp.float32), pltpu.VMEM((LANES, ROW_TILE), jnp.float32),
                        pltpu.VMEM((LANES, ROW_TILE), jnp.float32)],
        compiler_params=_cparams("arbitrary", "arbitrary"),
        name="mlstm_bwd" if reverse else "mlstm_fwd",
    )(*args)


def mlstm_mix(p_main, p_gate, gn):
    cos, sin = rope_tables(p_main.shape[0])
    o_f = mlstm_scan(p_main, p_gate, cos, sin, chunk_masks(D_CHUNK, False), False)
    return mlstm_scan(p_main, p_gate, cos, sin, chunk_masks(D_CHUNK, True), True, o_f, gn)


def kernel(x, c, ctx, c_ctx, ada_w, ada_b, norm1_w, norm2_w, mix_w_in, mix_b_in, hgrn_lb, hgrn_gn, natten_rpb,
           pool_w, pool_scale, mlstm_gn, mix_w_out, moe_wg, moe_bg, moe_we, moe_be, moe_w1, moe_w3, moe_w2, final_w):
    assert x.shape == (1, SEQ, D_MODEL) and ctx.shape == (1, CTX_LEN, D_MODEL)
    bf16 = jnp.bfloat16
    lbs = hgrn_lower_bounds(hgrn_lb)

    cond = jnp.zeros((16, D_MODEL), jnp.float32).at[0].set(jax.nn.silu(c_ctx)).at[1].set(jax.nn.silu(c[0]))
    mods = ada_modulation(cond.astype(bf16), ada_w, ada_b)[:, :2].reshape(DEPTH, 2, 6, D_MODEL)

    xs = jnp.concatenate([ctx[0], x[0]], axis=0)
    for l in range(DEPTH):
        mod = mods[l]
        w_in = mix_w_in[l].astype(bf16)
        h = norm_modulate(xs, norm1_w[l], mod, 0)
        p_main = matmul_bias(h, w_in, mix_b_in[l], MM_TM, MM_TN, n=N_MAIN)
        w_gate = jnp.pad(w_in[:, N_MAIN:], ((0, 0), (0, LANES - N_GATE)))
        b_gate = jnp.pad(mix_b_in[l, N_MAIN:], (0, LANES - N_GATE))
        p_gate = matmul_bias(h, w_gate, b_gate, MM_TM, LANES)

        out_a = hgrn_mix(p_main, lbs[0, l], lbs[1, l], hgrn_gn[l])
        out_b = natten(p_main, natten_bias_tiles(natten_rpb[l]))
        out_c = pool_mix(p_main, pool_w[l].astype(bf16), pool_scale[l])
        out_d = mlstm_mix(p_main, p_gate, mlstm_gn[l])
        xs = matmul_gated_residual((out_a, out_b, out_c, out_d), mix_w_out[l].astype(bf16), xs, mod, 2)

        w_router = jnp.pad(jnp.concatenate([moe_wg[l], moe_we[l]], axis=1),
                           ((0, 0), (0, LANES - N_GROUPS - N_EXPERTS)))
        h2, logits = norm_modulate_router(xs, norm2_w[l], mod, w_router)
        eid, top_w = moe_route(logits, moe_bg[l], moe_be[l])
        tile_expert, tile_valid, row_token, cpos = moe_schedule(eid)
        ys = moe_grouped(h2, tile_expert, tile_valid, row_token,
                         moe_w1[l].astype(bf16), moe_w3[l].astype(bf16), moe_w2[l].astype(bf16))
        xs = moe_combine(cpos, ys, xs, top_w, mod, 5)

    return norm_modulate(xs, final_w, mods[0], None, out_dtype=jnp.float32, skip_rows=CTX_LEN)[None]
```

```python
import functools

import jax
import jax.numpy as jnp
from jax import lax
from jax.experimental import pallas as pl
from jax.experimental.pallas import tpu as pltpu

D_MODEL = 4096
SEQ = 8192
DEPTH = 2
GRID_W = 64
CTX_LEN = 256
N_TOK = CTX_LEN + SEQ
GROUP_W = 1024
HEAD_DIM = 128
N_HEADS = GROUP_W // HEAD_DIM
C_GROUPS = 4
C_GW = GROUP_W // C_GROUPS
POOL_WINDOWS = (2, 4, 8, 16)
NA_MAX_ROWS = 8
NA_COLS = 16
A_CHUNK = 32
D_CHUNK = 64
ROPE_BASE = 10000.0
N_GROUPS = 4
EXP_PER_GROUP = 8
N_EXPERTS = N_GROUPS * EXP_PER_GROUP
TOP_K = 2
D_FF = D_MODEL // 8
EPS = 1e-6
N_MAIN = 13 * GROUP_W
N_GATE = 4 * N_HEADS
LANES = 128
VMEM_LIMIT = 56 * 1024 * 1024

ROW_TILE = 256
MM_TM = 768
MM_TN = 512
MOE_TM = 256
MOE_TILES = (TOP_K * N_TOK + N_EXPERTS * (MOE_TM - 1)) // MOE_TM + 1
CMB_TB = 128
SCAN_HP = 8
DMA_UNROLL = 8

COL_AQ, COL_AI, COL_AG, COL_AFF, COL_AFB = 0, 1, 2, 3, 4
COL_Q, COL_K, COL_V, COL_C = 5, 6, 7, 8
COL_DQ, COL_DK, COL_DV, COL_DO = 9, 10, 11, 12
GATE_I_F, GATE_F_F, GATE_I_B, GATE_F_B = 0, 1, 2, 3
HPG = GROUP_W // HEAD_DIM


def _cparams(*sem):
    return pltpu.CompilerParams(dimension_semantics=sem, vmem_limit_bytes=VMEM_LIMIT)


def _ada_kernel(c_ref, w_ref, b_ref, o_ref):
    acc = jnp.dot(c_ref[...], w_ref[...].astype(jnp.bfloat16), preferred_element_type=jnp.float32)
    o_ref[...] = acc + b_ref[...]


def ada_modulation(cond, ada_w, ada_b):
    tn = 512
    n = ada_w.shape[-1]
    rows = cond.shape[0]
    return pl.pallas_call(
        _ada_kernel,
        grid=(DEPTH, n // tn),
        in_specs=[pl.BlockSpec((rows, D_MODEL), lambda l, j: (0, 0)),
                  pl.BlockSpec((None, D_MODEL, tn), lambda l, j: (l, 0, j)),
                  pl.BlockSpec((None, 1, tn), lambda l, j: (l, 0, j))],
        out_specs=pl.BlockSpec((None, rows, tn), lambda l, j: (l, 0, j)),
        out_shape=jax.ShapeDtypeStruct((DEPTH, rows, n), jnp.float32),
        compiler_params=_cparams("arbitrary", "arbitrary"),
        name="ada_modulation",
    )(cond, ada_w, ada_b.reshape(DEPTH, 1, n))


def _norm_kernel(x_ref, w_ref, mod_ref, o_ref, *, shift_row, scale_row):
    x = x_ref[...]
    y = x * lax.rsqrt(jnp.mean(x * x, axis=-1, keepdims=True) + EPS) * w_ref[...]
    if shift_row is not None:
        y = y * (1.0 + mod_ref[scale_row:scale_row + 1, :]) + mod_ref[shift_row:shift_row + 1, :]
    o_ref[...] = y.astype(o_ref.dtype)


def _norm_router_kernel(x_ref, w_ref, mod_ref, wr_ref, o_ref, lg_ref, *, shift_row, scale_row):
    x = x_ref[...]
    y = x * lax.rsqrt(jnp.mean(x * x, axis=-1, keepdims=True) + EPS) * w_ref[...]
    y = y * (1.0 + mod_ref[scale_row:scale_row + 1, :]) + mod_ref[shift_row:shift_row + 1, :]
    o_ref[...] = y.astype(o_ref.dtype)
    y_hi = y.astype(jnp.bfloat16)
    y_lo = (y - y_hi.astype(jnp.float32)).astype(jnp.bfloat16)
    w = wr_ref[...]
    w_hi = w.astype(jnp.bfloat16)
    w_lo = (w - w_hi.astype(jnp.float32)).astype(jnp.bfloat16)
    dot = functools.partial(jnp.dot, preferred_element_type=jnp.float32)
    lg_ref[...] = dot(y_hi, w_hi) + (dot(y_lo, w_hi) + dot(y_hi, w_lo))


def _tok_type(i):
    return jnp.where(i * ROW_TILE >= CTX_LEN, 1, 0)


def norm_modulate(x, w, mod, which, out_dtype=jnp.bfloat16, skip_rows=0):
    n = x.shape[0] - skip_rows
    off = skip_rows // ROW_TILE
    rows = (None, None) if which is None else (3 * which, 3 * which + 1)
    return pl.pallas_call(
        functools.partial(_norm_kernel, shift_row=rows[0], scale_row=rows[1]),
        grid=(n // ROW_TILE,),
        in_specs=[pl.BlockSpec((ROW_TILE, D_MODEL), lambda i: (i + off, 0)),
                  pl.BlockSpec((1, D_MODEL), lambda i: (0, 0)),
                  pl.BlockSpec((None, 6, D_MODEL), lambda i: (_tok_type(i + off), 0, 0))],
        out_specs=pl.BlockSpec((ROW_TILE, D_MODEL), lambda i: (i, 0)),
        out_shape=jax.ShapeDtypeStruct((n, D_MODEL), out_dtype),
        compiler_params=_cparams("arbitrary"),
        name="norm_modulate",
    )(x, w.reshape(1, D_MODEL), mod)


def norm_modulate_router(x, w, mod, w_router):
    n = x.shape[0]
    return pl.pallas_call(
        functools.partial(_norm_router_kernel, shift_row=3, scale_row=4),
        grid=(n // ROW_TILE,),
        in_specs=[pl.BlockSpec((ROW_TILE, D_MODEL), lambda i: (i, 0)),
                  pl.BlockSpec((1, D_MODEL), lambda i: (0, 0)),
                  pl.BlockSpec((None, 6, D_MODEL), lambda i: (_tok_type(i), 0, 0)),
                  pl.BlockSpec((D_MODEL, LANES), lambda i: (0, 0))],
        out_specs=[pl.BlockSpec((ROW_TILE, D_MODEL), lambda i: (i, 0)),
                   pl.BlockSpec((ROW_TILE, LANES), lambda i: (i, 0))],
        out_shape=[jax.ShapeDtypeStruct((n, D_MODEL), jnp.float32),
                   jax.ShapeDtypeStruct((n, LANES), jnp.float32)],
        compiler_params=_cparams("arbitrary"),
        name="norm_modulate_router",
    )(x, w.reshape(1, D_MODEL), mod, w_router)


def _mm_bias_kernel(a_ref, b_ref, bias_ref, o_ref):
    o_ref[...] = jnp.dot(a_ref[...], b_ref[...], preferred_element_type=jnp.float32) + bias_ref[...]


def _mm_resid_kernel(a0_ref, a1_ref, a2_ref, a3_ref, b_ref, x_ref, mod_ref, o_ref, *, gate_row):
    acc = None
    for g, a_ref in enumerate((a0_ref, a1_ref, a2_ref, a3_ref)):
        part = jnp.dot(a_ref[...], b_ref[g * GROUP_W:(g + 1) * GROUP_W, :], preferred_element_type=jnp.float32)
        acc = part if acc is None else acc + part
    tm = x_ref.shape[0]
    row = pl.program_id(0) * tm + lax.broadcasted_iota(jnp.int32, (tm, 1), 0)
    gate = jnp.where(row < CTX_LEN, mod_ref[0, gate_row:gate_row + 1, :], mod_ref[1, gate_row:gate_row + 1, :])
    o_ref[...] = x_ref[...] + gate * acc


def matmul_bias(a, b, bias, tm, tn, n=None, layer=None):
    m, k = a.shape
    n = b.shape[-1] if n is None else n
    if layer is None:
        b_spec = pl.BlockSpec((k, tn), lambda i, j: (0, j))
    else:
        b_spec = pl.BlockSpec((None, k, tn), lambda i, j: (layer, 0, j))
    return pl.pallas_call(
        _mm_bias_kernel,
        grid=(m // tm, n // tn),
        in_specs=[pl.BlockSpec((tm, k), lambda i, j: (i, 0)),
                  b_spec,
                  pl.BlockSpec((1, tn), lambda i, j: (0, j))],
        out_specs=pl.BlockSpec((tm, tn), lambda i, j: (i, j)),
        out_shape=jax.ShapeDtypeStruct((m, n), jnp.float32),
        compiler_params=_cparams("arbitrary", "arbitrary"),
        name="matmul_bias",
    )(a, b, bias.reshape(1, -1))


def matmul_gated_residual(parts, b, layer, x, mod, gate_row):
    m, n = x.shape
    tm, tn = MM_TM, MM_TN
    a_spec = pl.BlockSpec((tm, GROUP_W), lambda i, j: (i, 0))
    return pl.pallas_call(
        functools.partial(_mm_resid_kernel, gate_row=gate_row),
        grid=(m // tm, n // tn),
        in_specs=[a_spec, a_spec, a_spec, a_spec,
                  pl.BlockSpec((None, b.shape[1], tn), lambda i, j: (layer, 0, j)),
                  pl.BlockSpec((tm, tn), lambda i, j: (i, j)),
                  pl.BlockSpec((2, 6, tn), lambda i, j: (0, 0, j))],
        out_specs=pl.BlockSpec((tm, tn), lambda i, j: (i, j)),
        out_shape=jax.ShapeDtypeStruct((m, n), jnp.float32),
        compiler_params=_cparams("arbitrary", "arbitrary"),
        name="matmul_gated_residual",
    )(*parts, b, x, mod)


def _moe_kernel(te_ref, tv_ref, rt_ref, h_hbm, w1_ref, w3_ref, w2_ref, y_ref, xg_ref, sem_ref):
    del te_ref
    i = pl.program_id(0)
    n_tiles = pl.num_programs(0)

    def row_copy(tile, slot, r):
        tok = rt_ref[tile * MOE_TM + r]
        return pltpu.make_async_copy(h_hbm.at[pl.ds(tok, 1), :], xg_ref.at[slot, pl.ds(r, 1), :], sem_ref.at[slot])

    def start_gather(tile, slot):
        @pl.when(tv_ref[tile] != 0)
        def _():
            def body(r, carry):
                row_copy(tile, slot, r).start()
                return carry
            lax.fori_loop(0, MOE_TM, body, 0, unroll=DMA_UNROLL)

    @pl.when(i == 0)
    def _():
        start_gather(0, 0)

    @pl.when(i + 1 < n_tiles)
    def _():
        start_gather(i + 1, (i + 1) % 2)

    @pl.when(tv_ref[i] != 0)
    def _():
        slot = i % 2

        def wait_body(r, carry):
            row_copy(i, slot, r).wait()
            return carry
        lax.fori_loop(0, MOE_TM, wait_body, 0, unroll=DMA_UNROLL)
        xg = xg_ref[slot].astype(jnp.bfloat16)
        a = jnp.dot(xg, w1_ref[...], preferred_element_type=jnp.float32)
        b = jnp.dot(xg, w3_ref[...], preferred_element_type=jnp.float32)
        act = (a * jax.nn.sigmoid(a)) * b
        y_ref[...] = jnp.dot(act.astype(jnp.bfloat16), w2_ref[...], preferred_element_type=jnp.float32)

    @pl.when(tv_ref[i] == 0)
    def _():
        y_ref[...] = jnp.zeros_like(y_ref)


def moe_grouped(h, tile_expert, tile_valid, row_token, w1, w3, w2, layer):
    grid_spec = pltpu.PrefetchScalarGridSpec(
        num_scalar_prefetch=3,
        grid=(MOE_TILES,),
        in_specs=[pl.BlockSpec(memory_space=pl.ANY),
                  pl.BlockSpec((None, None, D_MODEL, D_FF), lambda i, te, tv, rt: (layer, te[i], 0, 0)),
                  pl.BlockSpec((None, None, D_MODEL, D_FF), lambda i, te, tv, rt: (layer, te[i], 0, 0)),
                  pl.BlockSpec((None, None, D_FF, D_MODEL), lambda i, te, tv, rt: (layer, te[i], 0, 0))],
        out_specs=pl.BlockSpec((MOE_TM, D_MODEL), lambda i, te, tv, rt: (i, 0)),
        scratch_shapes=[pltpu.VMEM((2, MOE_TM, D_MODEL), jnp.float32), pltpu.SemaphoreType.DMA((2,))],
    )
    return pl.pallas_call(
        _moe_kernel,
        grid_spec=grid_spec,
        out_shape=jax.ShapeDtypeStruct((MOE_TILES * MOE_TM, D_MODEL), jnp.float32),
        compiler_params=_cparams("arbitrary"),
        name="moe_grouped",
    )(tile_expert, tile_valid, row_token, h, w1, w3, w2)


def _combine_kernel(pos_ref, y_hbm, x_ref, w_ref, mod_ref, o_ref, buf_ref, sem_ref, *, gate_row):
    i = pl.program_id(0)
    n_steps = pl.num_programs(0)

    def row_copy(step, slot, r):
        src = pos_ref[step * (TOP_K * CMB_TB) + r]
        return pltpu.make_async_copy(y_hbm.at[pl.ds(src, 1), :], buf_ref.at[slot, pl.ds(r, 1), :], sem_ref.at[slot])

    def start_gather(step, slot):
        def body(r, carry):
            row_copy(step, slot, r).start()
            return carry
        lax.fori_loop(0, TOP_K * CMB_TB, body, 0, unroll=DMA_UNROLL)

    @pl.when(i == 0)
    def _():
        start_gather(0, 0)

    @pl.when(i + 1 < n_steps)
    def _():
        start_gather(i + 1, (i + 1) % 2)

    slot = i % 2

    def wait_body(r, carry):
        row_copy(i, slot, r).wait()
        return carry
    lax.fori_loop(0, TOP_K * CMB_TB, wait_body, 0, unroll=DMA_UNROLL)
    w = w_ref[...]
    moe = w[:, 0:1] * buf_ref[slot, 0:CMB_TB, :] + w[:, 1:2] * buf_ref[slot, CMB_TB:2 * CMB_TB, :]
    o_ref[...] = x_ref[...] + mod_ref[gate_row:gate_row + 1, :] * moe


def moe_combine(pos, ys, x, top_w, mod, gate_row):
    n = x.shape[0]
    grid_spec = pltpu.PrefetchScalarGridSpec(
        num_scalar_prefetch=1,
        grid=(n // CMB_TB,),
        in_specs=[pl.BlockSpec(memory_space=pl.ANY),
                  pl.BlockSpec((CMB_TB, D_MODEL), lambda i, pos: (i, 0)),
                  pl.BlockSpec((CMB_TB, TOP_K), lambda i, pos: (i, 0)),
                  pl.BlockSpec((None, 6, D_MODEL), lambda i, pos: (jnp.where(i * CMB_TB >= CTX_LEN, 1, 0), 0, 0))],
        out_specs=pl.BlockSpec((CMB_TB, D_MODEL), lambda i, pos: (i, 0)),
        scratch_shapes=[pltpu.VMEM((2, TOP_K * CMB_TB, D_MODEL), jnp.float32), pltpu.SemaphoreType.DMA((2,))],
    )
    return pl.pallas_call(
        functools.partial(_combine_kernel, gate_row=gate_row),
        grid_spec=grid_spec,
        out_shape=jax.ShapeDtypeStruct((n, D_MODEL), jnp.float32),
        compiler_params=_cparams("arbitrary"),
        name="moe_combine",
    )(pos, ys, x, top_w, mod)


def moe_route(logits, bg, be):
    n = logits.shape[0]
    g_logits = logits[:, :N_GROUPS] + bg
    grp = jnp.argmax(g_logits, axis=-1)
    g_sel = grp[:, None] == jnp.arange(N_GROUPS)[None, :]
    p_grp = jnp.sum(jnp.where(g_sel, jax.nn.softmax(g_logits, axis=-1), 0.0), axis=-1, keepdims=True)
    e_logits = (logits[:, N_GROUPS:N_GROUPS + N_EXPERTS] + be).reshape(n, N_GROUPS, EXP_PER_GROUP)
    e_in = jnp.sum(jnp.where(g_sel[:, :, None], e_logits, 0.0), axis=1)
    lane = jnp.arange(EXP_PER_GROUP)[None, :]
    i1 = jnp.argmax(e_in, axis=-1)
    v1 = jnp.max(e_in, axis=-1)
    rest = jnp.where(lane == i1[:, None], -jnp.inf, e_in)
    i2 = jnp.argmax(rest, axis=-1)
    v2 = jnp.max(rest, axis=-1)
    top_w = jax.nn.softmax(jnp.stack([v1, v2], axis=-1), axis=-1) * p_grp
    eid = grp[:, None] * EXP_PER_GROUP + jnp.stack([i1, i2], axis=-1)
    return eid.astype(jnp.int32), top_w


def moe_schedule(eid):
    n = eid.shape[0]
    flat_e = eid.reshape(-1)
    onehot = (flat_e[:, None] == jnp.arange(N_EXPERTS)[None, :]).astype(jnp.int32)
    rank = jnp.sum(onehot * (jnp.cumsum(onehot, axis=0) - 1), axis=1)
    counts = jnp.sum(onehot, axis=0)
    padded = ((counts + MOE_TM - 1) // MOE_TM) * MOE_TM
    pad_end = jnp.cumsum(padded)
    pad_start = pad_end - padded
    dest = (jnp.sum(onehot * pad_start[None, :], axis=1) + rank).astype(jnp.int32)
    n_rows = MOE_TILES * MOE_TM
    row_token = jnp.zeros((n_rows,), jnp.int32).at[dest].set(jnp.arange(TOP_K * n, dtype=jnp.int32) // TOP_K)
    tile_row0 = jnp.arange(MOE_TILES, dtype=jnp.int32) * MOE_TM
    n_valid = pad_end[-1] // MOE_TM
    tile_valid = (tile_row0 < pad_end[-1]).astype(jnp.int32)
    tile_expert = jnp.sum((tile_row0[:, None] >= pad_end[None, :]).astype(jnp.int32), axis=1)
    last_expert = jnp.sum(jnp.where(jnp.arange(MOE_TILES) == n_valid - 1, tile_expert, 0))
    tile_expert = jnp.where(tile_valid != 0, tile_expert, last_expert).astype(jnp.int32)
    pos = dest.reshape(n // CMB_TB, CMB_TB, TOP_K).transpose(0, 2, 1).reshape(-1)
    return tile_expert, tile_valid, row_token, pos


NA_QR = 4
NA_KR = NA_QR + NA_MAX_ROWS - 1
NA_TQ = NA_QR * GRID_W
NA_TK = NA_KR * GRID_W
NA_HP = 2


def natten_bias_tiles(rpb):
    hi = lax.Precision.HIGHEST
    cols = jnp.arange(GRID_W)
    dc = jnp.clip(cols[None, :] - cols[:, None] + (NA_COLS - 1), 0, 2 * NA_COLS - 2)
    c0 = jnp.clip(cols - NA_COLS // 2, 0, GRID_W - NA_COLS)
    in_win = (cols[None, :] >= c0[:, None]) & (cols[None, :] < c0[:, None] + NA_COLS)
    oh_c = (dc[..., None] == jnp.arange(2 * NA_COLS - 1)).astype(jnp.float32)
    toep = jnp.einsum('qkc,hdc->hdqk', oh_c, rpb.astype(jnp.float32), precision=hi)
    rl = jnp.arange(NA_QR)[:, None]
    kl = jnp.arange(NA_KR)[None, :]
    half = NA_MAX_ROWS // 2
    pats = [(kl - rl + NA_MAX_ROWS - 1, (kl < NA_MAX_ROWS) & (rl >= 0)),
            (kl - rl + NA_MAX_ROWS - 1 - half, (kl - rl >= 0) & (kl - rl < NA_MAX_ROWS)),
            (kl - rl, (kl >= NA_KR - NA_MAX_ROWS) & (rl >= 0))]
    tiles = []
    for dr, valid in pats:
        oh_r = ((dr[..., None] == jnp.arange(2 * NA_MAX_ROWS - 1)) & valid[..., None]).astype(jnp.float32)
        t = jnp.einsum('rkd,hdqc->hrqkc', oh_r, toep, precision=hi)
        ok = valid[None, :, None, :, None] & in_win[None, None, :, None, :]
        tiles.append(jnp.where(ok, t, -jnp.inf).reshape(rpb.shape[0], NA_TQ, NA_TK))
    return jnp.stack(tiles, axis=1)


def _natten_kernel(q_ref, k_ref, v_ref, bias_ref, o_ref, *, n_rows, hp):
    j = pl.program_id(1)
    bf16 = jnp.bfloat16
    f32 = jnp.float32
    nt = (((1,), (1,)), ((), ()))
    heads = [slice(hh * HEAD_DIM, (hh + 1) * HEAD_DIM) for hh in range(hp)]

    def ctx_scores(cols):
        q = (q_ref[:, cols] * HEAD_DIM ** -0.5).astype(bf16)
        kc = k_ref[0:CTX_LEN, cols].astype(bf16)
        return q, lax.dot_general(q, kc, nt, preferred_element_type=f32)

    @pl.when(j == 0)
    def _():
        for cols in heads:
            _, s_ctx = ctx_scores(cols)
            m = jnp.max(s_ctx, axis=-1, keepdims=True)
            p = jnp.exp(s_ctx - m)
            l = jnp.sum(p, axis=-1, keepdims=True)
            o = jnp.dot(p.astype(bf16), v_ref[0:CTX_LEN, cols].astype(bf16), preferred_element_type=f32)
            o_ref[:, cols] = (o / l).astype(o_ref.dtype)

    @pl.when(j > 0)
    def _():
        r0 = (j - 1) * NA_QR
        kbase = jnp.clip(r0 - NA_MAX_ROWS // 2, 0, n_rows - NA_KR)
        start = pl.multiple_of(CTX_LEN + kbase * GRID_W, GRID_W)
        for hh, cols in enumerate(heads):
            q, s_ctx = ctx_scores(cols)
            kl = k_ref[pl.ds(start, NA_TK), cols].astype(bf16)
            vl = v_ref[pl.ds(start, NA_TK), cols].astype(bf16)
            s_loc = lax.dot_general(q, kl, nt, preferred_element_type=f32) + bias_ref[hh]
            m = jnp.maximum(jnp.max(s_loc, axis=-1, keepdims=True), jnp.max(s_ctx, axis=-1, keepdims=True))
            p_loc = jnp.exp(s_loc - m)
            p_ctx = jnp.exp(s_ctx - m)
            l = jnp.sum(p_loc, axis=-1, keepdims=True) + jnp.sum(p_ctx, axis=-1, keepdims=True)
            o = (jnp.dot(p_loc.astype(bf16), vl, preferred_element_type=f32)
                 + jnp.dot(p_ctx.astype(bf16), v_ref[0:CTX_LEN, cols].astype(bf16), preferred_element_type=f32))
            o_ref[:, cols] = (o / l).astype(o_ref.dtype)


def natten(p_main, bias_tiles, hp=NA_HP):
    n = p_main.shape[0]
    n_rows = (n - CTX_LEN) // GRID_W
    n_blocks = n_rows // NA_QR
    assert n_rows % NA_QR == 0 and n_rows >= NA_KR and CTX_LEN == NA_TQ
    w = hp * HEAD_DIM
    gpc = GROUP_W // w

    def pat(j):
        return jnp.where(j <= 1, 0, jnp.where(j == n_blocks, 2, 1))

    return pl.pallas_call(
        functools.partial(_natten_kernel, n_rows=n_rows, hp=hp),
        grid=(gpc, n_blocks + 1),
        in_specs=[pl.BlockSpec((NA_TQ, w), lambda h, j: (j, COL_Q * gpc + h)),
                  pl.BlockSpec((n, w), lambda h, j: (0, COL_K * gpc + h)),
                  pl.BlockSpec((n, w), lambda h, j: (0, COL_V * gpc + h)),
                  pl.BlockSpec((hp, None, NA_TQ, NA_TK), lambda h, j: (h, pat(j), 0, 0))],
        out_specs=pl.BlockSpec((NA_TQ, w), lambda h, j: (j, h)),
        out_shape=jax.ShapeDtypeStruct((n, GROUP_W), jnp.bfloat16),
        compiler_params=_cparams("arbitrary", "arbitrary"),
        name="natten",
    )(p_main, p_main, p_main, bias_tiles)


POOL_HALO = max(POOL_WINDOWS) // 2


def _pool_kernel(prev_ref, cur_ref, next_ref, w_ref, s_ref, o_ref, ext_ref, *, n_tok):
    i = pl.program_id(0)
    t0 = i * ROW_TILE
    seq_lo = jnp.where(t0 < CTX_LEN, 0, CTX_LEN)
    seq_hi = jnp.where(t0 < CTX_LEN, CTX_LEN, n_tok)
    ext_ref[0:POOL_HALO, :] = jnp.where(t0 - POOL_HALO >= seq_lo, prev_ref[...], 0.0)
    ext_ref[POOL_HALO:POOL_HALO + ROW_TILE, :] = cur_ref[...]
    ext_ref[POOL_HALO + ROW_TILE:, :] = jnp.where(t0 + ROW_TILE < seq_hi, next_ref[...], 0.0)
    t = t0 + lax.broadcasted_iota(jnp.int32, (ROW_TILE, 1), 0)
    for g, win in enumerate(POOL_WINDOWS):
        half = win // 2
        cs = slice(g * C_GW, (g + 1) * C_GW)
        acc = ext_ref[POOL_HALO - half:POOL_HALO - half + ROW_TILE, cs]
        for d in range(-half + 1, half):
            acc = acc + ext_ref[POOL_HALO + d:POOL_HALO + d + ROW_TILE, cs]
        cnt = (jnp.minimum(t + half, seq_hi) - jnp.maximum(t - half, seq_lo)).astype(jnp.float32)
        diff = acc / cnt - cur_ref[:, cs]
        y = jnp.dot(diff.astype(jnp.bfloat16), w_ref[g], preferred_element_type=jnp.float32)
        o_ref[:, cs] = (y * s_ref[:, cs]).astype(o_ref.dtype)


def pool_mix(p_main, w_pool, pool_scale):
    n = p_main.shape[0]
    hb = ROW_TILE // POOL_HALO
    n_hblk = n // POOL_HALO
    return pl.pallas_call(
        functools.partial(_pool_kernel, n_tok=n),
        grid=(n // ROW_TILE,),
        in_specs=[pl.BlockSpec((POOL_HALO, GROUP_W), lambda i: (jnp.maximum(i * hb - 1, 0), COL_C)),
                  pl.BlockSpec((ROW_TILE, GROUP_W), lambda i: (i, COL_C)),
                  pl.BlockSpec((POOL_HALO, GROUP_W), lambda i: (jnp.minimum((i + 1) * hb, n_hblk - 1), COL_C)),
                  pl.BlockSpec((C_GROUPS, C_GW, C_GW), lambda i: (0, 0, 0)),
                  pl.BlockSpec((1, GROUP_W), lambda i: (0, 0))],
        out_specs=pl.BlockSpec((ROW_TILE, GROUP_W), lambda i: (i, 0)),
        out_shape=jax.ShapeDtypeStruct((n, GROUP_W), jnp.bfloat16),
        scratch_shapes=[pltpu.VMEM((ROW_TILE + 2 * POOL_HALO, GROUP_W), jnp.float32)],
        compiler_params=_cparams("arbitrary"),
        name="pool_mix",
    )(p_main, p_main, p_main, w_pool, pool_scale.reshape(1, GROUP_W))


def chunk_masks(chunk, reverse):
    i = jnp.arange(ROW_TILE)[:, None]
    j = jnp.arange(ROW_TILE)[None, :]
    same = (i // chunk) == (j // chunk)
    return (same & ((j >= i) if reverse else (j <= i))).astype(jnp.bfloat16)


def _split3(x):
    hi = x.astype(jnp.bfloat16)
    r = x - hi.astype(jnp.float32)
    mid = r.astype(jnp.bfloat16)
    lo = (r - mid.astype(jnp.float32)).astype(jnp.bfloat16)
    return hi, mid, lo


def _dot3(m, x):
    hi, mid, lo = _split3(x)
    d = functools.partial(jnp.dot, preferred_element_type=jnp.float32)
    return d(m, hi) + (d(m, mid) + d(m, lo))


def _scan_block(n_blk, reverse):
    if reverse:
        return lambda j: jnp.where(j == 0, 0, n_blk - j)
    return lambda j: j


def _hgrn_head(q_ref, v_ref, z_ref, lb_ref, mask_ref, st, cols, reverse):
    bf16 = jnp.bfloat16
    f32 = jnp.float32
    T, C = ROW_TILE, A_CHUNK
    nt = (((1,), (1,)), ((), ()))
    tn = (((0,), (0,)), ((), ()))
    z = z_ref[:, cols]
    log_lb = lb_ref[0:1, cols]
    log1m_lb = lb_ref[1:2, cols]
    one_m_lb = lb_ref[2:3, cols]
    log_sig = jnp.minimum(z, 0.0) - jnp.log1p(jnp.exp(-jnp.abs(z)))
    bb = log1m_lb + log_sig
    mx = jnp.maximum(log_lb, bb)
    lf = mx + jnp.log1p(jnp.exp(-jnp.abs(log_lb - bb)))
    k = one_m_lb / (1.0 + jnp.exp(z))
    aq = q_ref[:, cols]
    q = aq / (1.0 + jnp.exp(-aq))
    v = v_ref[:, cols].astype(bf16)

    b = _dot3(mask_ref[...], lf)
    last = 0 if reverse else C - 1
    b_end = jnp.concatenate([jnp.broadcast_to(b[c * C + last:c * C + last + 1, :], (C, HEAD_DIM))
                             for c in range(T // C)], axis=0)
    q_dec = (q * jnp.exp(b)).astype(bf16)
    k_inv = (k * jnp.exp(-b)).astype(bf16)
    k_dec = (k * jnp.exp(b_end - b)).astype(bf16)
    dec = jnp.exp(b_end)

    scores = lax.dot_general(q_dec, k_inv, nt, preferred_element_type=f32)
    scores = jnp.where(mask_ref[...] > 0, scores, 0.0)
    o_intra = jnp.dot(scores.astype(bf16), v, preferred_element_type=f32)

    n_c = T // C
    rows = [slice(c * C, (c + 1) * C) for c in range(n_c)]
    deltas = [lax.dot_general(v[r], k_dec[r], tn, preferred_element_type=f32) for r in rows]
    entering = [None] * n_c
    for c in (range(n_c - 1, -1, -1) if reverse else range(n_c)):
        entering[c] = st
        st = st * dec[c * C:c * C + 1, :] + deltas[c]
    outs = [lax.dot_general(q_dec[r], entering[c].astype(bf16), nt, preferred_element_type=f32)
            for c, r in enumerate(rows)]
    return o_intra + jnp.concatenate(outs, axis=0), st


def _hgrn_kernel(q_ref, v_ref, z_ref, lb_ref, mask_ref, *rest, reverse, hp):
    if reverse:
        g_ref, of_ref, gn_ref, o_ref, st_ref = rest
    else:
        o_ref, st_ref = rest

    @pl.when(pl.program_id(1) == 0)
    def _():
        st_ref[...] = jnp.zeros_like(st_ref)

    for hh in range(hp):
        cols = slice(hh * HEAD_DIM, (hh + 1) * HEAD_DIM)
        o, st = _hgrn_head(q_ref, v_ref, z_ref, lb_ref, mask_ref, st_ref[hh], cols, reverse)
        st_ref[hh] = st
        if reverse:
            o = o + of_ref[:, cols]
            o = o * lax.rsqrt(jnp.mean(o * o, axis=-1, keepdims=True) + EPS) * gn_ref[:, cols]
            ag = g_ref[:, cols]
            o_ref[:, cols] = (o * (ag / (1.0 + jnp.exp(-ag)))).astype(o_ref.dtype)
        else:
            o_ref[:, cols] = o


def hgrn_scan(p_main, lb_rows, mask, reverse, o_fwd=None, gn=None, hp=SCAN_HP):
    n = p_main.shape[0]
    n_blk = n // ROW_TILE
    tb = _scan_block(n_blk, reverse)
    w = hp * HEAD_DIM
    gpc = GROUP_W // w
    blk = lambda c: pl.BlockSpec((ROW_TILE, w), lambda h, j: (tb(j), c * gpc + h))
    in_specs = [blk(COL_AQ), blk(COL_AI), blk(COL_AFB if reverse else COL_AFF),
                pl.BlockSpec((3, w), lambda h, j: (0, h)),
                pl.BlockSpec((ROW_TILE, ROW_TILE), lambda h, j: (0, 0))]
    args = [p_main, p_main, p_main, lb_rows, mask]
    if reverse:
        in_specs += [blk(COL_AG), pl.BlockSpec((ROW_TILE, w), lambda h, j: (tb(j), h)),
                     pl.BlockSpec((1, w), lambda h, j: (0, h))]
        args += [p_main, o_fwd, gn.reshape(1, GROUP_W)]
    return pl.pallas_call(
        functools.partial(_hgrn_kernel, reverse=reverse, hp=hp),
        grid=(gpc, n_blk),
        in_specs=in_specs,
        out_specs=pl.BlockSpec((ROW_TILE, w), lambda h, j: (tb(j), h)),
        out_shape=jax.ShapeDtypeStruct((n, GROUP_W), jnp.bfloat16 if reverse else jnp.float32),
        scratch_shapes=[pltpu.VMEM((hp, HEAD_DIM, HEAD_DIM), jnp.float32)],
        compiler_params=_cparams("arbitrary", "arbitrary"),
        name="hgrn_bwd" if reverse else "hgrn_fwd",
    )(*args)


def hgrn_mix(p_main, lb_f, lb_b, gn):
    rows = lambda lb: jnp.stack([jnp.log(lb), jnp.log1p(-lb), 1.0 - lb])
    o_f = hgrn_scan(p_main, rows(lb_f), chunk_masks(A_CHUNK, False), False)
    return hgrn_scan(p_main, rows(lb_b), chunk_masks(A_CHUNK, True), True, o_f, gn)


def hgrn_lower_bounds(p):
    cs = jnp.cumsum(jax.nn.softmax(p.astype(jnp.float32), axis=1), axis=1)
    return cs - cs[:, :1]


def rope_tables(n_tok):
    quarter = HEAD_DIM // 4
    pos = jnp.arange(n_tok - CTX_LEN)
    inv_freq = ROPE_BASE ** (-jnp.arange(quarter, dtype=jnp.float32) / quarter)
    lane = jnp.arange(HEAD_DIM)
    p = jnp.where(lane[None, :] < HEAD_DIM // 2, (pos // GRID_W)[:, None], (pos % GRID_W)[:, None]).astype(jnp.float32)
    ang = p * inv_freq[lane % quarter][None, :]
    sign = jnp.where((lane % (2 * quarter)) < quarter, -1.0, 1.0)
    cos = jnp.concatenate([jnp.ones((CTX_LEN, HEAD_DIM), jnp.float32), jnp.cos(ang)], axis=0)
    sin = jnp.concatenate([jnp.zeros((CTX_LEN, HEAD_DIM), jnp.float32), jnp.sin(ang) * sign], axis=0)
    return cos, sin


def _rope(x, cos, sin):
    quarter = HEAD_DIM // 4
    lane = lax.broadcasted_iota(jnp.int32, x.shape, 1)
    partner = jnp.where((lane % (2 * quarter)) < quarter,
                        pltpu.roll(x, HEAD_DIM - quarter, axis=1), pltpu.roll(x, quarter, axis=1))
    return x * cos + partner * sin


def _mlstm_head(q_ref, k_ref, v_ref, cos, sin, cols, b_bc, li_bc, b_row, li_row, state, reverse):
    bf16 = jnp.bfloat16
    f32 = jnp.float32
    T, C = ROW_TILE, D_CHUNK
    n_c = T // C
    nt = (((1,), (1,)), ((), ()))
    tn = (((0,), (0,)), ((), ()))
    d = functools.partial(jnp.dot, preferred_element_type=f32)
    q = _rope(q_ref[:, cols], cos, sin) * HEAD_DIM ** -0.5
    k = _rope(k_ref[:, cols], cos, sin)
    qb = q.astype(bf16)
    kb = k.astype(bf16)
    vb = v_ref[:, cols].astype(bf16)
    ii = lax.broadcasted_iota(jnp.int32, (C, C), 0)
    jj = lax.broadcasted_iota(jnp.int32, (C, C), 1)
    causal = (jj >= ii) if reverse else (jj <= ii)
    last = 0 if reverse else C - 1
    rows = [slice(c * C, (c + 1) * C) for c in range(n_c)]
    bends, m_locs, d_ss, d_ns = [], [], [], []
    for c, r in enumerate(rows):
        bend = b_bc[c * C + last:c * C + last + 1]
        gcol = bend - b_bc[r] + li_bc[r]
        m_loc = jnp.max(gcol, axis=0, keepdims=True)
        kw = k[r] * jnp.exp(gcol - m_loc)
        bends.append(bend)
        m_locs.append(m_loc)
        d_ss.append(lax.dot_general(kw.astype(bf16), vb[r], tn, preferred_element_type=f32))
        d_ns.append(jnp.sum(kw, axis=0, keepdims=True))
    s, nv, m = state
    entering = [None] * n_c
    for c in (range(n_c - 1, -1, -1) if reverse else range(n_c)):
        entering[c] = (s, nv, m)
        m_new = jnp.maximum(bends[c] + m, m_locs[c])
        a = jnp.exp(bends[c] + m - m_new)
        cc = jnp.exp(m_locs[c] - m_new)
        s = a * s + cc * d_ss[c]
        nv = a * nv + cc * d_ns[c]
        m = m_new
    outs = []
    for c, r in enumerate(rows):
        s_in, n_in, m_in = entering[c]
        dmat = jnp.where(causal, b_bc[r][:, 0:C] - b_row[:, r] + li_row[:, r], -jnp.inf)
        m_inter = b_bc[r] + m_in
        m_t = jnp.maximum(m_inter, jnp.max(dmat, axis=-1, keepdims=True))
        w = jnp.exp(dmat - m_t[:, 0:C]) * lax.dot_general(qb[r], kb[r], nt, preferred_element_type=f32)
        a_col = jnp.exp(m_inter - m_t)
        num = d(w.astype(bf16), vb[r]) + a_col * d(qb[r], s_in.astype(bf16))
        den = jnp.sum(w, axis=-1, keepdims=True) + a_col * jnp.sum(q[r] * n_in, axis=-1, keepdims=True)
        outs.append(num / jnp.maximum(jnp.abs(den), jnp.exp(-m_t)))
    return jnp.concatenate(outs, axis=0), (s, nv, m)


def _mlstm_kernel(q_ref, k_ref, v_ref, g_ref, cos_ref, sin_ref, mask_ref, maskt_ref, *rest, reverse, hp):
    if reverse:
        do_ref, of_ref, gn_ref, o_ref, s_ref, n_ref, m_ref, cumt_ref, lit_ref = rest
    else:
        o_ref, s_ref, n_ref, m_ref, cumt_ref, lit_ref = rest
    hg = pl.program_id(0)
    f32 = jnp.float32
    bf16 = jnp.bfloat16

    @pl.when(pl.program_id(1) == 0)
    def _():
        s_ref[...] = jnp.zeros_like(s_ref)
        n_ref[...] = jnp.zeros_like(n_ref)
        m_ref[...] = jnp.zeros_like(m_ref)

    g = g_ref[...]
    lsg = jnp.minimum(g, 0.0) - jnp.log1p(jnp.exp(-jnp.abs(g)))
    cum = _dot3(mask_ref[...], lsg)
    hi, mid, lo = _split3(lsg.T)
    d = functools.partial(jnp.dot, preferred_element_type=f32)
    cumt_ref[...] = d(hi, maskt_ref[...]) + (d(mid, maskt_ref[...]) + d(lo, maskt_ref[...]))
    lit_ref[...] = g.T
    cum_parts = _split3(cum)
    g_parts = _split3(g)
    sel_row = lax.broadcasted_iota(jnp.int32, (LANES, LANES), 0)

    def replicate(parts, col):
        sel = (sel_row == col).astype(bf16)
        return d(parts[0], sel) + (d(parts[1], sel) + d(parts[2], sel))

    cos = cos_ref[...]
    sin = sin_ref[...]
    for hh in range(hp):
        h = hg * hp + hh
        col_i = (GATE_I_B if reverse else GATE_I_F) * N_HEADS + h
        col_f = (GATE_F_B if reverse else GATE_F_F) * N_HEADS + h
        b_bc = replicate(cum_parts, col_f)
        li_bc = replicate(g_parts, col_i)
        b_row = cumt_ref[pl.ds(col_f, 1), :]
        li_row = lit_ref[pl.ds(col_i, 1), :]
        cols = slice(hh * HEAD_DIM, (hh + 1) * HEAD_DIM)
        state = (s_ref[hh], n_ref[hh, 0:1, :], m_ref[hh, 0:1, :])
        o, (s, nv, m) = _mlstm_head(q_ref, k_ref, v_ref, cos, sin, cols, b_bc, li_bc, b_row, li_row, state, reverse)
        s_ref[hh] = s
        n_ref[hh] = jnp.broadcast_to(nv, n_ref.shape[1:])
        m_ref[hh] = jnp.broadcast_to(m, m_ref.shape[1:])
        if reverse:
            o = o + of_ref[:, cols]
            o = o * lax.rsqrt(jnp.mean(o * o, axis=-1, keepdims=True) + EPS) * gn_ref[:, cols]
            o_ref[:, cols] = (o / (1.0 + jnp.exp(-do_ref[:, cols]))).astype(o_ref.dtype)
        else:
            o_ref[:, cols] = o


def mlstm_scan(p_main, p_gate, cos, sin, mask, reverse, o_fwd=None, gn=None, hp=SCAN_HP):
    n = p_main.shape[0]
    n_blk = n // ROW_TILE
    tb = _scan_block(n_blk, reverse)
    w = hp * HEAD_DIM
    gpc = GROUP_W // w
    blk = lambda c: pl.BlockSpec((ROW_TILE, w), lambda h, j: (tb(j), c * gpc + h))
    tok = pl.BlockSpec((ROW_TILE, LANES), lambda h, j: (tb(j), 0))
    sq = pl.BlockSpec((ROW_TILE, ROW_TILE), lambda h, j: (0, 0))
    in_specs = [blk(COL_DQ), blk(COL_DK), blk(COL_DV), tok, tok, tok, sq, sq]
    args = [p_main, p_main, p_main, p_gate, cos, sin, mask, mask.T]
    if reverse:
        in_specs += [blk(COL_DO), pl.BlockSpec((ROW_TILE, w), lambda h, j: (tb(j), h)),
                     pl.BlockSpec((1, w), lambda h, j: (0, h))]
        args += [p_main, o_fwd, gn.reshape(1, GROUP_W)]
    return pl.pallas_call(
        functools.partial(_mlstm_kernel, reverse=reverse, hp=hp),
        grid=(gpc, n_blk),
        in_specs=in_specs,
        out_specs=pl.BlockSpec((ROW_TILE, w), lambda h, j: (tb(j), h)),
        out_shape=jax.ShapeDtypeStruct((n, GROUP_W), jnp.bfloat16 if reverse else jnp.float32),
        scratch_shapes=[pltpu.VMEM((hp, HEAD_DIM, HEAD_DIM), jnp.float32), pltpu.VMEM((hp, 8, HEAD_DIM), jnp.float32),
                        pltpu.VMEM((hp, 8, LANES), jnp.float32), pltpu.VMEM((LANES, ROW_TILE), jnp.float32),
                        pltpu.VMEM((LANES, ROW_TILE), jnp.float32)],
        compiler_params=_cparams("arbitrary", "arbitrary"),
        name="mlstm_bwd" if reverse else "mlstm_fwd",
    )(*args)


def mlstm_mix(p_main, p_gate, gn):
    cos, sin = rope_tables(p_main.shape[0])
    o_f = mlstm_scan(p_main, p_gate, cos, sin, chunk_masks(D_CHUNK, False), False)
    return mlstm_scan(p_main, p_gate, cos, sin, chunk_masks(D_CHUNK, True), True, o_f, gn)


def kernel(x, c, ctx, c_ctx, ada_w, ada_b, norm1_w, norm2_w, mix_w_in, mix_b_in, hgrn_lb, hgrn_gn, natten_rpb,
           pool_w, pool_scale, mlstm_gn, mix_w_out, moe_wg, moe_bg, moe_we, moe_be, moe_w1, moe_w3, moe_w2, final_w):
    assert x.shape == (1, SEQ, D_MODEL) and ctx.shape == (1, CTX_LEN, D_MODEL)
    bf16 = jnp.bfloat16
    lbs = hgrn_lower_bounds(hgrn_lb)

    cond = jnp.zeros((16, D_MODEL), jnp.float32).at[0].set(jax.nn.silu(c_ctx)).at[1].set(jax.nn.silu(c[0]))
    mods = ada_modulation(cond.astype(bf16), ada_w, ada_b)[:, :2].reshape(DEPTH, 2, 6, D_MODEL)

    w_in = mix_w_in.astype(bf16)
    w_out = mix_w_out.astype(bf16)
    w1, w3, w2 = moe_w1.astype(bf16), moe_w3.astype(bf16), moe_w2.astype(bf16)

    xs = jnp.concatenate([ctx[0], x[0]], axis=0)
    for l in range(DEPTH):
        mod = mods[l]
        h = norm_modulate(xs, norm1_w[l], mod, 0)
        p_main = matmul_bias(h, w_in, mix_b_in[l], MM_TM, MM_TN, n=N_MAIN, layer=l)
        w_gate = jnp.pad(w_in[l, :, N_MAIN:], ((0, 0), (0, LANES - N_GATE)))
        b_gate = jnp.pad(mix_b_in[l, N_MAIN:], (0, LANES - N_GATE))
        p_gate = matmul_bias(h, w_gate, b_gate, MM_TM, LANES)

        out_a = hgrn_mix(p_main, lbs[0, l], lbs[1, l], hgrn_gn[l])
        out_b = natten(p_main, natten_bias_tiles(natten_rpb[l]))
        out_c = pool_mix(p_main, pool_w[l].astype(bf16), pool_scale[l])
        out_d = mlstm_mix(p_main, p_gate, mlstm_gn[l])
        xs = matmul_gated_residual((out_a, out_b, out_c, out_d), w_out, l, xs, mod, 2)

        w_router = jnp.pad(jnp.concatenate([moe_wg[l], moe_we[l]], axis=1),
                           ((0, 0), (0, LANES - N_GROUPS - N_EXPERTS)))
        h2, logits = norm_modulate_router(xs, norm2_w[l], mod, w_router)
        eid, top_w = moe_route(logits, moe_bg[l], moe_be[l])
        tile_expert, tile_valid, row_token, cpos = moe_schedule(eid)
        ys = moe_grouped(h2, tile_expert, tile_valid, row_token, w1, w3, w2, l)
        xs = moe_combine(cpos, ys, xs, top_w, mod, 5)

    return norm_modulate(xs, final_w, mods[0], None, out_dtype=jnp.float32, skip_rows=CTX_LEN)[None]
```

```python
import functools

import jax
import jax.numpy as jnp
from jax import lax
from jax.experimental import pallas as pl
from jax.experimental.pallas import tpu as pltpu

D_MODEL = 4096
SEQ = 8192
DEPTH = 2
GRID_W = 64
CTX_LEN = 256
N_TOK = CTX_LEN + SEQ
GROUP_W = 1024
HEAD_DIM = 128
N_HEADS = GROUP_W // HEAD_DIM
C_GROUPS = 4
C_GW = GROUP_W // C_GROUPS
POOL_WINDOWS = (2, 4, 8, 16)
NA_MAX_ROWS = 8
NA_COLS = 16
A_CHUNK = 32
D_CHUNK = 64
ROPE_BASE = 10000.0
N_GROUPS = 4
EXP_PER_GROUP = 8
N_EXPERTS = N_GROUPS * EXP_PER_GROUP
TOP_K = 2
D_FF = D_MODEL // 8
EPS = 1e-6
N_MAIN = 13 * GROUP_W
N_GATE = 4 * N_HEADS
LANES = 128
VMEM_LIMIT = 56 * 1024 * 1024

ROW_TILE = 256
MM_TM = 768
MM_TN = 512
MOE_TM = 256
MOE_TILES = (TOP_K * N_TOK + N_EXPERTS * (MOE_TM - 1)) // MOE_TM + 1
CMB_TB = 128
SCAN_HP = 8
DMA_UNROLL = 8
CAST_SLABS = 8

COL_AQ, COL_AI, COL_AG, COL_AFF, COL_AFB = 0, 1, 2, 3, 4
COL_Q, COL_K, COL_V, COL_C = 5, 6, 7, 8
COL_DQ, COL_DK, COL_DV, COL_DO = 9, 10, 11, 12
GATE_I_F, GATE_F_F, GATE_I_B, GATE_F_B = 0, 1, 2, 3
HPG = GROUP_W // HEAD_DIM


def _cparams(*sem):
    return pltpu.CompilerParams(dimension_semantics=sem, vmem_limit_bytes=VMEM_LIMIT)


def _ada_kernel(c_ref, w_ref, b_ref, o_ref):
    acc = jnp.dot(c_ref[...], w_ref[...].astype(jnp.bfloat16), preferred_element_type=jnp.float32)
    o_ref[...] = acc + b_ref[...]


def ada_modulation(cond, ada_w, ada_b):
    tn = 512
    n = ada_w.shape[-1]
    rows = cond.shape[0]
    return pl.pallas_call(
        _ada_kernel,
        grid=(DEPTH, n // tn),
        in_specs=[pl.BlockSpec((rows, D_MODEL), lambda l, j: (0, 0)),
                  pl.BlockSpec((None, D_MODEL, tn), lambda l, j: (l, 0, j)),
                  pl.BlockSpec((None, 1, tn), lambda l, j: (l, 0, j))],
        out_specs=pl.BlockSpec((None, rows, tn), lambda l, j: (l, 0, j)),
        out_shape=jax.ShapeDtypeStruct((DEPTH, rows, n), jnp.float32),
        compiler_params=_cparams("arbitrary", "arbitrary"),
        name="ada_modulation",
    )(cond, ada_w, ada_b.reshape(DEPTH, 1, n))


def _norm_kernel(x_ref, w_ref, mod_ref, o_ref, *, shift_row, scale_row):
    x = x_ref[...]
    y = x * lax.rsqrt(jnp.mean(x * x, axis=-1, keepdims=True) + EPS) * w_ref[...]
    if shift_row is not None:
        y = y * (1.0 + mod_ref[scale_row:scale_row + 1, :]) + mod_ref[shift_row:shift_row + 1, :]
    o_ref[...] = y.astype(o_ref.dtype)


def _norm_router_kernel(x_ref, w_ref, mod_ref, wr_ref, o_ref, lg_ref, *, shift_row, scale_row):
    x = x_ref[...]
    y = x * lax.rsqrt(jnp.mean(x * x, axis=-1, keepdims=True) + EPS) * w_ref[...]
    y = y * (1.0 + mod_ref[scale_row:scale_row + 1, :]) + mod_ref[shift_row:shift_row + 1, :]
    o_ref[...] = y.astype(o_ref.dtype)
    y_hi = y.astype(jnp.bfloat16)
    y_lo = (y - y_hi.astype(jnp.float32)).astype(jnp.bfloat16)
    w = wr_ref[...]
    w_hi = w.astype(jnp.bfloat16)
    w_lo = (w - w_hi.astype(jnp.float32)).astype(jnp.bfloat16)
    dot = functools.partial(jnp.dot, preferred_element_type=jnp.float32)
    lg_ref[...] = dot(y_hi, w_hi) + (dot(y_lo, w_hi) + dot(y_hi, w_lo))


def _tok_type(i):
    return jnp.where(i * ROW_TILE >= CTX_LEN, 1, 0)


def norm_modulate(x, w, mod, which, out_dtype=jnp.bfloat16, skip_rows=0):
    n = x.shape[0] - skip_rows
    off = skip_rows // ROW_TILE
    rows = (None, None) if which is None else (3 * which, 3 * which + 1)
    return pl.pallas_call(
        functools.partial(_norm_kernel, shift_row=rows[0], scale_row=rows[1]),
        grid=(n // ROW_TILE,),
        in_specs=[pl.BlockSpec((ROW_TILE, D_MODEL), lambda i: (i + off, 0)),
                  pl.BlockSpec((1, D_MODEL), lambda i: (0, 0)),
                  pl.BlockSpec((None, 6, D_MODEL), lambda i: (_tok_type(i + off), 0, 0))],
        out_specs=pl.BlockSpec((ROW_TILE, D_MODEL), lambda i: (i, 0)),
        out_shape=jax.ShapeDtypeStruct((n, D_MODEL), out_dtype),
        compiler_params=_cparams("arbitrary"),
        name="norm_modulate",
    )(x, w.reshape(1, D_MODEL), mod)


def norm_modulate_router(x, w, mod, w_router):
    n = x.shape[0]
    return pl.pallas_call(
        functools.partial(_norm_router_kernel, shift_row=3, scale_row=4),
        grid=(n // ROW_TILE,),
        in_specs=[pl.BlockSpec((ROW_TILE, D_MODEL), lambda i: (i, 0)),
                  pl.BlockSpec((1, D_MODEL), lambda i: (0, 0)),
                  pl.BlockSpec((None, 6, D_MODEL), lambda i: (_tok_type(i), 0, 0)),
                  pl.BlockSpec((D_MODEL, LANES), lambda i: (0, 0))],
        out_specs=[pl.BlockSpec((ROW_TILE, D_MODEL), lambda i: (i, 0)),
                   pl.BlockSpec((ROW_TILE, LANES), lambda i: (i, 0))],
        out_shape=[jax.ShapeDtypeStruct((n, D_MODEL), jnp.float32),
                   jax.ShapeDtypeStruct((n, LANES), jnp.float32)],
        compiler_params=_cparams("arbitrary"),
        name="norm_modulate_router",
    )(x, w.reshape(1, D_MODEL), mod, w_router)


def _mm_bias_kernel(a_ref, b_ref, bias_ref, o_ref):
    o_ref[...] = jnp.dot(a_ref[...], b_ref[...], preferred_element_type=jnp.float32) + bias_ref[...]


def _mm_resid_kernel(a0_ref, a1_ref, a2_ref, a3_ref, b_ref, x_ref, mod_ref, o_ref, *, gate_row):
    acc = None
    for g, a_ref in enumerate((a0_ref, a1_ref, a2_ref, a3_ref)):
        part = jnp.dot(a_ref[...], b_ref[g * GROUP_W:(g + 1) * GROUP_W, :], preferred_element_type=jnp.float32)
        acc = part if acc is None else acc + part
    tm = x_ref.shape[0]
    row = pl.program_id(0) * tm + lax.broadcasted_iota(jnp.int32, (tm, 1), 0)
    gate = jnp.where(row < CTX_LEN, mod_ref[0, gate_row:gate_row + 1, :], mod_ref[1, gate_row:gate_row + 1, :])
    o_ref[...] = x_ref[...] + gate * acc


def matmul_bias(a, b, bias, tm, tn):
    m, k = a.shape
    n = b.shape[1]
    return pl.pallas_call(
        _mm_bias_kernel,
        grid=(m // tm, n // tn),
        in_specs=[pl.BlockSpec((tm, k), lambda i, j: (i, 0)),
                  pl.BlockSpec((k, tn), lambda i, j: (0, j)),
                  pl.BlockSpec((1, tn), lambda i, j: (0, j))],
        out_specs=pl.BlockSpec((tm, tn), lambda i, j: (i, j)),
        out_shape=jax.ShapeDtypeStruct((m, n), jnp.float32),
        compiler_params=_cparams("arbitrary", "arbitrary"),
        name="matmul_bias",
    )(a, b, bias.reshape(1, -1))


def matmul_gated_residual(parts, b, layer, x, mod, gate_row):
    m, n = x.shape
    tm, tn = MM_TM, MM_TN
    a_spec = pl.BlockSpec((tm, GROUP_W), lambda i, j: (i, 0))
    return pl.pallas_call(
        functools.partial(_mm_resid_kernel, gate_row=gate_row),
        grid=(m // tm, n // tn),
        in_specs=[a_spec, a_spec, a_spec, a_spec,
                  pl.BlockSpec((None, b.shape[1], tn), lambda i, j: (layer, 0, j)),
                  pl.BlockSpec((tm, tn), lambda i, j: (i, j)),
                  pl.BlockSpec((2, 6, tn), lambda i, j: (0, 0, j))],
        out_specs=pl.BlockSpec((tm, tn), lambda i, j: (i, j)),
        out_shape=jax.ShapeDtypeStruct((m, n), jnp.float32),
        compiler_params=_cparams("arbitrary", "arbitrary"),
        name="matmul_gated_residual",
    )(*parts, b, x, mod)


def _cast_kernel(x_ref, o_ref):
    o_ref[...] = x_ref[...].astype(o_ref.dtype)


def cast_bf16(w, rows):
    nl, r, c = w.shape
    spec = pl.BlockSpec((None, rows, c), lambda l, i: (l, i, 0))
    return pl.pallas_call(
        _cast_kernel,
        grid=(nl, r // rows),
        in_specs=[spec],
        out_specs=spec,
        out_shape=jax.ShapeDtypeStruct(w.shape, jnp.bfloat16),
        compiler_params=_cparams("arbitrary", "arbitrary"),
        name="cast_bf16",
    )(w)


def _in_proj_cast_kernel(a_ref, b_ref, bias_ref, w1_ref, w3_ref, w2_ref, o_ref, w1o_ref, w3o_ref, w2o_ref):
    o_ref[...] = jnp.dot(a_ref[...], b_ref[...], preferred_element_type=jnp.float32) + bias_ref[...]
    w1o_ref[...] = w1_ref[...].astype(w1o_ref.dtype)
    w3o_ref[...] = w3_ref[...].astype(w3o_ref.dtype)
    w2o_ref[...] = w2_ref[...].astype(w2o_ref.dtype)


def in_proj_cast(a, b, bias, layer, w1, w3, w2):
    m, k = a.shape
    n, tm, tn = N_MAIN, MM_TM, MM_TN
    n_j = n // tn
    n_slabs = N_EXPERTS * CAST_SLABS
    assert (m // tm) * n_j >= n_slabs
    r13, r2 = D_MODEL // CAST_SLABS, D_FF // CAST_SLABS

    def slab(i, j):
        s = jnp.minimum(i * n_j + j, n_slabs - 1)
        return s // CAST_SLABS, s % CAST_SLABS

    in13 = pl.BlockSpec((None, None, r13, D_FF), lambda i, j: (layer, *slab(i, j), 0))
    in2 = pl.BlockSpec((None, None, r2, D_MODEL), lambda i, j: (layer, *slab(i, j), 0))
    out13 = pl.BlockSpec((None, r13, D_FF), lambda i, j: (*slab(i, j), 0))
    out2 = pl.BlockSpec((None, r2, D_MODEL), lambda i, j: (*slab(i, j), 0))
    bf16 = jnp.bfloat16
    return pl.pallas_call(
        _in_proj_cast_kernel,
        grid=(m // tm, n_j),
        in_specs=[pl.BlockSpec((tm, k), lambda i, j: (i, 0)),
                  pl.BlockSpec((None, k, tn), lambda i, j: (layer, 0, j)),
                  pl.BlockSpec((1, tn), lambda i, j: (0, j)),
                  in13, in13, in2],
        out_specs=[pl.BlockSpec((tm, tn), lambda i, j: (i, j)), out13, out13, out2],
        out_shape=[jax.ShapeDtypeStruct((m, n), jnp.float32),
                   jax.ShapeDtypeStruct((N_EXPERTS, D_MODEL, D_FF), bf16),
                   jax.ShapeDtypeStruct((N_EXPERTS, D_MODEL, D_FF), bf16),
                   jax.ShapeDtypeStruct((N_EXPERTS, D_FF, D_MODEL), bf16)],
        compiler_params=_cparams("arbitrary", "arbitrary"),
        name="in_proj_cast",
    )(a, b, bias.reshape(1, -1), w1, w3, w2)


def _moe_kernel(te_ref, tv_ref, rt_ref, h_hbm, w1_ref, w3_ref, w2_ref, y_ref, xg_ref, sem_ref):
    del te_ref
    i = pl.program_id(0)
    n_tiles = pl.num_programs(0)

    def row_copy(tile, slot, r):
        tok = rt_ref[tile * MOE_TM + r]
        return pltpu.make_async_copy(h_hbm.at[pl.ds(tok, 1), :], xg_ref.at[slot, pl.ds(r, 1), :], sem_ref.at[slot])

    def start_gather(tile, slot):
        @pl.when(tv_ref[tile] != 0)
        def _():
            def body(r, carry):
                row_copy(tile, slot, r).start()
                return carry
            lax.fori_loop(0, MOE_TM, body, 0, unroll=DMA_UNROLL)

    @pl.when(i == 0)
    def _():
        start_gather(0, 0)

    @pl.when(i + 1 < n_tiles)
    def _():
        start_gather(i + 1, (i + 1) % 2)

    @pl.when(tv_ref[i] != 0)
    def _():
        slot = i % 2

        def wait_body(r, carry):
            row_copy(i, slot, r).wait()
            return carry
        lax.fori_loop(0, MOE_TM, wait_body, 0, unroll=DMA_UNROLL)
        xg = xg_ref[slot].astype(jnp.bfloat16)
        a = jnp.dot(xg, w1_ref[...], preferred_element_type=jnp.float32)
        b = jnp.dot(xg, w3_ref[...], preferred_element_type=jnp.float32)
        act = (a * jax.nn.sigmoid(a)) * b
        y_ref[...] = jnp.dot(act.astype(jnp.bfloat16), w2_ref[...], preferred_element_type=jnp.float32)

    @pl.when(tv_ref[i] == 0)
    def _():
        y_ref[...] = jnp.zeros_like(y_ref)


def moe_grouped(h, tile_expert, tile_valid, row_token, w1, w3, w2):
    grid_spec = pltpu.PrefetchScalarGridSpec(
        num_scalar_prefetch=3,
        grid=(MOE_TILES,),
        in_specs=[pl.BlockSpec(memory_space=pl.ANY),
                  pl.BlockSpec((None, D_MODEL, D_FF), lambda i, te, tv, rt: (te[i], 0, 0)),
                  pl.BlockSpec((None, D_MODEL, D_FF), lambda i, te, tv, rt: (te[i], 0, 0)),
                  pl.BlockSpec((None, D_FF, D_MODEL), lambda i, te, tv, rt: (te[i], 0, 0))],
        out_specs=pl.BlockSpec((MOE_TM, D_MODEL), lambda i, te, tv, rt: (i, 0)),
        scratch_shapes=[pltpu.VMEM((2, MOE_TM, D_MODEL), jnp.float32), pltpu.SemaphoreType.DMA((2,))],
    )
    return pl.pallas_call(
        _moe_kernel,
        grid_spec=grid_spec,
        out_shape=jax.ShapeDtypeStruct((MOE_TILES * MOE_TM, D_MODEL), jnp.float32),
        compiler_params=_cparams("arbitrary"),
        name="moe_grouped",
    )(tile_expert, tile_valid, row_token, h, w1, w3, w2)


def _combine_kernel(pos_ref, y_hbm, x_ref, w_ref, mod_ref, o_ref, buf_ref, sem_ref, *, gate_row):
    i = pl.program_id(0)
    n_steps = pl.num_programs(0)

    def row_copy(step, slot, r):
        src = pos_ref[step * (TOP_K * CMB_TB) + r]
        return pltpu.make_async_copy(y_hbm.at[pl.ds(src, 1), :], buf_ref.at[slot, pl.ds(r, 1), :], sem_ref.at[slot])

    def start_gather(step, slot):
        def body(r, carry):
            row_copy(step, slot, r).start()
            return carry
        lax.fori_loop(0, TOP_K * CMB_TB, body, 0, unroll=DMA_UNROLL)

    @pl.when(i == 0)
    def _():
        start_gather(0, 0)

    @pl.when(i + 1 < n_steps)
    def _():
        start_gather(i + 1, (i + 1) % 2)

    slot = i % 2

    def wait_body(r, carry):
        row_copy(i, slot, r).wait()
        return carry
    lax.fori_loop(0, TOP_K * CMB_TB, wait_body, 0, unroll=DMA_UNROLL)
    w = w_ref[...]
    moe = w[:, 0:1] * buf_ref[slot, 0:CMB_TB, :] + w[:, 1:2] * buf_ref[slot, CMB_TB:2 * CMB_TB, :]
    o_ref[...] = x_ref[...] + mod_ref[gate_row:gate_row + 1, :] * moe


def moe_combine(pos, ys, x, top_w, mod, gate_row):
    n = x.shape[0]
    grid_spec = pltpu.PrefetchScalarGridSpec(
        num_scalar_prefetch=1,
        grid=(n // CMB_TB,),
        in_specs=[pl.BlockSpec(memory_space=pl.ANY),
                  pl.BlockSpec((CMB_TB, D_MODEL), lambda i, pos: (i, 0)),
                  pl.BlockSpec((CMB_TB, TOP_K), lambda i, pos: (i, 0)),
                  pl.BlockSpec((None, 6, D_MODEL), lambda i, pos: (jnp.where(i * CMB_TB >= CTX_LEN, 1, 0), 0, 0))],
        out_specs=pl.BlockSpec((CMB_TB, D_MODEL), lambda i, pos: (i, 0)),
        scratch_shapes=[pltpu.VMEM((2, TOP_K * CMB_TB, D_MODEL), jnp.float32), pltpu.SemaphoreType.DMA((2,))],
    )
    return pl.pallas_call(
        functools.partial(_combine_kernel, gate_row=gate_row),
        grid_spec=grid_spec,
        out_shape=jax.ShapeDtypeStruct((n, D_MODEL), jnp.float32),
        compiler_params=_cparams("arbitrary"),
        name="moe_combine",
    )(pos, ys, x, top_w, mod)


def moe_route(logits, bg, be):
    n = logits.shape[0]
    g_logits = logits[:, :N_GROUPS] + bg
    grp = jnp.argmax(g_logits, axis=-1)
    g_sel = grp[:, None] == jnp.arange(N_GROUPS)[None, :]
    p_grp = jnp.sum(jnp.where(g_sel, jax.nn.softmax(g_logits, axis=-1), 0.0), axis=-1, keepdims=True)
    e_logits = (logits[:, N_GROUPS:N_GROUPS + N_EXPERTS] + be).reshape(n, N_GROUPS, EXP_PER_GROUP)
    e_in = jnp.sum(jnp.where(g_sel[:, :, None], e_logits, 0.0), axis=1)
    lane = jnp.arange(EXP_PER_GROUP)[None, :]
    i1 = jnp.argmax(e_in, axis=-1)
    v1 = jnp.max(e_in, axis=-1)
    rest = jnp.where(lane == i1[:, None], -jnp.inf, e_in)
    i2 = jnp.argmax(rest, axis=-1)
    v2 = jnp.max(rest, axis=-1)
    top_w = jax.nn.softmax(jnp.stack([v1, v2], axis=-1), axis=-1) * p_grp
    eid = grp[:, None] * EXP_PER_GROUP + jnp.stack([i1, i2], axis=-1)
    return eid.astype(jnp.int32), top_w


def moe_schedule(eid):
    n = eid.shape[0]
    flat_e = eid.reshape(-1)
    onehot = (flat_e[:, None] == jnp.arange(N_EXPERTS)[None, :]).astype(jnp.int32)
    rank = jnp.sum(onehot * (jnp.cumsum(onehot, axis=0) - 1), axis=1)
    counts = jnp.sum(onehot, axis=0)
    padded = ((counts + MOE_TM - 1) // MOE_TM) * MOE_TM
    pad_end = jnp.cumsum(padded)
    pad_start = pad_end - padded
    dest = (jnp.sum(onehot * pad_start[None, :], axis=1) + rank).astype(jnp.int32)
    n_rows = MOE_TILES * MOE_TM
    row_token = jnp.zeros((n_rows,), jnp.int32).at[dest].set(jnp.arange(TOP_K * n, dtype=jnp.int32) // TOP_K)
    tile_row0 = jnp.arange(MOE_TILES, dtype=jnp.int32) * MOE_TM
    n_valid = pad_end[-1] // MOE_TM
    tile_valid = (tile_row0 < pad_end[-1]).astype(jnp.int32)
    tile_expert = jnp.sum((tile_row0[:, None] >= pad_end[None, :]).astype(jnp.int32), axis=1)
    last_expert = jnp.sum(jnp.where(jnp.arange(MOE_TILES) == n_valid - 1, tile_expert, 0))
    tile_expert = jnp.where(tile_valid != 0, tile_expert, last_expert).astype(jnp.int32)
    pos = dest.reshape(n // CMB_TB, CMB_TB, TOP_K).transpose(0, 2, 1).reshape(-1)
    return tile_expert, tile_valid, row_token, pos


NA_QR = 4
NA_KR = NA_QR + NA_MAX_ROWS - 1
NA_TQ = NA_QR * GRID_W
NA_TK = NA_KR * GRID_W
NA_HP = 8
NA_WIN_BLOCKS = 3


def natten_bias_tiles(rpb):
    hi = lax.Precision.HIGHEST
    cols = jnp.arange(GRID_W)
    dc = jnp.clip(cols[None, :] - cols[:, None] + (NA_COLS - 1), 0, 2 * NA_COLS - 2)
    c0 = jnp.clip(cols - NA_COLS // 2, 0, GRID_W - NA_COLS)
    in_win = (cols[None, :] >= c0[:, None]) & (cols[None, :] < c0[:, None] + NA_COLS)
    oh_c = (dc[..., None] == jnp.arange(2 * NA_COLS - 1)).astype(jnp.float32)
    toep = jnp.einsum('qkc,hdc->hdqk', oh_c, rpb.astype(jnp.float32), precision=hi)
    rl = jnp.arange(NA_QR)[:, None]
    kl = jnp.arange(NA_KR)[None, :]
    half = NA_MAX_ROWS // 2
    pats = [(kl - rl + NA_MAX_ROWS - 1, (kl < NA_MAX_ROWS) & (rl >= 0)),
            (kl - rl + NA_MAX_ROWS - 1 - half, (kl - rl >= 0) & (kl - rl < NA_MAX_ROWS)),
            (kl - rl, (kl >= NA_KR - NA_MAX_ROWS) & (rl >= 0))]
    tiles = []
    for dr, valid in pats:
        oh_r = ((dr[..., None] == jnp.arange(2 * NA_MAX_ROWS - 1)) & valid[..., None]).astype(jnp.float32)
        t = jnp.einsum('rkd,hdqc->hrqkc', oh_r, toep, precision=hi)
        ok = valid[None, :, None, :, None] & in_win[None, None, :, None, :]
        tiles.append(jnp.where(ok, t, -jnp.inf).reshape(rpb.shape[0], NA_TQ, NA_TK))
    return jnp.stack(tiles, axis=1)


def _natten_kernel(q_ref, kc_ref, vc_ref, k0_ref, k1_ref, k2_ref, v0_ref, v1_ref, v2_ref, bias_ref, o_ref,
                   kw_ref, vw_ref, *, n_rows, n_blk, hp):
    j = pl.program_id(1)
    bf16 = jnp.bfloat16
    f32 = jnp.float32
    nt = (((1,), (1,)), ((), ()))
    heads = [slice(hh * HEAD_DIM, (hh + 1) * HEAD_DIM) for hh in range(hp)]

    def ctx_scores(cols):
        q = (q_ref[:, cols] * HEAD_DIM ** -0.5).astype(bf16)
        kc = kc_ref[:, cols].astype(bf16)
        return q, lax.dot_general(q, kc, nt, preferred_element_type=f32)

    @pl.when(j == 0)
    def _():
        for cols in heads:
            _, s_ctx = ctx_scores(cols)
            m = jnp.max(s_ctx, axis=-1, keepdims=True)
            p = jnp.exp(s_ctx - m)
            l = jnp.sum(p, axis=-1, keepdims=True)
            o = jnp.dot(p.astype(bf16), vc_ref[:, cols].astype(bf16), preferred_element_type=f32)
            o_ref[:, cols] = (o / l).astype(o_ref.dtype)

    @pl.when(j > 0)
    def _():
        for t, (k_ref, v_ref) in enumerate(((k0_ref, v0_ref), (k1_ref, v1_ref), (k2_ref, v2_ref))):
            kw_ref[t * ROW_TILE:(t + 1) * ROW_TILE, :] = k_ref[...].astype(bf16)
            vw_ref[t * ROW_TILE:(t + 1) * ROW_TILE, :] = v_ref[...].astype(bf16)
        r0 = (j - 1) * NA_QR
        kbase = jnp.clip(r0 - NA_MAX_ROWS // 2, 0, n_rows - NA_KR)
        first_blk = jnp.clip(j - 1, 1, n_blk - NA_WIN_BLOCKS)
        off = pl.multiple_of(CTX_LEN + kbase * GRID_W - first_blk * ROW_TILE, GRID_W)
        for hh, cols in enumerate(heads):
            q, s_ctx = ctx_scores(cols)
            kl = kw_ref[pl.ds(off, NA_TK), cols]
            vl = vw_ref[pl.ds(off, NA_TK), cols]
            s_loc = lax.dot_general(q, kl, nt, preferred_element_type=f32) + bias_ref[hh]
            m = jnp.maximum(jnp.max(s_loc, axis=-1, keepdims=True), jnp.max(s_ctx, axis=-1, keepdims=True))
            p_loc = jnp.exp(s_loc - m)
            p_ctx = jnp.exp(s_ctx - m)
            l = jnp.sum(p_loc, axis=-1, keepdims=True) + jnp.sum(p_ctx, axis=-1, keepdims=True)
            o = (jnp.dot(p_loc.astype(bf16), vl, preferred_element_type=f32)
                 + jnp.dot(p_ctx.astype(bf16), vc_ref[:, cols].astype(bf16), preferred_element_type=f32))
            o_ref[:, cols] = (o / l).astype(o_ref.dtype)


def natten(p_main, bias_tiles, hp=NA_HP):
    n = p_main.shape[0]
    n_rows = (n - CTX_LEN) // GRID_W
    n_blocks = n_rows // NA_QR
    n_blk = n // ROW_TILE
    assert n_rows % NA_QR == 0 and n_rows >= NA_KR and CTX_LEN == NA_TQ == ROW_TILE
    assert NA_TK + GRID_W <= NA_WIN_BLOCKS * ROW_TILE and n_blk > NA_WIN_BLOCKS
    w = hp * HEAD_DIM
    gpc = GROUP_W // w

    def pat(j):
        return jnp.where(j <= 1, 0, jnp.where(j == n_blocks, 2, 1))

    def win(col, t):
        return pl.BlockSpec((ROW_TILE, w), lambda h, j: (jnp.clip(j - 1, 1, n_blk - NA_WIN_BLOCKS) + t, col * gpc + h))

    return pl.pallas_call(
        functools.partial(_natten_kernel, n_rows=n_rows, n_blk=n_blk, hp=hp),
        grid=(gpc, n_blocks + 1),
        in_specs=[pl.BlockSpec((NA_TQ, w), lambda h, j: (j, COL_Q * gpc + h)),
                  pl.BlockSpec((CTX_LEN, w), lambda h, j: (0, COL_K * gpc + h)),
                  pl.BlockSpec((CTX_LEN, w), lambda h, j: (0, COL_V * gpc + h)),
                  win(COL_K, 0), win(COL_K, 1), win(COL_K, 2), win(COL_V, 0), win(COL_V, 1), win(COL_V, 2),
                  pl.BlockSpec((hp, None, NA_TQ, NA_TK), lambda h, j: (h, pat(j), 0, 0))],
        out_specs=pl.BlockSpec((NA_TQ, w), lambda h, j: (j, h)),
        out_shape=jax.ShapeDtypeStruct((n, GROUP_W), jnp.bfloat16),
        scratch_shapes=[pltpu.VMEM((NA_WIN_BLOCKS * ROW_TILE, w), jnp.bfloat16),
                        pltpu.VMEM((NA_WIN_BLOCKS * ROW_TILE, w), jnp.bfloat16)],
        compiler_params=_cparams("arbitrary", "arbitrary"),
        name="natten",
    )(*([p_main] * 9), bias_tiles)


POOL_HALO = max(POOL_WINDOWS) // 2


def _pool_kernel(prev_ref, cur_ref, next_ref, w_ref, s_ref, o_ref, ext_ref, *, n_tok):
    i = pl.program_id(0)
    t0 = i * ROW_TILE
    seq_lo = jnp.where(t0 < CTX_LEN, 0, CTX_LEN)
    seq_hi = jnp.where(t0 < CTX_LEN, CTX_LEN, n_tok)
    ext_ref[0:POOL_HALO, :] = jnp.where(t0 - POOL_HALO >= seq_lo, prev_ref[...], 0.0)
    ext_ref[POOL_HALO:POOL_HALO + ROW_TILE, :] = cur_ref[...]
    ext_ref[POOL_HALO + ROW_TILE:, :] = jnp.where(t0 + ROW_TILE < seq_hi, next_ref[...], 0.0)
    t = t0 + lax.broadcasted_iota(jnp.int32, (ROW_TILE, 1), 0)
    for g, win in enumerate(POOL_WINDOWS):
        half = win // 2
        cs = slice(g * C_GW, (g + 1) * C_GW)
        acc = ext_ref[POOL_HALO - half:POOL_HALO - half + ROW_TILE, cs]
        for d in range(-half + 1, half):
            acc = acc + ext_ref[POOL_HALO + d:POOL_HALO + d + ROW_TILE, cs]
        cnt = (jnp.minimum(t + half, seq_hi) - jnp.maximum(t - half, seq_lo)).astype(jnp.float32)
        diff = acc / cnt - cur_ref[:, cs]
        y = jnp.dot(diff.astype(jnp.bfloat16), w_ref[g], preferred_element_type=jnp.float32)
        o_ref[:, cs] = (y * s_ref[:, cs]).astype(o_ref.dtype)


def pool_mix(p_main, w_pool, pool_scale):
    n = p_main.shape[0]
    hb = ROW_TILE // POOL_HALO
    n_hblk = n // POOL_HALO
    return pl.pallas_call(
        functools.partial(_pool_kernel, n_tok=n),
        grid=(n // ROW_TILE,),
        in_specs=[pl.BlockSpec((POOL_HALO, GROUP_W), lambda i: (jnp.maximum(i * hb - 1, 0), COL_C)),
                  pl.BlockSpec((ROW_TILE, GROUP_W), lambda i: (i, COL_C)),
                  pl.BlockSpec((POOL_HALO, GROUP_W), lambda i: (jnp.minimum((i + 1) * hb, n_hblk - 1), COL_C)),
                  pl.BlockSpec((C_GROUPS, C_GW, C_GW), lambda i: (0, 0, 0)),
                  pl.BlockSpec((1, GROUP_W), lambda i: (0, 0))],
        out_specs=pl.BlockSpec((ROW_TILE, GROUP_W), lambda i: (i, 0)),
        out_shape=jax.ShapeDtypeStruct((n, GROUP_W), jnp.bfloat16),
        scratch_shapes=[pltpu.VMEM((ROW_TILE + 2 * POOL_HALO, GROUP_W), jnp.float32)],
        compiler_params=_cparams("arbitrary"),
        name="pool_mix",
    )(p_main, p_main, p_main, w_pool, pool_scale.reshape(1, GROUP_W))


def chunk_masks(chunk, reverse):
    i = jnp.arange(ROW_TILE)[:, None]
    j = jnp.arange(ROW_TILE)[None, :]
    same = (i // chunk) == (j // chunk)
    return (same & ((j >= i) if reverse else (j <= i))).astype(jnp.bfloat16)


def _split3(x):
    hi = x.astype(jnp.bfloat16)
    r = x - hi.astype(jnp.float32)
    mid = r.astype(jnp.bfloat16)
    lo = (r - mid.astype(jnp.float32)).astype(jnp.bfloat16)
    return hi, mid, lo


def _dot3(m, x):
    hi, mid, lo = _split3(x)
    d = functools.partial(jnp.dot, preferred_element_type=jnp.float32)
    return d(m, hi) + (d(m, mid) + d(m, lo))


def _scan_block(n_blk, reverse):
    if reverse:
        return lambda j: jnp.where(j == 0, 0, n_blk - j)
    return lambda j: j


def _hgrn_head(q_ref, v_ref, z_ref, lb_ref, mask_ref, st, cols, reverse):
    bf16 = jnp.bfloat16
    f32 = jnp.float32
    T, C = ROW_TILE, A_CHUNK
    nt = (((1,), (1,)), ((), ()))
    tn = (((0,), (0,)), ((), ()))
    z = z_ref[:, cols]
    log_lb = lb_ref[0:1, cols]
    log1m_lb = lb_ref[1:2, cols]
    one_m_lb = lb_ref[2:3, cols]
    log_sig = jnp.minimum(z, 0.0) - jnp.log1p(jnp.exp(-jnp.abs(z)))
    bb = log1m_lb + log_sig
    mx = jnp.maximum(log_lb, bb)
    lf = mx + jnp.log1p(jnp.exp(-jnp.abs(log_lb - bb)))
    k = one_m_lb / (1.0 + jnp.exp(z))
    aq = q_ref[:, cols]
    q = aq / (1.0 + jnp.exp(-aq))
    v = v_ref[:, cols].astype(bf16)

    b = _dot3(mask_ref[...], lf)
    last = 0 if reverse else C - 1
    b_end = jnp.concatenate([jnp.broadcast_to(b[c * C + last:c * C + last + 1, :], (C, HEAD_DIM))
                             for c in range(T // C)], axis=0)
    q_dec = (q * jnp.exp(b)).astype(bf16)
    k_inv = (k * jnp.exp(-b)).astype(bf16)
    k_dec = (k * jnp.exp(b_end - b)).astype(bf16)
    dec = jnp.exp(b_end)

    scores = lax.dot_general(q_dec, k_inv, nt, preferred_element_type=f32)
    scores = jnp.where(mask_ref[...] > 0, scores, 0.0)
    o_intra = jnp.dot(scores.astype(bf16), v, preferred_element_type=f32)

    n_c = T // C
    rows = [slice(c * C, (c + 1) * C) for c in range(n_c)]
    deltas = [lax.dot_general(v[r], k_dec[r], tn, preferred_element_type=f32) for r in rows]
    entering = [None] * n_c
    for c in (range(n_c - 1, -1, -1) if reverse else range(n_c)):
        entering[c] = st
        st = st * dec[c * C:c * C + 1, :] + deltas[c]
    outs = [lax.dot_general(q_dec[r], entering[c].astype(bf16), nt, preferred_element_type=f32)
            for c, r in enumerate(rows)]
    return o_intra + jnp.concatenate(outs, axis=0), st


def _hgrn_kernel(q_ref, v_ref, z_ref, lb_ref, mask_ref, *rest, reverse, hp):
    if reverse:
        g_ref, of_ref, gn_ref, o_ref, st_ref = rest
    else:
        o_ref, st_ref = rest

    @pl.when(pl.program_id(1) == 0)
    def _():
        st_ref[...] = jnp.zeros_like(st_ref)

    for hh in range(hp):
        cols = slice(hh * HEAD_DIM, (hh + 1) * HEAD_DIM)
        o, st = _hgrn_head(q_ref, v_ref, z_ref, lb_ref, mask_ref, st_ref[hh], cols, reverse)
        st_ref[hh] = st
        if reverse:
            o = o + of_ref[:, cols]
            o = o * lax.rsqrt(jnp.mean(o * o, axis=-1, keepdims=True) + EPS) * gn_ref[:, cols]
            ag = g_ref[:, cols]
            o_ref[:, cols] = (o * (ag / (1.0 + jnp.exp(-ag)))).astype(o_ref.dtype)
        else:
            o_ref[:, cols] = o


def hgrn_scan(p_main, lb_rows, mask, reverse, o_fwd=None, gn=None, hp=SCAN_HP):
    n = p_main.shape[0]
    n_blk = n // ROW_TILE
    tb = _scan_block(n_blk, reverse)
    w = hp * HEAD_DIM
    gpc = GROUP_W // w
    blk = lambda c: pl.BlockSpec((ROW_TILE, w), lambda h, j: (tb(j), c * gpc + h))
    in_specs = [blk(COL_AQ), blk(COL_AI), blk(COL_AFB if reverse else COL_AFF),
                pl.BlockSpec((3, w), lambda h, j: (0, h)),
                pl.BlockSpec((ROW_TILE, ROW_TILE), lambda h, j: (0, 0))]
    args = [p_main, p_main, p_main, lb_rows, mask]
    if reverse:
        in_specs += [blk(COL_AG), pl.BlockSpec((ROW_TILE, w), lambda h, j: (tb(j), h)),
                     pl.BlockSpec((1, w), lambda h, j: (0, h))]
        args += [p_main, o_fwd, gn.reshape(1, GROUP_W)]
    return pl.pallas_call(
        functools.partial(_hgrn_kernel, reverse=reverse, hp=hp),
        grid=(gpc, n_blk),
        in_specs=in_specs,
        out_specs=pl.BlockSpec((ROW_TILE, w), lambda h, j: (tb(j), h)),
        out_shape=jax.ShapeDtypeStruct((n, GROUP_W), jnp.bfloat16 if reverse else jnp.float32),
        scratch_shapes=[pltpu.VMEM((hp, HEAD_DIM, HEAD_DIM), jnp.float32)],
        compiler_params=_cparams("arbitrary", "arbitrary"),
        name="hgrn_bwd" if reverse else "hgrn_fwd",
    )(*args)


def hgrn_mix(p_main, lb_f, lb_b, gn):
    rows = lambda lb: jnp.stack([jnp.log(lb), jnp.log1p(-lb), 1.0 - lb])
    o_f = hgrn_scan(p_main, rows(lb_f), chunk_masks(A_CHUNK, False), False)
    return hgrn_scan(p_main, rows(lb_b), chunk_masks(A_CHUNK, True), True, o_f, gn)


def hgrn_lower_bounds(p):
    cs = jnp.cumsum(jax.nn.softmax(p.astype(jnp.float32), axis=1), axis=1)
    return cs - cs[:, :1]


def rope_tables(n_tok):
    quarter = HEAD_DIM // 4
    pos = jnp.arange(n_tok - CTX_LEN)
    inv_freq = ROPE_BASE ** (-jnp.arange(quarter, dtype=jnp.float32) / quarter)
    lane = jnp.arange(HEAD_DIM)
    p = jnp.where(lane[None, :] < HEAD_DIM // 2, (pos // GRID_W)[:, None], (pos % GRID_W)[:, None]).astype(jnp.float32)
    ang = p * inv_freq[lane % quarter][None, :]
    sign = jnp.where((lane % (2 * quarter)) < quarter, -1.0, 1.0)
    cos = jnp.concatenate([jnp.ones((CTX_LEN, HEAD_DIM), jnp.float32), jnp.cos(ang)], axis=0)
    sin = jnp.concatenate([jnp.zeros((CTX_LEN, HEAD_DIM), jnp.float32), jnp.sin(ang) * sign], axis=0)
    return cos, sin


def _rope(x, cos, sin):
    quarter = HEAD_DIM // 4
    lane = lax.broadcasted_iota(jnp.int32, x.shape, 1)
    partner = jnp.where((lane % (2 * quarter)) < quarter,
                        pltpu.roll(x, HEAD_DIM - quarter, axis=1), pltpu.roll(x, quarter, axis=1))
    return x * cos + partner * sin


def _mlstm_head(q_ref, k_ref, v_ref, cos, sin, cols, b_bc, li_bc, b_row, li_row, state, reverse):
    bf16 = jnp.bfloat16
    f32 = jnp.float32
    T, C = ROW_TILE, D_CHUNK
    n_c = T // C
    nt = (((1,), (1,)), ((), ()))
    tn = (((0,), (0,)), ((), ()))
    d = functools.partial(jnp.dot, preferred_element_type=f32)
    q = _rope(q_ref[:, cols], cos, sin) * HEAD_DIM ** -0.5
    k = _rope(k_ref[:, cols], cos, sin)
    qb = q.astype(bf16)
    kb = k.astype(bf16)
    vb = v_ref[:, cols].astype(bf16)
    ii = lax.broadcasted_iota(jnp.int32, (C, C), 0)
    jj = lax.broadcasted_iota(jnp.int32, (C, C), 1)
    causal = (jj >= ii) if reverse else (jj <= ii)
    last = 0 if reverse else C - 1
    rows = [slice(c * C, (c + 1) * C) for c in range(n_c)]
    bends, m_locs, d_ss, d_ns = [], [], [], []
    for c, r in enumerate(rows):
        bend = b_bc[c * C + last:c * C + last + 1]
        gcol = bend - b_bc[r] + li_bc[r]
        m_loc = jnp.max(gcol, axis=0, keepdims=True)
        kw = k[r] * jnp.exp(gcol - m_loc)
        bends.append(bend)
        m_locs.append(m_loc)
        d_ss.append(lax.dot_general(kw.astype(bf16), vb[r], tn, preferred_element_type=f32))
        d_ns.append(jnp.sum(kw, axis=0, keepdims=True))
    s, nv, m = state
    entering = [None] * n_c
    for c in (range(n_c - 1, -1, -1) if reverse else range(n_c)):
        entering[c] = (s, nv, m)
        m_new = jnp.maximum(bends[c] + m, m_locs[c])
        a = jnp.exp(bends[c] + m - m_new)
        cc = jnp.exp(m_locs[c] - m_new)
        s = a * s + cc * d_ss[c]
        nv = a * nv + cc * d_ns[c]
        m = m_new
    outs = []
    for c, r in enumerate(rows):
        s_in, n_in, m_in = entering[c]
        dmat = jnp.where(causal, b_bc[r][:, 0:C] - b_row[:, r] + li_row[:, r], -jnp.inf)
        m_inter = b_bc[r] + m_in
        m_t = jnp.maximum(m_inter, jnp.max(dmat, axis=-1, keepdims=True))
        w = jnp.exp(dmat - m_t[:, 0:C]) * lax.dot_general(qb[r], kb[r], nt, preferred_element_type=f32)
        a_col = jnp.exp(m_inter - m_t)
        num = d(w.astype(bf16), vb[r]) + a_col * d(qb[r], s_in.astype(bf16))
        den = jnp.sum(w, axis=-1, keepdims=True) + a_col * jnp.sum(q[r] * n_in, axis=-1, keepdims=True)
        outs.append(num / jnp.maximum(jnp.abs(den), jnp.exp(-m_t)))
    return jnp.concatenate(outs, axis=0), (s, nv, m)


def _mlstm_kernel(q_ref, k_ref, v_ref, g_ref, cos_ref, sin_ref, mask_ref, maskt_ref, *rest, reverse, hp):
    if reverse:
        do_ref, of_ref, gn_ref, o_ref, s_ref, n_ref, m_ref, cumt_ref, lit_ref = rest
    else:
        o_ref, s_ref, n_ref, m_ref, cumt_ref, lit_ref = rest
    hg = pl.program_id(0)
    f32 = jnp.float32
    bf16 = jnp.bfloat16

    @pl.when(pl.program_id(1) == 0)
    def _():
        s_ref[...] = jnp.zeros_like(s_ref)
        n_ref[...] = jnp.zeros_like(n_ref)
        m_ref[...] = jnp.zeros_like(m_ref)

    g = g_ref[...]
    lsg = jnp.minimum(g, 0.0) - jnp.log1p(jnp.exp(-jnp.abs(g)))
    cum = _dot3(mask_ref[...], lsg)
    hi, mid, lo = _split3(lsg.T)
    d = functools.partial(jnp.dot, preferred_element_type=f32)
    cumt_ref[...] = d(hi, maskt_ref[...]) + (d(mid, maskt_ref[...]) + d(lo, maskt_ref[...]))
    lit_ref[...] = g.T
    cum_parts = _split3(cum)
    g_parts = _split3(g)
    sel_row = lax.broadcasted_iota(jnp.int32, (LANES, LANES), 0)

    def replicate(parts, col):
        sel = (sel_row == col).astype(bf16)
        return d(parts[0], sel) + (d(parts[1], sel) + d(parts[2], sel))

    cos = cos_ref[...]
    sin = sin_ref[...]
    for hh in range(hp):
        h = hg * hp + hh
        col_i = (GATE_I_B if reverse else GATE_I_F) * N_HEADS + h
        col_f = (GATE_F_B if reverse else GATE_F_F) * N_HEADS + h
        b_bc = replicate(cum_parts, col_f)
        li_bc = replicate(g_parts, col_i)
        b_row = cumt_ref[pl.ds(col_f, 1), :]
        li_row = lit_ref[pl.ds(col_i, 1), :]
        cols = slice(hh * HEAD_DIM, (hh + 1) * HEAD_DIM)
        state = (s_ref[hh], n_ref[hh, 0:1, :], m_ref[hh, 0:1, :])
        o, (s, nv, m) = _mlstm_head(q_ref, k_ref, v_ref, cos, sin, cols, b_bc, li_bc, b_row, li_row, state, reverse)
        s_ref[hh] = s
        n_ref[hh] = jnp.broadcast_to(nv, n_ref.shape[1:])
        m_ref[hh] = jnp.broadcast_to(m, m_ref.shape[1:])
        if reverse:
            o = o + of_ref[:, cols]
            o = o * lax.rsqrt(jnp.mean(o * o, axis=-1, keepdims=True) + EPS) * gn_ref[:, cols]
            o_ref[:, cols] = (o / (1.0 + jnp.exp(-do_ref[:, cols]))).astype(o_ref.dtype)
        else:
            o_ref[:, cols] = o


def mlstm_scan(p_main, p_gate, cos, sin, mask, reverse, o_fwd=None, gn=None, hp=SCAN_HP):
    n = p_main.shape[0]
    n_blk = n // ROW_TILE
    tb = _scan_block(n_blk, reverse)
    w = hp * HEAD_DIM
    gpc = GROUP_W // w
    blk = lambda c: pl.BlockSpec((ROW_TILE, w), lambda h, j: (tb(j), c * gpc + h))
    tok = pl.BlockSpec((ROW_TILE, LANES), lambda h, j: (tb(j), 0))
    sq = pl.BlockSpec((ROW_TILE, ROW_TILE), lambda h, j: (0, 0))
    in_specs = [blk(COL_DQ), blk(COL_DK), blk(COL_DV), tok, tok, tok, sq, sq]
    args = [p_main, p_main, p_main, p_gate, cos, sin, mask, mask.T]
    if reverse:
        in_specs += [blk(COL_DO), pl.BlockSpec((ROW_TILE, w), lambda h, j: (tb(j), h)),
                     pl.BlockSpec((1, w), lambda h, j: (0, h))]
        args += [p_main, o_fwd, gn.reshape(1, GROUP_W)]
    return pl.pallas_call(
        functools.partial(_mlstm_kernel, reverse=reverse, hp=hp),
        grid=(gpc, n_blk),
        in_specs=in_specs,
        out_specs=pl.BlockSpec((ROW_TILE, w), lambda h, j: (tb(j), h)),
        out_shape=jax.ShapeDtypeStruct((n, GROUP_W), jnp.bfloat16 if reverse else jnp.float32),
        scratch_shapes=[pltpu.VMEM((hp, HEAD_DIM, HEAD_DIM), jnp.float32), pltpu.VMEM((hp, 8, HEAD_DIM), jnp.float32),
                        pltpu.VMEM((hp, 8, LANES), jnp.float32), pltpu.VMEM((LANES, ROW_TILE), jnp.float32),
                        pltpu.VMEM((LANES, ROW_TILE), jnp.float32)],
        compiler_params=_cparams("arbitrary", "arbitrary"),
        name="mlstm_bwd" if reverse else "mlstm_fwd",
    )(*args)


def mlstm_mix(p_main, p_gate, gn):
    cos, sin = rope_tables(p_main.shape[0])
    o_f = mlstm_scan(p_main, p_gate, cos, sin, chunk_masks(D_CHUNK, False), False)
    return mlstm_scan(p_main, p_gate, cos, sin, chunk_masks(D_CHUNK, True), True, o_f, gn)


def kernel(x, c, ctx, c_ctx, ada_w, ada_b, norm1_w, norm2_w, mix_w_in, mix_b_in, hgrn_lb, hgrn_gn, natten_rpb,
           pool_w, pool_scale, mlstm_gn, mix_w_out, moe_wg, moe_bg, moe_we, moe_be, moe_w1, moe_w3, moe_w2, final_w):
    assert x.shape == (1, SEQ, D_MODEL) and ctx.shape == (1, CTX_LEN, D_MODEL)
    bf16 = jnp.bfloat16
    lbs = hgrn_lower_bounds(hgrn_lb)

    cond = jnp.zeros((16, D_MODEL), jnp.float32).at[0].set(jax.nn.silu(c_ctx)).at[1].set(jax.nn.silu(c[0]))
    mods = ada_modulation(cond.astype(bf16), ada_w, ada_b)[:, :2].reshape(DEPTH, 2, 6, D_MODEL)

    w_in = cast_bf16(mix_w_in, LANES)
    w_out = cast_bf16(mix_w_out, 4 * LANES)

    xs = jnp.concatenate([ctx[0], x[0]], axis=0)
    for l in range(DEPTH):
        mod = mods[l]
        h = norm_modulate(xs, norm1_w[l], mod, 0)
        p_main, w1, w3, w2 = in_proj_cast(h, w_in, mix_b_in[l], l, moe_w1, moe_w3, moe_w2)
        w_gate = jnp.pad(w_in[l, :, N_MAIN:], ((0, 0), (0, LANES - N_GATE)))
        b_gate = jnp.pad(mix_b_in[l, N_MAIN:], (0, LANES - N_GATE))
        p_gate = matmul_bias(h, w_gate, b_gate, MM_TM, LANES)

        out_a = hgrn_mix(p_main, lbs[0, l], lbs[1, l], hgrn_gn[l])
        out_b = natten(p_main, natten_bias_tiles(natten_rpb[l]))
        out_c = pool_mix(p_main, pool_w[l].astype(bf16), pool_scale[l])
        out_d = mlstm_mix(p_main, p_gate, mlstm_gn[l])
        xs = matmul_gated_residual((out_a, out_b, out_c, out_d), w_out, l, xs, mod, 2)

        w_router = jnp.pad(jnp.concatenate([moe_wg[l], moe_we[l]], axis=1),
                           ((0, 0), (0, LANES - N_GROUPS - N_EXPERTS)))
        h2, logits = norm_modulate_router(xs, norm2_w[l], mod, w_router)
        eid, top_w = moe_route(logits, moe_bg[l], moe_be[l])
        tile_expert, tile_valid, row_token, cpos = moe_schedule(eid)
        ys = moe_grouped(h2, tile_expert, tile_valid, row_token, w1, w3, w2)
        xs = moe_combine(cpos, ys, xs, top_w, mod, 5)

    return norm_modulate(xs, final_w, mods[0], None, out_dtype=jnp.float32, skip_rows=CTX_LEN)[None]
```

```python
import functools

import jax
import jax.numpy as jnp
from jax import lax
from jax.experimental import pallas as pl
from jax.experimental.pallas import tpu as pltpu

D_MODEL = 4096
SEQ = 8192
DEPTH = 2
GRID_W = 64
CTX_LEN = 256
N_TOK = CTX_LEN + SEQ
GROUP_W = 1024
HEAD_DIM = 128
N_HEADS = GROUP_W // HEAD_DIM
C_GROUPS = 4
C_GW = GROUP_W // C_GROUPS
POOL_WINDOWS = (2, 4, 8, 16)
NA_MAX_ROWS = 8
NA_COLS = 16
A_CHUNK = 32
D_CHUNK = 64
ROPE_BASE = 10000.0
N_GROUPS = 4
EXP_PER_GROUP = 8
N_EXPERTS = N_GROUPS * EXP_PER_GROUP
TOP_K = 2
D_FF = D_MODEL // 8
EPS = 1e-6
N_MAIN = 13 * GROUP_W
N_GATE = 4 * N_HEADS
LANES = 128
VMEM_LIMIT = 56 * 1024 * 1024

ROW_TILE = 256
MM_TM = 768
MM_TN = 512
MOE_TM = 256
MOE_TILES = (TOP_K * N_TOK + N_EXPERTS * (MOE_TM - 1)) // MOE_TM + 1
CMB_TB = 128
SCAN_HP = 8
DMA_UNROLL = 8
CAST_SLABS = 8
CT_TN, CT_TK = 1024, 512

COL_AQ, COL_AI, COL_AG, COL_AFF, COL_AFB = 0, 1, 2, 3, 4
COL_Q, COL_K, COL_V, COL_C = 5, 6, 7, 8
COL_DQ, COL_DK, COL_DV, COL_DO = 9, 10, 11, 12
GATE_I_F, GATE_F_F, GATE_I_B, GATE_F_B = 0, 1, 2, 3
HPG = GROUP_W // HEAD_DIM


def _cparams(*sem):
    return pltpu.CompilerParams(dimension_semantics=sem, vmem_limit_bytes=VMEM_LIMIT)


def _ada_kernel(c_ref, w_ref, b_ref, o_ref):
    acc = jnp.dot(c_ref[...], w_ref[...].astype(jnp.bfloat16), preferred_element_type=jnp.float32)
    o_ref[...] = acc + b_ref[...]


def ada_modulation(cond, ada_w, ada_b):
    tn = 512
    n = ada_w.shape[-1]
    rows = cond.shape[0]
    return pl.pallas_call(
        _ada_kernel,
        grid=(DEPTH, n // tn),
        in_specs=[pl.BlockSpec((rows, D_MODEL), lambda l, j: (0, 0)),
                  pl.BlockSpec((None, D_MODEL, tn), lambda l, j: (l, 0, j)),
                  pl.BlockSpec((None, 1, tn), lambda l, j: (l, 0, j))],
        out_specs=pl.BlockSpec((None, rows, tn), lambda l, j: (l, 0, j)),
        out_shape=jax.ShapeDtypeStruct((DEPTH, rows, n), jnp.float32),
        compiler_params=_cparams("arbitrary", "arbitrary"),
        name="ada_modulation",
    )(cond, ada_w, ada_b.reshape(DEPTH, 1, n))


def _norm_kernel(x_ref, w_ref, mod_ref, o_ref, *, shift_row, scale_row):
    x = x_ref[...]
    y = x * lax.rsqrt(jnp.mean(x * x, axis=-1, keepdims=True) + EPS) * w_ref[...]
    if shift_row is not None:
        y = y * (1.0 + mod_ref[scale_row:scale_row + 1, :]) + mod_ref[shift_row:shift_row + 1, :]
    o_ref[...] = y.astype(o_ref.dtype)


def _norm_router_kernel(x_ref, w_ref, mod_ref, wr_ref, o_ref, lg_ref, *, shift_row, scale_row):
    x = x_ref[...]
    y = x * lax.rsqrt(jnp.mean(x * x, axis=-1, keepdims=True) + EPS) * w_ref[...]
    y = y * (1.0 + mod_ref[scale_row:scale_row + 1, :]) + mod_ref[shift_row:shift_row + 1, :]
    o_ref[...] = y.astype(o_ref.dtype)
    y_hi = y.astype(jnp.bfloat16)
    y_lo = (y - y_hi.astype(jnp.float32)).astype(jnp.bfloat16)
    w = wr_ref[...]
    w_hi = w.astype(jnp.bfloat16)
    w_lo = (w - w_hi.astype(jnp.float32)).astype(jnp.bfloat16)
    dot = functools.partial(jnp.dot, preferred_element_type=jnp.float32)
    lg_ref[...] = dot(y_hi, w_hi) + (dot(y_lo, w_hi) + dot(y_hi, w_lo))


def _tok_type(i):
    return jnp.where(i * ROW_TILE >= CTX_LEN, 1, 0)


def norm_modulate(x, w, mod, which, out_dtype=jnp.bfloat16, skip_rows=0):
    n = x.shape[0] - skip_rows
    off = skip_rows // ROW_TILE
    rows = (None, None) if which is None else (3 * which, 3 * which + 1)
    return pl.pallas_call(
        functools.partial(_norm_kernel, shift_row=rows[0], scale_row=rows[1]),
        grid=(n // ROW_TILE,),
        in_specs=[pl.BlockSpec((ROW_TILE, D_MODEL), lambda i: (i + off, 0)),
                  pl.BlockSpec((1, D_MODEL), lambda i: (0, 0)),
                  pl.BlockSpec((None, 6, D_MODEL), lambda i: (_tok_type(i + off), 0, 0))],
        out_specs=pl.BlockSpec((ROW_TILE, D_MODEL), lambda i: (i, 0)),
        out_shape=jax.ShapeDtypeStruct((n, D_MODEL), out_dtype),
        compiler_params=_cparams("arbitrary"),
        name="norm_modulate",
    )(x, w.reshape(1, D_MODEL), mod)


def norm_modulate_router(x, w, mod, w_router):
    n = x.shape[0]
    return pl.pallas_call(
        functools.partial(_norm_router_kernel, shift_row=3, scale_row=4),
        grid=(n // ROW_TILE,),
        in_specs=[pl.BlockSpec((ROW_TILE, D_MODEL), lambda i: (i, 0)),
                  pl.BlockSpec((1, D_MODEL), lambda i: (0, 0)),
                  pl.BlockSpec((None, 6, D_MODEL), lambda i: (_tok_type(i), 0, 0)),
                  pl.BlockSpec((D_MODEL, LANES), lambda i: (0, 0))],
        out_specs=[pl.BlockSpec((ROW_TILE, D_MODEL), lambda i: (i, 0)),
                   pl.BlockSpec((ROW_TILE, LANES), lambda i: (i, 0))],
        out_shape=[jax.ShapeDtypeStruct((n, D_MODEL), jnp.float32),
                   jax.ShapeDtypeStruct((n, LANES), jnp.float32)],
        compiler_params=_cparams("arbitrary"),
        name="norm_modulate_router",
    )(x, w.reshape(1, D_MODEL), mod, w_router)


def _mm_bias_nt_kernel(a_ref, bt_ref, bias_ref, o_ref):
    bt = bt_ref[...].astype(jnp.bfloat16)
    acc = lax.dot_general(a_ref[...], bt, (((1,), (1,)), ((), ())), preferred_element_type=jnp.float32)
    o_ref[...] = acc + bias_ref[...]


def _mm_resid_kernel(a0_ref, a1_ref, a2_ref, a3_ref, b_ref, x_ref, mod_ref, o_ref, *, gate_row):
    acc = None
    for g, a_ref in enumerate((a0_ref, a1_ref, a2_ref, a3_ref)):
        part = jnp.dot(a_ref[...], b_ref[g * GROUP_W:(g + 1) * GROUP_W, :], preferred_element_type=jnp.float32)
        acc = part if acc is None else acc + part
    tm = x_ref.shape[0]
    row = pl.program_id(0) * tm + lax.broadcasted_iota(jnp.int32, (tm, 1), 0)
    gate = jnp.where(row < CTX_LEN, mod_ref[0, gate_row:gate_row + 1, :], mod_ref[1, gate_row:gate_row + 1, :])
    o_ref[...] = x_ref[...] + gate * acc


def matmul_bias_nt(a, bt, bias, tm):
    m, k = a.shape
    n = bt.shape[0]
    return pl.pallas_call(
        _mm_bias_nt_kernel,
        grid=(m // tm,),
        in_specs=[pl.BlockSpec((tm, k), lambda i: (i, 0)),
                  pl.BlockSpec((n, k), lambda i: (0, 0)),
                  pl.BlockSpec((1, n), lambda i: (0, 0))],
        out_specs=pl.BlockSpec((tm, n), lambda i: (i, 0)),
        out_shape=jax.ShapeDtypeStruct((m, n), jnp.float32),
        compiler_params=_cparams("arbitrary"),
        name="matmul_bias_nt",
    )(a, bt, bias.reshape(1, n))


def matmul_gated_residual(parts, b, layer, x, mod, gate_row):
    m, n = x.shape
    tm, tn = MM_TM, MM_TN
    a_spec = pl.BlockSpec((tm, GROUP_W), lambda i, j: (i, 0))
    return pl.pallas_call(
        functools.partial(_mm_resid_kernel, gate_row=gate_row),
        grid=(m // tm, n // tn),
        in_specs=[a_spec, a_spec, a_spec, a_spec,
                  pl.BlockSpec((None, b.shape[1], tn), lambda i, j: (layer, 0, j)),
                  pl.BlockSpec((tm, tn), lambda i, j: (i, j)),
                  pl.BlockSpec((2, 6, tn), lambda i, j: (0, 0, j))],
        out_specs=pl.BlockSpec((tm, tn), lambda i, j: (i, j)),
        out_shape=jax.ShapeDtypeStruct((m, n), jnp.float32),
        compiler_params=_cparams("arbitrary", "arbitrary"),
        name="matmul_gated_residual",
    )(*parts, b, x, mod)


def _cast_kernel(x_ref, o_ref):
    o_ref[...] = x_ref[...].astype(o_ref.dtype)


def cast_bf16(w, rows):
    nl, r, c = w.shape
    spec = pl.BlockSpec((None, rows, c), lambda l, i: (l, i, 0))
    return pl.pallas_call(
        _cast_kernel,
        grid=(nl, r // rows),
        in_specs=[spec],
        out_specs=spec,
        out_shape=jax.ShapeDtypeStruct(w.shape, jnp.bfloat16),
        compiler_params=_cparams("arbitrary", "arbitrary"),
        name="cast_bf16",
    )(w)


def _cast_t_kernel(x_ref, o_ref):
    o_ref[...] = x_ref[...].T.astype(o_ref.dtype)


def cast_bf16_transposed(wt, n_cols):
    nl, _, k = wt.shape
    return pl.pallas_call(
        _cast_t_kernel,
        grid=(nl, k // CT_TK, n_cols // CT_TN),
        in_specs=[pl.BlockSpec((None, CT_TN, CT_TK), lambda l, i, j: (l, j, i))],
        out_specs=pl.BlockSpec((None, CT_TK, CT_TN), lambda l, i, j: (l, i, j)),
        out_shape=jax.ShapeDtypeStruct((nl, k, n_cols), jnp.bfloat16),
        compiler_params=_cparams("arbitrary", "arbitrary", "arbitrary"),
        name="cast_bf16_transposed",
    )(wt)


def _in_proj_cast_kernel(a_ref, b_ref, bias_ref, w1_ref, w3_ref, w2_ref, o_ref, w1o_ref, w3o_ref, w2o_ref):
    o_ref[...] = jnp.dot(a_ref[...], b_ref[...], preferred_element_type=jnp.float32) + bias_ref[...]
    w1o_ref[...] = w1_ref[...].astype(w1o_ref.dtype)
    w3o_ref[...] = w3_ref[...].astype(w3o_ref.dtype)
    w2o_ref[...] = w2_ref[...].astype(w2o_ref.dtype)


def in_proj_cast(a, b, bias, layer, w1, w3, w2):
    m, k = a.shape
    n, tm, tn = N_MAIN, MM_TM, MM_TN
    n_j = n // tn
    n_slabs = N_EXPERTS * CAST_SLABS
    assert (m // tm) * n_j >= n_slabs
    r13, r2 = D_MODEL // CAST_SLABS, D_FF // CAST_SLABS

    def slab(i, j):
        s = jnp.minimum(i * n_j + j, n_slabs - 1)
        return s // CAST_SLABS, s % CAST_SLABS

    in13 = pl.BlockSpec((None, None, r13, D_FF), lambda i, j: (layer, *slab(i, j), 0))
    in2 = pl.BlockSpec((None, None, r2, D_MODEL), lambda i, j: (layer, *slab(i, j), 0))
    out13 = pl.BlockSpec((None, r13, D_FF), lambda i, j: (*slab(i, j), 0))
    out2 = pl.BlockSpec((None, r2, D_MODEL), lambda i, j: (*slab(i, j), 0))
    bf16 = jnp.bfloat16
    return pl.pallas_call(
        _in_proj_cast_kernel,
        grid=(m // tm, n_j),
        in_specs=[pl.BlockSpec((tm, k), lambda i, j: (i, 0)),
                  pl.BlockSpec((None, k, tn), lambda i, j: (layer, 0, j)),
                  pl.BlockSpec((1, tn), lambda i, j: (0, j)),
                  in13, in13, in2],
        out_specs=[pl.BlockSpec((tm, tn), lambda i, j: (i, j)), out13, out13, out2],
        out_shape=[jax.ShapeDtypeStruct((m, n), jnp.float32),
                   jax.ShapeDtypeStruct((N_EXPERTS, D_MODEL, D_FF), bf16),
                   jax.ShapeDtypeStruct((N_EXPERTS, D_MODEL, D_FF), bf16),
                   jax.ShapeDtypeStruct((N_EXPERTS, D_FF, D_MODEL), bf16)],
        compiler_params=_cparams("arbitrary", "arbitrary"),
        name="in_proj_cast",
    )(a, b, bias.reshape(1, -1), w1, w3, w2)


def _moe_kernel(te_ref, tv_ref, rt_ref, h_hbm, w1_ref, w3_ref, w2_ref, y_ref, xg_ref, sem_ref):
    del te_ref
    i = pl.program_id(0)
    n_tiles = pl.num_programs(0)

    def row_copy(tile, slot, r):
        tok = rt_ref[tile * MOE_TM + r]
        return pltpu.make_async_copy(h_hbm.at[pl.ds(tok, 1), :], xg_ref.at[slot, pl.ds(r, 1), :], sem_ref.at[slot])

    def start_gather(tile, slot):
        @pl.when(tv_ref[tile] != 0)
        def _():
            def body(r, carry):
                row_copy(tile, slot, r).start()
                return carry
            lax.fori_loop(0, MOE_TM, body, 0, unroll=DMA_UNROLL)

    @pl.when(i == 0)
    def _():
        start_gather(0, 0)

    @pl.when(i + 1 < n_tiles)
    def _():
        start_gather(i + 1, (i + 1) % 2)

    @pl.when(tv_ref[i] != 0)
    def _():
        slot = i % 2

        def wait_body(r, carry):
            row_copy(i, slot, r).wait()
            return carry
        lax.fori_loop(0, MOE_TM, wait_body, 0, unroll=DMA_UNROLL)
        xg = xg_ref[slot].astype(jnp.bfloat16)
        a = jnp.dot(xg, w1_ref[...], preferred_element_type=jnp.float32)
        b = jnp.dot(xg, w3_ref[...], preferred_element_type=jnp.float32)
        act = (a * jax.nn.sigmoid(a)) * b
        y_ref[...] = jnp.dot(act.astype(jnp.bfloat16), w2_ref[...], preferred_element_type=jnp.float32)

    @pl.when(tv_ref[i] == 0)
    def _():
        y_ref[...] = jnp.zeros_like(y_ref)


def moe_grouped(h, tile_expert, tile_valid, row_token, w1, w3, w2):
    grid_spec = pltpu.PrefetchScalarGridSpec(
        num_scalar_prefetch=3,
        grid=(MOE_TILES,),
        in_specs=[pl.BlockSpec(memory_space=pl.ANY),
                  pl.BlockSpec((None, D_MODEL, D_FF), lambda i, te, tv, rt: (te[i], 0, 0)),
                  pl.BlockSpec((None, D_MODEL, D_FF), lambda i, te, tv, rt: (te[i], 0, 0)),
                  pl.BlockSpec((None, D_FF, D_MODEL), lambda i, te, tv, rt: (te[i], 0, 0))],
        out_specs=pl.BlockSpec((MOE_TM, D_MODEL), lambda i, te, tv, rt: (i, 0)),
        scratch_shapes=[pltpu.VMEM((2, MOE_TM, D_MODEL), jnp.float32), pltpu.SemaphoreType.DMA((2,))],
    )
    return pl.pallas_call(
        _moe_kernel,
        grid_spec=grid_spec,
        out_shape=jax.ShapeDtypeStruct((MOE_TILES * MOE_TM, D_MODEL), jnp.float32),
        compiler_params=_cparams("arbitrary"),
        name="moe_grouped",
    )(tile_expert, tile_valid, row_token, h, w1, w3, w2)


def _combine_kernel(pos_ref, y_hbm, x_ref, w_ref, mod_ref, o_ref, buf_ref, sem_ref, *, gate_row):
    i = pl.program_id(0)
    n_steps = pl.num_programs(0)

    def row_copy(step, slot, r):
        src = pos_ref[step * (TOP_K * CMB_TB) + r]
        return pltpu.make_async_copy(y_hbm.at[pl.ds(src, 1), :], buf_ref.at[slot, pl.ds(r, 1), :], sem_ref.at[slot])

    def start_gather(step, slot):
        def body(r, carry):
            row_copy(step, slot, r).start()
            return carry
        lax.fori_loop(0, TOP_K * CMB_TB, body, 0, unroll=DMA_UNROLL)

    @pl.when(i == 0)
    def _():
        start_gather(0, 0)

    @pl.when(i + 1 < n_steps)
    def _():
        start_gather(i + 1, (i + 1) % 2)

    slot = i % 2

    def wait_body(r, carry):
        row_copy(i, slot, r).wait()
        return carry
    lax.fori_loop(0, TOP_K * CMB_TB, wait_body, 0, unroll=DMA_UNROLL)
    w = w_ref[...]
    moe = w[:, 0:1] * buf_ref[slot, 0:CMB_TB, :] + w[:, 1:2] * buf_ref[slot, CMB_TB:2 * CMB_TB, :]
    o_ref[...] = x_ref[...] + mod_ref[gate_row:gate_row + 1, :] * moe


def moe_combine(pos, ys, x, top_w, mod, gate_row):
    n = x.shape[0]
    grid_spec = pltpu.PrefetchScalarGridSpec(
        num_scalar_prefetch=1,
        grid=(n // CMB_TB,),
        in_specs=[pl.BlockSpec(memory_space=pl.ANY),
                  pl.BlockSpec((CMB_TB, D_MODEL), lambda i, pos: (i, 0)),
                  pl.BlockSpec((CMB_TB, TOP_K), lambda i, pos: (i, 0)),
                  pl.BlockSpec((None, 6, D_MODEL), lambda i, pos: (jnp.where(i * CMB_TB >= CTX_LEN, 1, 0), 0, 0))],
        out_specs=pl.BlockSpec((CMB_TB, D_MODEL), lambda i, pos: (i, 0)),
        scratch_shapes=[pltpu.VMEM((2, TOP_K * CMB_TB, D_MODEL), jnp.float32), pltpu.SemaphoreType.DMA((2,))],
    )
    return pl.pallas_call(
        functools.partial(_combine_kernel, gate_row=gate_row),
        grid_spec=grid_spec,
        out_shape=jax.ShapeDtypeStruct((n, D_MODEL), jnp.float32),
        compiler_params=_cparams("arbitrary"),
        name="moe_combine",
    )(pos, ys, x, top_w, mod)


def moe_route(logits, bg, be):
    n = logits.shape[0]
    g_logits = logits[:, :N_GROUPS] + bg
    grp = jnp.argmax(g_logits, axis=-1)
    g_sel = grp[:, None] == jnp.arange(N_GROUPS)[None, :]
    p_grp = jnp.sum(jnp.where(g_sel, jax.nn.softmax(g_logits, axis=-1), 0.0), axis=-1, keepdims=True)
    e_logits = (logits[:, N_GROUPS:N_GROUPS + N_EXPERTS] + be).reshape(n, N_GROUPS, EXP_PER_GROUP)
    e_in = jnp.sum(jnp.where(g_sel[:, :, None], e_logits, 0.0), axis=1)
    lane = jnp.arange(EXP_PER_GROUP)[None, :]
    i1 = jnp.argmax(e_in, axis=-1)
    v1 = jnp.max(e_in, axis=-1)
    rest = jnp.where(lane == i1[:, None], -jnp.inf, e_in)
    i2 = jnp.argmax(rest, axis=-1)
    v2 = jnp.max(rest, axis=-1)
    top_w = jax.nn.softmax(jnp.stack([v1, v2], axis=-1), axis=-1) * p_grp
    eid = grp[:, None] * EXP_PER_GROUP + jnp.stack([i1, i2], axis=-1)
    return eid.astype(jnp.int32), top_w


def moe_schedule(eid):
    n = eid.shape[0]
    flat_e = eid.reshape(-1)
    onehot = (flat_e[:, None] == jnp.arange(N_EXPERTS)[None, :]).astype(jnp.int32)
    rank = jnp.sum(onehot * (jnp.cumsum(onehot, axis=0) - 1), axis=1)
    counts = jnp.sum(onehot, axis=0)
    padded = ((counts + MOE_TM - 1) // MOE_TM) * MOE_TM
    pad_end = jnp.cumsum(padded)
    pad_start = pad_end - padded
    dest = (jnp.sum(onehot * pad_start[None, :], axis=1) + rank).astype(jnp.int32)
    n_rows = MOE_TILES * MOE_TM
    row_token = jnp.zeros((n_rows,), jnp.int32).at[dest].set(jnp.arange(TOP_K * n, dtype=jnp.int32) // TOP_K)
    tile_row0 = jnp.arange(MOE_TILES, dtype=jnp.int32) * MOE_TM
    n_valid = pad_end[-1] // MOE_TM
    tile_valid = (tile_row0 < pad_end[-1]).astype(jnp.int32)
    tile_expert = jnp.sum((tile_row0[:, None] >= pad_end[None, :]).astype(jnp.int32), axis=1)
    last_expert = jnp.sum(jnp.where(jnp.arange(MOE_TILES) == n_valid - 1, tile_expert, 0))
    tile_expert = jnp.where(tile_valid != 0, tile_expert, last_expert).astype(jnp.int32)
    pos = dest.reshape(n // CMB_TB, CMB_TB, TOP_K).transpose(0, 2, 1).reshape(-1)
    return tile_expert, tile_valid, row_token, pos


NA_QR = 4
NA_KR = NA_QR + NA_MAX_ROWS - 1
NA_TQ = NA_QR * GRID_W
NA_TK = NA_KR * GRID_W
NA_HP = 8
NA_WIN_BLOCKS = 3


def natten_bias_tiles(rpb):
    hi = lax.Precision.HIGHEST
    cols = jnp.arange(GRID_W)
    dc = jnp.clip(cols[None, :] - cols[:, None] + (NA_COLS - 1), 0, 2 * NA_COLS - 2)
    c0 = jnp.clip(cols - NA_COLS // 2, 0, GRID_W - NA_COLS)
    in_win = (cols[None, :] >= c0[:, None]) & (cols[None, :] < c0[:, None] + NA_COLS)
    oh_c = (dc[..., None] == jnp.arange(2 * NA_COLS - 1)).astype(jnp.float32)
    toep = jnp.einsum('qkc,hdc->hdqk', oh_c, rpb.astype(jnp.float32), precision=hi)
    rl = jnp.arange(NA_QR)[:, None]
    kl = jnp.arange(NA_KR)[None, :]
    half = NA_MAX_ROWS // 2
    pats = [(kl - rl + NA_MAX_ROWS - 1, (kl < NA_MAX_ROWS) & (rl >= 0)),
            (kl - rl + NA_MAX_ROWS - 1 - half, (kl - rl >= 0) & (kl - rl < NA_MAX_ROWS)),
            (kl - rl, (kl >= NA_KR - NA_MAX_ROWS) & (rl >= 0))]
    tiles = []
    for dr, valid in pats:
        oh_r = ((dr[..., None] == jnp.arange(2 * NA_MAX_ROWS - 1)) & valid[..., None]).astype(jnp.float32)
        t = jnp.einsum('rkd,hdqc->hrqkc', oh_r, toep, precision=hi)
        ok = valid[None, :, None, :, None] & in_win[None, None, :, None, :]
        tiles.append(jnp.where(ok, t, -jnp.inf).reshape(rpb.shape[0], NA_TQ, NA_TK))
    return jnp.stack(tiles, axis=1)


def _natten_kernel(q_ref, kc_ref, vc_ref, k0_ref, k1_ref, k2_ref, v0_ref, v1_ref, v2_ref, bias_ref, o_ref,
                   kw_ref, vw_ref, *, n_rows, n_blk, hp):
    j = pl.program_id(1)
    bf16 = jnp.bfloat16
    f32 = jnp.float32
    nt = (((1,), (1,)), ((), ()))
    heads = [slice(hh * HEAD_DIM, (hh + 1) * HEAD_DIM) for hh in range(hp)]

    def ctx_scores(cols):
        q = (q_ref[:, cols] * HEAD_DIM ** -0.5).astype(bf16)
        kc = kc_ref[:, cols].astype(bf16)
        return q, lax.dot_general(q, kc, nt, preferred_element_type=f32)

    @pl.when(j == 0)
    def _():
        for cols in heads:
            _, s_ctx = ctx_scores(cols)
            m = jnp.max(s_ctx, axis=-1, keepdims=True)
            p = jnp.exp(s_ctx - m)
            l = jnp.sum(p, axis=-1, keepdims=True)
            o = jnp.dot(p.astype(bf16), vc_ref[:, cols].astype(bf16), preferred_element_type=f32)
            o_ref[:, cols] = (o / l).astype(o_ref.dtype)

    @pl.when(j > 0)
    def _():
        for t, (k_ref, v_ref) in enumerate(((k0_ref, v0_ref), (k1_ref, v1_ref), (k2_ref, v2_ref))):
            kw_ref[t * ROW_TILE:(t + 1) * ROW_TILE, :] = k_ref[...].astype(bf16)
            vw_ref[t * ROW_TILE:(t + 1) * ROW_TILE, :] = v_ref[...].astype(bf16)
        r0 = (j - 1) * NA_QR
        kbase = jnp.clip(r0 - NA_MAX_ROWS // 2, 0, n_rows - NA_KR)
        first_blk = jnp.clip(j - 1, 1, n_blk - NA_WIN_BLOCKS)
        off = pl.multiple_of(CTX_LEN + kbase * GRID_W - first_blk * ROW_TILE, GRID_W)
        for hh, cols in enumerate(heads):
            q, s_ctx = ctx_scores(cols)
            kl = kw_ref[pl.ds(off, NA_TK), cols]
            vl = vw_ref[pl.ds(off, NA_TK), cols]
            s_loc = lax.dot_general(q, kl, nt, preferred_element_type=f32) + bias_ref[hh]
            m = jnp.maximum(jnp.max(s_loc, axis=-1, keepdims=True), jnp.max(s_ctx, axis=-1, keepdims=True))
            p_loc = jnp.exp(s_loc - m)
            p_ctx = jnp.exp(s_ctx - m)
            l = jnp.sum(p_loc, axis=-1, keepdims=True) + jnp.sum(p_ctx, axis=-1, keepdims=True)
            o = (jnp.dot(p_loc.astype(bf16), vl, preferred_element_type=f32)
                 + jnp.dot(p_ctx.astype(bf16), vc_ref[:, cols].astype(bf16), preferred_element_type=f32))
            o_ref[:, cols] = (o / l).astype(o_ref.dtype)


def natten(p_main, bias_tiles, hp=NA_HP):
    n = p_main.shape[0]
    n_rows = (n - CTX_LEN) // GRID_W
    n_blocks = n_rows // NA_QR
    n_blk = n // ROW_TILE
    assert n_rows % NA_QR == 0 and n_rows >= NA_KR and CTX_LEN == NA_TQ == ROW_TILE
    assert NA_TK + GRID_W <= NA_WIN_BLOCKS * ROW_TILE and n_blk > NA_WIN_BLOCKS
    w = hp * HEAD_DIM
    gpc = GROUP_W // w

    def pat(j):
        return jnp.where(j <= 1, 0, jnp.where(j == n_blocks, 2, 1))

    def win(col, t):
        return pl.BlockSpec((ROW_TILE, w), lambda h, j: (jnp.clip(j - 1, 1, n_blk - NA_WIN_BLOCKS) + t, col * gpc + h))

    return pl.pallas_call(
        functools.partial(_natten_kernel, n_rows=n_rows, n_blk=n_blk, hp=hp),
        grid=(gpc, n_blocks + 1),
        in_specs=[pl.BlockSpec((NA_TQ, w), lambda h, j: (j, COL_Q * gpc + h)),
                  pl.BlockSpec((CTX_LEN, w), lambda h, j: (0, COL_K * gpc + h)),
                  pl.BlockSpec((CTX_LEN, w), lambda h, j: (0, COL_V * gpc + h)),
                  win(COL_K, 0), win(COL_K, 1), win(COL_K, 2), win(COL_V, 0), win(COL_V, 1), win(COL_V, 2),
                  pl.BlockSpec((hp, None, NA_TQ, NA_TK), lambda h, j: (h, pat(j), 0, 0))],
        out_specs=pl.BlockSpec((NA_TQ, w), lambda h, j: (j, h)),
        out_shape=jax.ShapeDtypeStruct((n, GROUP_W), jnp.bfloat16),
        scratch_shapes=[pltpu.VMEM((NA_WIN_BLOCKS * ROW_TILE, w), jnp.bfloat16),
                        pltpu.VMEM((NA_WIN_BLOCKS * ROW_TILE, w), jnp.bfloat16)],
        compiler_params=_cparams("arbitrary", "arbitrary"),
        name="natten",
    )(*([p_main] * 9), bias_tiles)


POOL_HALO = max(POOL_WINDOWS) // 2


def _pool_kernel(prev_ref, cur_ref, next_ref, w_ref, s_ref, o_ref, ext_ref, *, n_tok):
    i = pl.program_id(0)
    t0 = i * ROW_TILE
    seq_lo = jnp.where(t0 < CTX_LEN, 0, CTX_LEN)
    seq_hi = jnp.where(t0 < CTX_LEN, CTX_LEN, n_tok)
    ext_ref[0:POOL_HALO, :] = jnp.where(t0 - POOL_HALO >= seq_lo, prev_ref[...], 0.0)
    ext_ref[POOL_HALO:POOL_HALO + ROW_TILE, :] = cur_ref[...]
    ext_ref[POOL_HALO + ROW_TILE:, :] = jnp.where(t0 + ROW_TILE < seq_hi, next_ref[...], 0.0)
    t = t0 + lax.broadcasted_iota(jnp.int32, (ROW_TILE, 1), 0)
    for g, win in enumerate(POOL_WINDOWS):
        half = win // 2
        cs = slice(g * C_GW, (g + 1) * C_GW)
        acc = ext_ref[POOL_HALO - half:POOL_HALO - half + ROW_TILE, cs]
        for d in range(-half + 1, half):
            acc = acc + ext_ref[POOL_HALO + d:POOL_HALO + d + ROW_TILE, cs]
        cnt = (jnp.minimum(t + half, seq_hi) - jnp.maximum(t - half, seq_lo)).astype(jnp.float32)
        diff = acc / cnt - cur_ref[:, cs]
        y = jnp.dot(diff.astype(jnp.bfloat16), w_ref[g], preferred_element_type=jnp.float32)
        o_ref[:, cs] = (y * s_ref[:, cs]).astype(o_ref.dtype)


def pool_mix(p_main, w_pool, pool_scale):
    n = p_main.shape[0]
    hb = ROW_TILE // POOL_HALO
    n_hblk = n // POOL_HALO
    return pl.pallas_call(
        functools.partial(_pool_kernel, n_tok=n),
        grid=(n // ROW_TILE,),
        in_specs=[pl.BlockSpec((POOL_HALO, GROUP_W), lambda i: (jnp.maximum(i * hb - 1, 0), COL_C)),
                  pl.BlockSpec((ROW_TILE, GROUP_W), lambda i: (i, COL_C)),
                  pl.BlockSpec((POOL_HALO, GROUP_W), lambda i: (jnp.minimum((i + 1) * hb, n_hblk - 1), COL_C)),
                  pl.BlockSpec((C_GROUPS, C_GW, C_GW), lambda i: (0, 0, 0)),
                  pl.BlockSpec((1, GROUP_W), lambda i: (0, 0))],
        out_specs=pl.BlockSpec((ROW_TILE, GROUP_W), lambda i: (i, 0)),
        out_shape=jax.ShapeDtypeStruct((n, GROUP_W), jnp.bfloat16),
        scratch_shapes=[pltpu.VMEM((ROW_TILE + 2 * POOL_HALO, GROUP_W), jnp.float32)],
        compiler_params=_cparams("arbitrary"),
        name="pool_mix",
    )(p_main, p_main, p_main, w_pool, pool_scale.reshape(1, GROUP_W))


def chunk_masks(chunk, reverse):
    i = jnp.arange(ROW_TILE)[:, None]
    j = jnp.arange(ROW_TILE)[None, :]
    same = (i // chunk) == (j // chunk)
    return (same & ((j >= i) if reverse else (j <= i))).astype(jnp.bfloat16)


def _split3(x):
    hi = x.astype(jnp.bfloat16)
    r = x - hi.astype(jnp.float32)
    mid = r.astype(jnp.bfloat16)
    lo = (r - mid.astype(jnp.float32)).astype(jnp.bfloat16)
    return hi, mid, lo


def _dot3(m, x):
    hi, mid, lo = _split3(x)
    d = functools.partial(jnp.dot, preferred_element_type=jnp.float32)
    return d(m, hi) + (d(m, mid) + d(m, lo))


def _scan_block(n_blk, reverse):
    if reverse:
        return lambda j: jnp.where(j == 0, 0, n_blk - j)
    return lambda j: j


def _hgrn_head(q_ref, v_ref, z_ref, lb_ref, mask_ref, st, cols, reverse):
    bf16 = jnp.bfloat16
    f32 = jnp.float32
    T, C = ROW_TILE, A_CHUNK
    nt = (((1,), (1,)), ((), ()))
    tn = (((0,), (0,)), ((), ()))
    z = z_ref[:, cols]
    log_lb = lb_ref[0:1, cols]
    log1m_lb = lb_ref[1:2, cols]
    one_m_lb = lb_ref[2:3, cols]
    log_sig = jnp.minimum(z, 0.0) - jnp.log1p(jnp.exp(-jnp.abs(z)))
    bb = log1m_lb + log_sig
    mx = jnp.maximum(log_lb, bb)
    lf = mx + jnp.log1p(jnp.exp(-jnp.abs(log_lb - bb)))
    k = one_m_lb / (1.0 + jnp.exp(z))
    aq = q_ref[:, cols]
    q = aq / (1.0 + jnp.exp(-aq))
    v = v_ref[:, cols].astype(bf16)

    b = _dot3(mask_ref[...], lf)
    last = 0 if reverse else C - 1
    b_end = jnp.concatenate([jnp.broadcast_to(b[c * C + last:c * C + last + 1, :], (C, HEAD_DIM))
                             for c in range(T // C)], axis=0)
    q_dec = (q * jnp.exp(b)).astype(bf16)
    k_inv = (k * jnp.exp(-b)).astype(bf16)
    k_dec = (k * jnp.exp(b_end - b)).astype(bf16)
    dec = jnp.exp(b_end)

    scores = lax.dot_general(q_dec, k_inv, nt, preferred_element_type=f32)
    scores = jnp.where(mask_ref[...] > 0, scores, 0.0)
    o_intra = jnp.dot(scores.astype(bf16), v, preferred_element_type=f32)

    n_c = T // C
    rows = [slice(c * C, (c + 1) * C) for c in range(n_c)]
    deltas = [lax.dot_general(v[r], k_dec[r], tn, preferred_element_type=f32) for r in rows]
    entering = [None] * n_c
    for c in (range(n_c - 1, -1, -1) if reverse else range(n_c)):
        entering[c] = st
        st = st * dec[c * C:c * C + 1, :] + deltas[c]
    outs = [lax.dot_general(q_dec[r], entering[c].astype(bf16), nt, preferred_element_type=f32)
            for c, r in enumerate(rows)]
    return o_intra + jnp.concatenate(outs, axis=0), st


def _hgrn_kernel(q_ref, v_ref, z_ref, lb_ref, mask_ref, *rest, reverse, hp):
    if reverse:
        g_ref, of_ref, gn_ref, o_ref, st_ref = rest
    else:
        o_ref, st_ref = rest

    @pl.when(pl.program_id(1) == 0)
    def _():
        st_ref[...] = jnp.zeros_like(st_ref)

    for hh in range(hp):
        cols = slice(hh * HEAD_DIM, (hh + 1) * HEAD_DIM)
        o, st = _hgrn_head(q_ref, v_ref, z_ref, lb_ref, mask_ref, st_ref[hh], cols, reverse)
        st_ref[hh] = st
        if reverse:
            o = o + of_ref[:, cols]
            o = o * lax.rsqrt(jnp.mean(o * o, axis=-1, keepdims=True) + EPS) * gn_ref[:, cols]
            ag = g_ref[:, cols]
            o_ref[:, cols] = (o * (ag / (1.0 + jnp.exp(-ag)))).astype(o_ref.dtype)
        else:
            o_ref[:, cols] = o


def hgrn_scan(p_main, lb_rows, mask, reverse, o_fwd=None, gn=None, hp=SCAN_HP):
    n = p_main.shape[0]
    n_blk = n // ROW_TILE
    tb = _scan_block(n_blk, reverse)
    w = hp * HEAD_DIM
    gpc = GROUP_W // w
    blk = lambda c: pl.BlockSpec((ROW_TILE, w), lambda h, j: (tb(j), c * gpc + h))
    in_specs = [blk(COL_AQ), blk(COL_AI), blk(COL_AFB if reverse else COL_AFF),
                pl.BlockSpec((3, w), lambda h, j: (0, h)),
                pl.BlockSpec((ROW_TILE, ROW_TILE), lambda h, j: (0, 0))]
    args = [p_main, p_main, p_main, lb_rows, mask]
    if reverse:
        in_specs += [blk(COL_AG), pl.BlockSpec((ROW_TILE, w), lambda h, j: (tb(j), h)),
                     pl.BlockSpec((1, w), lambda h, j: (0, h))]
        args += [p_main, o_fwd, gn.reshape(1, GROUP_W)]
    return pl.pallas_call(
        functools.partial(_hgrn_kernel, reverse=reverse, hp=hp),
        grid=(gpc, n_blk),
        in_specs=in_specs,
        out_specs=pl.BlockSpec((ROW_TILE, w), lambda h, j: (tb(j), h)),
        out_shape=jax.ShapeDtypeStruct((n, GROUP_W), jnp.bfloat16 if reverse else jnp.float32),
        scratch_shapes=[pltpu.VMEM((hp, HEAD_DIM, HEAD_DIM), jnp.float32)],
        compiler_params=_cparams("arbitrary", "arbitrary"),
        name="hgrn_bwd" if reverse else "hgrn_fwd",
    )(*args)


def hgrn_mix(p_main, lb_f, lb_b, gn):
    rows = lambda lb: jnp.stack([jnp.log(lb), jnp.log1p(-lb), 1.0 - lb])
    o_f = hgrn_scan(p_main, rows(lb_f), chunk_masks(A_CHUNK, False), False)
    return hgrn_scan(p_main, rows(lb_b), chunk_masks(A_CHUNK, True), True, o_f, gn)


def hgrn_lower_bounds(p):
    cs = jnp.cumsum(jax.nn.softmax(p.astype(jnp.float32), axis=1), axis=1)
    return cs - cs[:, :1]


def rope_tables(n_tok):
    quarter = HEAD_DIM // 4
    pos = jnp.arange(n_tok - CTX_LEN)
    inv_freq = ROPE_BASE ** (-jnp.arange(quarter, dtype=jnp.float32) / quarter)
    lane = jnp.arange(HEAD_DIM)
    p = jnp.where(lane[None, :] < HEAD_DIM // 2, (pos // GRID_W)[:, None], (pos % GRID_W)[:, None]).astype(jnp.float32)
    ang = p * inv_freq[lane % quarter][None, :]
    sign = jnp.where((lane % (2 * quarter)) < quarter, -1.0, 1.0)
    cos = jnp.concatenate([jnp.ones((CTX_LEN, HEAD_DIM), jnp.float32), jnp.cos(ang)], axis=0)
    sin = jnp.concatenate([jnp.zeros((CTX_LEN, HEAD_DIM), jnp.float32), jnp.sin(ang) * sign], axis=0)
    return cos, sin


def _rope(x, cos, sin):
    quarter = HEAD_DIM // 4
    lane = lax.broadcasted_iota(jnp.int32, x.shape, 1)
    partner = jnp.where((lane % (2 * quarter)) < quarter,
                        pltpu.roll(x, HEAD_DIM - quarter, axis=1), pltpu.roll(x, quarter, axis=1))
    return x * cos + partner * sin


def _mlstm_head(q_ref, k_ref, v_ref, cos, sin, cols, b_bc, li_bc, b_row, li_row, state, reverse):
    bf16 = jnp.bfloat16
    f32 = jnp.float32
    T, C = ROW_TILE, D_CHUNK
    n_c = T // C
    nt = (((1,), (1,)), ((), ()))
    tn = (((0,), (0,)), ((), ()))
    d = functools.partial(jnp.dot, preferred_element_type=f32)
    q = _rope(q_ref[:, cols], cos, sin) * HEAD_DIM ** -0.5
    k = _rope(k_ref[:, cols], cos, sin)
    qb = q.astype(bf16)
    kb = k.astype(bf16)
    vb = v_ref[:, cols].astype(bf16)
    ii = lax.broadcasted_iota(jnp.int32, (C, C), 0)
    jj = lax.broadcasted_iota(jnp.int32, (C, C), 1)
    causal = (jj >= ii) if reverse else (jj <= ii)
    last = 0 if reverse else C - 1
    rows = [slice(c * C, (c + 1) * C) for c in range(n_c)]
    bends, m_locs, d_ss, d_ns = [], [], [], []
    for c, r in enumerate(rows):
        bend = b_bc[c * C + last:c * C + last + 1]
        gcol = bend - b_bc[r] + li_bc[r]
        m_loc = jnp.max(gcol, axis=0, keepdims=True)
        kw = k[r] * jnp.exp(gcol - m_loc)
        bends.append(bend)
        m_locs.append(m_loc)
        d_ss.append(lax.dot_general(kw.astype(bf16), vb[r], tn, preferred_element_type=f32))
        d_ns.append(jnp.sum(kw, axis=0, keepdims=True))
    s, nv, m = state
    entering = [None] * n_c
    for c in (range(n_c - 1, -1, -1) if reverse else range(n_c)):
        entering[c] = (s, nv, m)
        m_new = jnp.maximum(bends[c] + m, m_locs[c])
        a = jnp.exp(bends[c] + m - m_new)
        cc = jnp.exp(m_locs[c] - m_new)
        s = a * s + cc * d_ss[c]
        nv = a * nv + cc * d_ns[c]
        m = m_new
    outs = []
    for c, r in enumerate(rows):
        s_in, n_in, m_in = entering[c]
        dmat = jnp.where(causal, b_bc[r][:, 0:C] - b_row[:, r] + li_row[:, r], -jnp.inf)
        m_inter = b_bc[r] + m_in
        m_t = jnp.maximum(m_inter, jnp.max(dmat, axis=-1, keepdims=True))
        w = jnp.exp(dmat - m_t[:, 0:C]) * lax.dot_general(qb[r], kb[r], nt, preferred_element_type=f32)
        a_col = jnp.exp(m_inter - m_t)
        num = d(w.astype(bf16), vb[r]) + a_col * d(qb[r], s_in.astype(bf16))
        den = jnp.sum(w, axis=-1, keepdims=True) + a_col * jnp.sum(q[r] * n_in, axis=-1, keepdims=True)
        outs.append(num / jnp.maximum(jnp.abs(den), jnp.exp(-m_t)))
    return jnp.concatenate(outs, axis=0), (s, nv, m)


def _mlstm_kernel(q_ref, k_ref, v_ref, g_ref, cos_ref, sin_ref, mask_ref, maskt_ref, *rest, reverse, hp):
    if reverse:
        do_ref, of_ref, gn_ref, o_ref, s_ref, n_ref, m_ref, cumt_ref, lit_ref = rest
    else:
        o_ref, s_ref, n_ref, m_ref, cumt_ref, lit_ref = rest
    hg = pl.program_id(0)
    f32 = jnp.float32
    bf16 = jnp.bfloat16

    @pl.when(pl.program_id(1) == 0)
    def _():
        s_ref[...] = jnp.zeros_like(s_ref)
        n_ref[...] = jnp.zeros_like(n_ref)
        m_ref[...] = jnp.zeros_like(m_ref)

    g = g_ref[...]
    lsg = jnp.minimum(g, 0.0) - jnp.log1p(jnp.exp(-jnp.abs(g)))
    cum = _dot3(mask_ref[...], lsg)
    hi, mid, lo = _split3(lsg.T)
    d = functools.partial(jnp.dot, preferred_element_type=f32)
    cumt_ref[...] = d(hi, maskt_ref[...]) + (d(mid, maskt_ref[...]) + d(lo, maskt_ref[...]))
    lit_ref[...] = g.T
    cum_parts = _split3(cum)
    g_parts = _split3(g)
    sel_row = lax.broadcasted_iota(jnp.int32, (LANES, LANES), 0)

    def replicate(parts, col):
        sel = (sel_row == col).astype(bf16)
        return d(parts[0], sel) + (d(parts[1], sel) + d(parts[2], sel))

    cos = cos_ref[...]
    sin = sin_ref[...]
    for hh in range(hp):
        h = hg * hp + hh
        col_i = (GATE_I_B if reverse else GATE_I_F) * N_HEADS + h
        col_f = (GATE_F_B if reverse else GATE_F_F) * N_HEADS + h
        b_bc = replicate(cum_parts, col_f)
        li_bc = replicate(g_parts, col_i)
        b_row = cumt_ref[pl.ds(col_f, 1), :]
        li_row = lit_ref[pl.ds(col_i, 1), :]
        cols = slice(hh * HEAD_DIM, (hh + 1) * HEAD_DIM)
        state = (s_ref[hh], n_ref[hh, 0:1, :], m_ref[hh, 0:1, :])
        o, (s, nv, m) = _mlstm_head(q_ref, k_ref, v_ref, cos, sin, cols, b_bc, li_bc, b_row, li_row, state, reverse)
        s_ref[hh] = s
        n_ref[hh] = jnp.broadcast_to(nv, n_ref.shape[1:])
        m_ref[hh] = jnp.broadcast_to(m, m_ref.shape[1:])
        if reverse:
            o = o + of_ref[:, cols]
            o = o * lax.rsqrt(jnp.mean(o * o, axis=-1, keepdims=True) + EPS) * gn_ref[:, cols]
            o_ref[:, cols] = (o / (1.0 + jnp.exp(-do_ref[:, cols]))).astype(o_ref.dtype)
        else:
            o_ref[:, cols] = o


def mlstm_scan(p_main, p_gate, cos, sin, mask, reverse, o_fwd=None, gn=None, hp=SCAN_HP):
    n = p_main.shape[0]
    n_blk = n // ROW_TILE
    tb = _scan_block(n_blk, reverse)
    w = hp * HEAD_DIM
    gpc = GROUP_W // w
    blk = lambda c: pl.BlockSpec((ROW_TILE, w), lambda h, j: (tb(j), c * gpc + h))
    tok = pl.BlockSpec((ROW_TILE, LANES), lambda h, j: (tb(j), 0))
    sq = pl.BlockSpec((ROW_TILE, ROW_TILE), lambda h, j: (0, 0))
    in_specs = [blk(COL_DQ), blk(COL_DK), blk(COL_DV), tok, tok, tok, sq, sq]
    args = [p_main, p_main, p_main, p_gate, cos, sin, mask, mask.T]
    if reverse:
        in_specs += [blk(COL_DO), pl.BlockSpec((ROW_TILE, w), lambda h, j: (tb(j), h)),
                     pl.BlockSpec((1, w), lambda h, j: (0, h))]
        args += [p_main, o_fwd, gn.reshape(1, GROUP_W)]
    return pl.pallas_call(
        functools.partial(_mlstm_kernel, reverse=reverse, hp=hp),
        grid=(gpc, n_blk),
        in_specs=in_specs,
        out_specs=pl.BlockSpec((ROW_TILE, w), lambda h, j: (tb(j), h)),
        out_shape=jax.ShapeDtypeStruct((n, GROUP_W), jnp.bfloat16 if reverse else jnp.float32),
        scratch_shapes=[pltpu.VMEM((hp, HEAD_DIM, HEAD_DIM), jnp.float32), pltpu.VMEM((hp, 8, HEAD_DIM), jnp.float32),
                        pltpu.VMEM((hp, 8, LANES), jnp.float32), pltpu.VMEM((LANES, ROW_TILE), jnp.float32),
                        pltpu.VMEM((LANES, ROW_TILE), jnp.float32)],
        compiler_params=_cparams("arbitrary", "arbitrary"),
        name="mlstm_bwd" if reverse else "mlstm_fwd",
    )(*args)


def mlstm_mix(p_main, p_gate, gn):
    cos, sin = rope_tables(p_main.shape[0])
    o_f = mlstm_scan(p_main, p_gate, cos, sin, chunk_masks(D_CHUNK, False), False)
    return mlstm_scan(p_main, p_gate, cos, sin, chunk_masks(D_CHUNK, True), True, o_f, gn)


def kernel(x, c, ctx, c_ctx, ada_w, ada_b, norm1_w, norm2_w, mix_w_in, mix_b_in, hgrn_lb, hgrn_gn, natten_rpb,
           pool_w, pool_scale, mlstm_gn, mix_w_out, moe_wg, moe_bg, moe_we, moe_be, moe_w1, moe_w3, moe_w2, final_w):
    assert x.shape == (1, SEQ, D_MODEL) and ctx.shape == (1, CTX_LEN, D_MODEL)
    bf16 = jnp.bfloat16
    lbs = hgrn_lower_bounds(hgrn_lb)

    cond = jnp.zeros((16, D_MODEL), jnp.float32).at[0].set(jax.nn.silu(c_ctx)).at[1].set(jax.nn.silu(c[0]))
    mods = ada_modulation(cond.astype(bf16), ada_w, ada_b)[:, :2].reshape(DEPTH, 2, 6, D_MODEL)

    w_in_t = jnp.swapaxes(mix_w_in, 1, 2)
    w_in = cast_bf16_transposed(w_in_t, N_MAIN)
    w_out = cast_bf16(mix_w_out, 4 * LANES)

    xs = jnp.concatenate([ctx[0], x[0]], axis=0)
    for l in range(DEPTH):
        mod = mods[l]
        h = norm_modulate(xs, norm1_w[l], mod, 0)
        p_main, w1, w3, w2 = in_proj_cast(h, w_in, mix_b_in[l], l, moe_w1, moe_w3, moe_w2)
        w_gate_t = jnp.pad(w_in_t[l, N_MAIN:, :], ((0, LANES - N_GATE), (0, 0)))
        b_gate = jnp.pad(mix_b_in[l, N_MAIN:], (0, LANES - N_GATE))
        p_gate = matmul_bias_nt(h, w_gate_t, b_gate, MM_TM)

        out_a = hgrn_mix(p_main, lbs[0, l], lbs[1, l], hgrn_gn[l])
        out_b = natten(p_main, natten_bias_tiles(natten_rpb[l]))
        out_c = pool_mix(p_main, pool_w[l].astype(bf16), pool_scale[l])
        out_d = mlstm_mix(p_main, p_gate, mlstm_gn[l])
        xs = matmul_gated_residual((out_a, out_b, out_c, out_d), w_out, l, xs, mod, 2)

        w_router = jnp.pad(jnp.concatenate([moe_wg[l], moe_we[l]], axis=1),
                           ((0, 0), (0, LANES - N_GROUPS - N_EXPERTS)))
        h2, logits = norm_modulate_router(xs, norm2_w[l], mod, w_router)
        eid, top_w = moe_route(logits, moe_bg[l], moe_be[l])
        tile_expert, tile_valid, row_token, cpos = moe_schedule(eid)
        ys = moe_grouped(h2, tile_expert, tile_valid, row_token, w1, w3, w2)
        xs = moe_combine(cpos, ys, xs, top_w, mod, 5)

    return norm_modulate(xs, final_w, mods[0], None, out_dtype=jnp.float32, skip_rows=CTX_LEN)[None]
```

```python
import functools

import jax
import jax.numpy as jnp
from jax import lax
from jax.experimental import pallas as pl
from jax.experimental.pallas import tpu as pltpu

D_MODEL = 4096
SEQ = 8192
DEPTH = 2
GRID_W = 64
CTX_LEN = 256
N_TOK = CTX_LEN + SEQ
GROUP_W = 1024
HEAD_DIM = 128
N_HEADS = GROUP_W // HEAD_DIM
C_GROUPS = 4
C_GW = GROUP_W // C_GROUPS
POOL_WINDOWS = (2, 4, 8, 16)
NA_MAX_ROWS = 8
NA_COLS = 16
A_CHUNK = 32
D_CHUNK = 64
ROPE_BASE = 10000.0
N_GROUPS = 4
EXP_PER_GROUP = 8
N_EXPERTS = N_GROUPS * EXP_PER_GROUP
TOP_K = 2
D_FF = D_MODEL // 8
EPS = 1e-6
N_MAIN = 13 * GROUP_W
N_GATE = 4 * N_HEADS
LANES = 128
VMEM_LIMIT = 56 * 1024 * 1024

ROW_TILE = 256
MM_TM = 768
MM_TN = 512
MOE_TM = 256
MOE_TILES = (TOP_K * N_TOK + N_EXPERTS * (MOE_TM - 1)) // MOE_TM + 1
CMB_TB = 128
SCAN_HP = 8
DMA_UNROLL = 8
CAST_SLABS = 8
CT_TN, CT_TK = 1024, 1024
MOE_AHEAD = 2
MOE_SLOTS = MOE_AHEAD + 1

COL_AQ, COL_AI, COL_AG, COL_AFF, COL_AFB = 0, 1, 2, 3, 4
COL_Q, COL_K, COL_V, COL_C = 5, 6, 7, 8
COL_DQ, COL_DK, COL_DV, COL_DO = 9, 10, 11, 12
GATE_I_F, GATE_F_F, GATE_I_B, GATE_F_B = 0, 1, 2, 3
HPG = GROUP_W // HEAD_DIM


def _cparams(*sem):
    return pltpu.CompilerParams(dimension_semantics=sem, vmem_limit_bytes=VMEM_LIMIT)


def _ada_kernel(c_ref, w_ref, b_ref, o_ref):
    acc = jnp.dot(c_ref[...], w_ref[...].astype(jnp.bfloat16), preferred_element_type=jnp.float32)
    o_ref[...] = acc + b_ref[...]


def ada_modulation(cond, ada_w, ada_b):
    tn = 512
    n = ada_w.shape[-1]
    rows = cond.shape[0]
    return pl.pallas_call(
        _ada_kernel,
        grid=(DEPTH, n // tn),
        in_specs=[pl.BlockSpec((rows, D_MODEL), lambda l, j: (0, 0)),
                  pl.BlockSpec((None, D_MODEL, tn), lambda l, j: (l, 0, j)),
                  pl.BlockSpec((None, 1, tn), lambda l, j: (l, 0, j))],
        out_specs=pl.BlockSpec((None, rows, tn), lambda l, j: (l, 0, j)),
        out_shape=jax.ShapeDtypeStruct((DEPTH, rows, n), jnp.float32),
        compiler_params=_cparams("arbitrary", "arbitrary"),
        name="ada_modulation",
    )(cond, ada_w, ada_b.reshape(DEPTH, 1, n))


def _norm_kernel(x_ref, w_ref, mod_ref, o_ref, *, shift_row, scale_row):
    x = x_ref[...]
    y = x * lax.rsqrt(jnp.mean(x * x, axis=-1, keepdims=True) + EPS) * w_ref[...]
    if shift_row is not None:
        y = y * (1.0 + mod_ref[scale_row:scale_row + 1, :]) + mod_ref[shift_row:shift_row + 1, :]
    o_ref[...] = y.astype(o_ref.dtype)


def _norm_router_kernel(x_ref, w_ref, mod_ref, wr_ref, o_ref, lg_ref, *, shift_row, scale_row):
    x = x_ref[...]
    y = x * lax.rsqrt(jnp.mean(x * x, axis=-1, keepdims=True) + EPS) * w_ref[...]
    y = y * (1.0 + mod_ref[scale_row:scale_row + 1, :]) + mod_ref[shift_row:shift_row + 1, :]
    o_ref[...] = y.astype(o_ref.dtype)
    y_hi = y.astype(jnp.bfloat16)
    y_lo = (y - y_hi.astype(jnp.float32)).astype(jnp.bfloat16)
    w = wr_ref[...]
    w_hi = w.astype(jnp.bfloat16)
    w_lo = (w - w_hi.astype(jnp.float32)).astype(jnp.bfloat16)
    dot = functools.partial(jnp.dot, preferred_element_type=jnp.float32)
    lg_ref[...] = dot(y_hi, w_hi) + (dot(y_lo, w_hi) + dot(y_hi, w_lo))


def _tok_type(i):
    return jnp.where(i * ROW_TILE >= CTX_LEN, 1, 0)


def norm_modulate(x, w, mod, which, out_dtype=jnp.bfloat16, skip_rows=0):
    n = x.shape[0] - skip_rows
    off = skip_rows // ROW_TILE
    rows = (None, None) if which is None else (3 * which, 3 * which + 1)
    return pl.pallas_call(
        functools.partial(_norm_kernel, shift_row=rows[0], scale_row=rows[1]),
        grid=(n // ROW_TILE,),
        in_specs=[pl.BlockSpec((ROW_TILE, D_MODEL), lambda i: (i + off, 0)),
                  pl.BlockSpec((1, D_MODEL), lambda i: (0, 0)),
                  pl.BlockSpec((None, 6, D_MODEL), lambda i: (_tok_type(i + off), 0, 0))],
        out_specs=pl.BlockSpec((ROW_TILE, D_MODEL), lambda i: (i, 0)),
        out_shape=jax.ShapeDtypeStruct((n, D_MODEL), out_dtype),
        compiler_params=_cparams("arbitrary"),
        name="norm_modulate",
    )(x, w.reshape(1, D_MODEL), mod)


def norm_modulate_router(x, w, mod, w_router):
    n = x.shape[0]
    return pl.pallas_call(
        functools.partial(_norm_router_kernel, shift_row=3, scale_row=4),
        grid=(n // ROW_TILE,),
        in_specs=[pl.BlockSpec((ROW_TILE, D_MODEL), lambda i: (i, 0)),
                  pl.BlockSpec((1, D_MODEL), lambda i: (0, 0)),
                  pl.BlockSpec((None, 6, D_MODEL), lambda i: (_tok_type(i), 0, 0)),
                  pl.BlockSpec((D_MODEL, LANES), lambda i: (0, 0))],
        out_specs=[pl.BlockSpec((ROW_TILE, D_MODEL), lambda i: (i, 0)),
                   pl.BlockSpec((ROW_TILE, LANES), lambda i: (i, 0))],
        out_shape=[jax.ShapeDtypeStruct((n, D_MODEL), jnp.float32),
                   jax.ShapeDtypeStruct((n, LANES), jnp.float32)],
        compiler_params=_cparams("arbitrary"),
        name="norm_modulate_router",
    )(x, w.reshape(1, D_MODEL), mod, w_router)


def _mm_bias_nt_kernel(a_ref, bt_ref, bias_ref, o_ref):
    bt = bt_ref[...].astype(jnp.bfloat16)
    acc = lax.dot_general(a_ref[...], bt, (((1,), (1,)), ((), ())), preferred_element_type=jnp.float32)
    o_ref[...] = acc + bias_ref[...]


def _mm_resid_kernel(a0_ref, a1_ref, a2_ref, a3_ref, b_ref, x_ref, mod_ref, o_ref, *, gate_row):
    acc = None
    for g, a_ref in enumerate((a0_ref, a1_ref, a2_ref, a3_ref)):
        part = jnp.dot(a_ref[...], b_ref[g * GROUP_W:(g + 1) * GROUP_W, :], preferred_element_type=jnp.float32)
        acc = part if acc is None else acc + part
    tm = x_ref.shape[0]
    row = pl.program_id(0) * tm + lax.broadcasted_iota(jnp.int32, (tm, 1), 0)
    gate = jnp.where(row < CTX_LEN, mod_ref[0, gate_row:gate_row + 1, :], mod_ref[1, gate_row:gate_row + 1, :])
    o_ref[...] = x_ref[...] + gate * acc


def matmul_bias_nt(a, bt, bias, tm):
    m, k = a.shape
    n = bt.shape[0]
    return pl.pallas_call(
        _mm_bias_nt_kernel,
        grid=(m // tm,),
        in_specs=[pl.BlockSpec((tm, k), lambda i: (i, 0)),
                  pl.BlockSpec((n, k), lambda i: (0, 0)),
                  pl.BlockSpec((1, n), lambda i: (0, 0))],
        out_specs=pl.BlockSpec((tm, n), lambda i: (i, 0)),
        out_shape=jax.ShapeDtypeStruct((m, n), jnp.float32),
        compiler_params=_cparams("arbitrary"),
        name="matmul_bias_nt",
    )(a, bt, bias.reshape(1, n))


def matmul_gated_residual(parts, b, layer, x, mod, gate_row):
    m, n = x.shape
    tm, tn = MM_TM, MM_TN
    a_spec = pl.BlockSpec((tm, GROUP_W), lambda i, j: (i, 0))
    return pl.pallas_call(
        functools.partial(_mm_resid_kernel, gate_row=gate_row),
        grid=(m // tm, n // tn),
        in_specs=[a_spec, a_spec, a_spec, a_spec,
                  pl.BlockSpec((None, b.shape[1], tn), lambda i, j: (layer, 0, j)),
                  pl.BlockSpec((tm, tn), lambda i, j: (i, j)),
                  pl.BlockSpec((2, 6, tn), lambda i, j: (0, 0, j))],
        out_specs=pl.BlockSpec((tm, tn), lambda i, j: (i, j)),
        out_shape=jax.ShapeDtypeStruct((m, n), jnp.float32),
        compiler_params=_cparams("arbitrary", "arbitrary"),
        name="matmul_gated_residual",
    )(*parts, b, x, mod)


def _cast_kernel(x_ref, o_ref):
    o_ref[...] = x_ref[...].astype(o_ref.dtype)


def cast_bf16(w, rows):
    nl, r, c = w.shape
    spec = pl.BlockSpec((None, rows, c), lambda l, i: (l, i, 0))
    return pl.pallas_call(
        _cast_kernel,
        grid=(nl, r // rows),
        in_specs=[spec],
        out_specs=spec,
        out_shape=jax.ShapeDtypeStruct(w.shape, jnp.bfloat16),
        compiler_params=_cparams("arbitrary", "arbitrary"),
        name="cast_bf16",
    )(w)


def _cast_t_kernel(x_ref, o_ref):
    o_ref[...] = x_ref[...].T.astype(o_ref.dtype)


def cast_bf16_transposed(wt, n_cols):
    nl, _, k = wt.shape
    return pl.pallas_call(
        _cast_t_kernel,
        grid=(nl, k // CT_TK, n_cols // CT_TN),
        in_specs=[pl.BlockSpec((None, CT_TN, CT_TK), lambda l, i, j: (l, j, i))],
        out_specs=pl.BlockSpec((None, CT_TK, CT_TN), lambda l, i, j: (l, i, j)),
        out_shape=jax.ShapeDtypeStruct((nl, k, n_cols), jnp.bfloat16),
        compiler_params=_cparams("arbitrary", "arbitrary", "arbitrary"),
        name="cast_bf16_transposed",
    )(wt)


def _in_proj_cast_kernel(a_ref, b_ref, bias_ref, w1_ref, w3_ref, w2_ref, o_ref, w1o_ref, w3o_ref, w2o_ref):
    o_ref[...] = jnp.dot(a_ref[...], b_ref[...], preferred_element_type=jnp.float32) + bias_ref[...]
    w1o_ref[...] = w1_ref[...].astype(w1o_ref.dtype)
    w3o_ref[...] = w3_ref[...].astype(w3o_ref.dtype)
    w2o_ref[...] = w2_ref[...].astype(w2o_ref.dtype)


def in_proj_cast(a, b, bias, layer, w1, w3, w2):
    m, k = a.shape
    n, tm, tn = N_MAIN, MM_TM, MM_TN
    n_j = n // tn
    n_slabs = N_EXPERTS * CAST_SLABS
    assert (m // tm) * n_j >= n_slabs
    r13, r2 = D_MODEL // CAST_SLABS, D_FF // CAST_SLABS

    def slab(i, j):
        s = jnp.minimum(i * n_j + j, n_slabs - 1)
        return s // CAST_SLABS, s % CAST_SLABS

    in13 = pl.BlockSpec((None, None, r13, D_FF), lambda i, j: (layer, *slab(i, j), 0))
    in2 = pl.BlockSpec((None, None, r2, D_MODEL), lambda i, j: (layer, *slab(i, j), 0))
    out13 = pl.BlockSpec((None, r13, D_FF), lambda i, j: (*slab(i, j), 0))
    out2 = pl.BlockSpec((None, r2, D_MODEL), lambda i, j: (*slab(i, j), 0))
    bf16 = jnp.bfloat16
    return pl.pallas_call(
        _in_proj_cast_kernel,
        grid=(m // tm, n_j),
        in_specs=[pl.BlockSpec((tm, k), lambda i, j: (i, 0)),
                  pl.BlockSpec((None, k, tn), lambda i, j: (layer, 0, j)),
                  pl.BlockSpec((1, tn), lambda i, j: (0, j)),
                  in13, in13, in2],
        out_specs=[pl.BlockSpec((tm, tn), lambda i, j: (i, j)), out13, out13, out2],
        out_shape=[jax.ShapeDtypeStruct((m, n), jnp.float32),
                   jax.ShapeDtypeStruct((N_EXPERTS, D_MODEL, D_FF), bf16),
                   jax.ShapeDtypeStruct((N_EXPERTS, D_MODEL, D_FF), bf16),
                   jax.ShapeDtypeStruct((N_EXPERTS, D_FF, D_MODEL), bf16)],
        compiler_params=_cparams("arbitrary", "arbitrary"),
        name="in_proj_cast",
    )(a, b, bias.reshape(1, -1), w1, w3, w2)


def _moe_kernel(te_ref, tv_ref, rt_ref, h_hbm, w1_ref, w3_ref, w2_ref, y_ref, xg_ref, sem_ref):
    del te_ref
    i = pl.program_id(0)
    n_tiles = pl.num_programs(0)

    def row_copy(tile, slot, r):
        tok = rt_ref[tile * MOE_TM + r]
        return pltpu.make_async_copy(h_hbm.at[pl.ds(tok, 1), :], xg_ref.at[slot, pl.ds(r, 1), :], sem_ref.at[slot])

    def start_gather(tile, slot):
        @pl.when(tv_ref[tile] != 0)
        def _():
            def body(r, carry):
                row_copy(tile, slot, r).start()
                return carry
            lax.fori_loop(0, MOE_TM, body, 0, unroll=DMA_UNROLL)

    @pl.when(i == 0)
    def _():
        for t in range(MOE_AHEAD):
            start_gather(t, t)

    @pl.when(i + MOE_AHEAD < n_tiles)
    def _():
        start_gather(i + MOE_AHEAD, (i + MOE_AHEAD) % MOE_SLOTS)

    @pl.when(tv_ref[i] != 0)
    def _():
        slot = i % MOE_SLOTS

        def wait_body(r, carry):
            row_copy(i, slot, r).wait()
            return carry
        lax.fori_loop(0, MOE_TM, wait_body, 0, unroll=DMA_UNROLL)
        xg = xg_ref[slot].astype(jnp.bfloat16)
        a = jnp.dot(xg, w1_ref[...], preferred_element_type=jnp.float32)
        b = jnp.dot(xg, w3_ref[...], preferred_element_type=jnp.float32)
        act = (a * jax.nn.sigmoid(a)) * b
        y_ref[...] = jnp.dot(act.astype(jnp.bfloat16), w2_ref[...], preferred_element_type=jnp.float32)

    @pl.when(tv_ref[i] == 0)
    def _():
        y_ref[...] = jnp.zeros_like(y_ref)


def moe_grouped(h, tile_expert, tile_valid, row_token, w1, w3, w2):
    grid_spec = pltpu.PrefetchScalarGridSpec(
        num_scalar_prefetch=3,
        grid=(MOE_TILES,),
        in_specs=[pl.BlockSpec(memory_space=pl.ANY),
                  pl.BlockSpec((None, D_MODEL, D_FF), lambda i, te, tv, rt: (te[i], 0, 0)),
                  pl.BlockSpec((None, D_MODEL, D_FF), lambda i, te, tv, rt: (te[i], 0, 0)),
                  pl.BlockSpec((None, D_FF, D_MODEL), lambda i, te, tv, rt: (te[i], 0, 0))],
        out_specs=pl.BlockSpec((MOE_TM, D_MODEL), lambda i, te, tv, rt: (i, 0)),
        scratch_shapes=[pltpu.VMEM((MOE_SLOTS, MOE_TM, D_MODEL), jnp.float32),
                        pltpu.SemaphoreType.DMA((MOE_SLOTS,))],
    )
    return pl.pallas_call(
        _moe_kernel,
        grid_spec=grid_spec,
        out_shape=jax.ShapeDtypeStruct((MOE_TILES * MOE_TM, D_MODEL), jnp.float32),
        compiler_params=_cparams("arbitrary"),
        name="moe_grouped",
    )(tile_expert, tile_valid, row_token, h, w1, w3, w2)


def _combine_kernel(pos_ref, y_hbm, x_ref, w_ref, mod_ref, o_ref, buf_ref, sem_ref, *, gate_row):
    i = pl.program_id(0)
    n_steps = pl.num_programs(0)

    def row_copy(step, slot, r):
        src = pos_ref[step * (TOP_K * CMB_TB) + r]
        return pltpu.make_async_copy(y_hbm.at[pl.ds(src, 1), :], buf_ref.at[slot, pl.ds(r, 1), :], sem_ref.at[slot])

    def start_gather(step, slot):
        def body(r, carry):
            row_copy(step, slot, r).start()
            return carry
        lax.fori_loop(0, TOP_K * CMB_TB, body, 0, unroll=DMA_UNROLL)

    @pl.when(i == 0)
    def _():
        start_gather(0, 0)

    @pl.when(i + 1 < n_steps)
    def _():
        start_gather(i + 1, (i + 1) % 2)

    slot = i % 2

    def wait_body(r, carry):
        row_copy(i, slot, r).wait()
        return carry
    lax.fori_loop(0, TOP_K * CMB_TB, wait_body, 0, unroll=DMA_UNROLL)
    w = w_ref[...]
    moe = w[:, 0:1] * buf_ref[slot, 0:CMB_TB, :] + w[:, 1:2] * buf_ref[slot, CMB_TB:2 * CMB_TB, :]
    o_ref[...] = x_ref[...] + mod_ref[gate_row:gate_row + 1, :] * moe


def moe_combine(pos, ys, x, top_w, mod, gate_row):
    n = x.shape[0]
    grid_spec = pltpu.PrefetchScalarGridSpec(
        num_scalar_prefetch=1,
        grid=(n // CMB_TB,),
        in_specs=[pl.BlockSpec(memory_space=pl.ANY),
                  pl.BlockSpec((CMB_TB, D_MODEL), lambda i, pos: (i, 0)),
                  pl.BlockSpec((CMB_TB, TOP_K), lambda i, pos: (i, 0)),
                  pl.BlockSpec((None, 6, D_MODEL), lambda i, pos: (jnp.where(i * CMB_TB >= CTX_LEN, 1, 0), 0, 0))],
        out_specs=pl.BlockSpec((CMB_TB, D_MODEL), lambda i, pos: (i, 0)),
        scratch_shapes=[pltpu.VMEM((2, TOP_K * CMB_TB, D_MODEL), jnp.float32), pltpu.SemaphoreType.DMA((2,))],
    )
    return pl.pallas_call(
        functools.partial(_combine_kernel, gate_row=gate_row),
        grid_spec=grid_spec,
        out_shape=jax.ShapeDtypeStruct((n, D_MODEL), jnp.float32),
        compiler_params=_cparams("arbitrary"),
        name="moe_combine",
    )(pos, ys, x, top_w, mod)


def moe_route(logits, bg, be):
    n = logits.shape[0]
    g_logits = logits[:, :N_GROUPS] + bg
    grp = jnp.argmax(g_logits, axis=-1)
    g_sel = grp[:, None] == jnp.arange(N_GROUPS)[None, :]
    p_grp = jnp.sum(jnp.where(g_sel, jax.nn.softmax(g_logits, axis=-1), 0.0), axis=-1, keepdims=True)
    e_logits = (logits[:, N_GROUPS:N_GROUPS + N_EXPERTS] + be).reshape(n, N_GROUPS, EXP_PER_GROUP)
    e_in = jnp.sum(jnp.where(g_sel[:, :, None], e_logits, 0.0), axis=1)
    lane = jnp.arange(EXP_PER_GROUP)[None, :]
    i1 = jnp.argmax(e_in, axis=-1)
    v1 = jnp.max(e_in, axis=-1)
    rest = jnp.where(lane == i1[:, None], -jnp.inf, e_in)
    i2 = jnp.argmax(rest, axis=-1)
    v2 = jnp.max(rest, axis=-1)
    top_w = jax.nn.softmax(jnp.stack([v1, v2], axis=-1), axis=-1) * p_grp
    eid = grp[:, None] * EXP_PER_GROUP + jnp.stack([i1, i2], axis=-1)
    return eid.astype(jnp.int32), top_w


def moe_schedule(eid):
    n = eid.shape[0]
    flat_e = eid.reshape(-1)
    onehot = (flat_e[:, None] == jnp.arange(N_EXPERTS)[None, :]).astype(jnp.int32)
    rank = jnp.sum(onehot * (jnp.cumsum(onehot, axis=0) - 1), axis=1)
    counts = jnp.sum(onehot, axis=0)
    padded = ((counts + MOE_TM - 1) // MOE_TM) * MOE_TM
    pad_end = jnp.cumsum(padded)
    pad_start = pad_end - padded
    dest = (jnp.sum(onehot * pad_start[None, :], axis=1) + rank).astype(jnp.int32)
    n_rows = MOE_TILES * MOE_TM
    row_token = jnp.zeros((n_rows,), jnp.int32).at[dest].set(jnp.arange(TOP_K * n, dtype=jnp.int32) // TOP_K)
    tile_row0 = jnp.arange(MOE_TILES, dtype=jnp.int32) * MOE_TM
    n_valid = pad_end[-1] // MOE_TM
    tile_valid = (tile_row0 < pad_end[-1]).astype(jnp.int32)
    tile_expert = jnp.sum((tile_row0[:, None] >= pad_end[None, :]).astype(jnp.int32), axis=1)
    last_expert = jnp.sum(jnp.where(jnp.arange(MOE_TILES) == n_valid - 1, tile_expert, 0))
    tile_expert = jnp.where(tile_valid != 0, tile_expert, last_expert).astype(jnp.int32)
    pos = dest.reshape(n // CMB_TB, CMB_TB, TOP_K).transpose(0, 2, 1).reshape(-1)
    return tile_expert, tile_valid, row_token, pos


NA_QR = 4
NA_KR = NA_QR + NA_MAX_ROWS - 1
NA_TQ = NA_QR * GRID_W
NA_TK = NA_KR * GRID_W
NA_HP = 8
NA_WIN_BLOCKS = 3


def natten_bias_tiles(rpb):
    hi = lax.Precision.HIGHEST
    cols = jnp.arange(GRID_W)
    dc = jnp.clip(cols[None, :] - cols[:, None] + (NA_COLS - 1), 0, 2 * NA_COLS - 2)
    c0 = jnp.clip(cols - NA_COLS // 2, 0, GRID_W - NA_COLS)
    in_win = (cols[None, :] >= c0[:, None]) & (cols[None, :] < c0[:, None] + NA_COLS)
    oh_c = (dc[..., None] == jnp.arange(2 * NA_COLS - 1)).astype(jnp.float32)
    toep = jnp.einsum('qkc,hdc->hdqk', oh_c, rpb.astype(jnp.float32), precision=hi)
    rl = jnp.arange(NA_QR)[:, None]
    kl = jnp.arange(NA_KR)[None, :]
    half = NA_MAX_ROWS // 2
    pats = [(kl - rl + NA_MAX_ROWS - 1, (kl < NA_MAX_ROWS) & (rl >= 0)),
            (kl - rl + NA_MAX_ROWS - 1 - half, (kl - rl >= 0) & (kl - rl < NA_MAX_ROWS)),
            (kl - rl, (kl >= NA_KR - NA_MAX_ROWS) & (rl >= 0))]
    tiles = []
    for dr, valid in pats:
        oh_r = ((dr[..., None] == jnp.arange(2 * NA_MAX_ROWS - 1)) & valid[..., None]).astype(jnp.float32)
        t = jnp.einsum('rkd,hdqc->hrqkc', oh_r, toep, precision=hi)
        ok = valid[None, :, None, :, None] & in_win[None, None, :, None, :]
        tiles.append(jnp.where(ok, t, -jnp.inf).reshape(rpb.shape[0], NA_TQ, NA_TK))
    return jnp.stack(tiles, axis=1)


def _natten_kernel(q_ref, kc_ref, vc_ref, k0_ref, k1_ref, k2_ref, v0_ref, v1_ref, v2_ref, bias_ref, o_ref,
                   kw_ref, vw_ref, *, n_rows, n_blk, hp):
    j = pl.program_id(1)
    bf16 = jnp.bfloat16
    f32 = jnp.float32
    nt = (((1,), (1,)), ((), ()))
    heads = [slice(hh * HEAD_DIM, (hh + 1) * HEAD_DIM) for hh in range(hp)]

    def ctx_scores(cols):
        q = (q_ref[:, cols] * HEAD_DIM ** -0.5).astype(bf16)
        kc = kc_ref[:, cols].astype(bf16)
        return q, lax.dot_general(q, kc, nt, preferred_element_type=f32)

    @pl.when(j == 0)
    def _():
        for cols in heads:
            _, s_ctx = ctx_scores(cols)
            m = jnp.max(s_ctx, axis=-1, keepdims=True)
            p = jnp.exp(s_ctx - m)
            l = jnp.sum(p, axis=-1, keepdims=True)
            o = jnp.dot(p.astype(bf16), vc_ref[:, cols].astype(bf16), preferred_element_type=f32)
            o_ref[:, cols] = (o / l).astype(o_ref.dtype)

    @pl.when(j > 0)
    def _():
        for t, (k_ref, v_ref) in enumerate(((k0_ref, v0_ref), (k1_ref, v1_ref), (k2_ref, v2_ref))):
            kw_ref[t * ROW_TILE:(t + 1) * ROW_TILE, :] = k_ref[...].astype(bf16)
            vw_ref[t * ROW_TILE:(t + 1) * ROW_TILE, :] = v_ref[...].astype(bf16)
        r0 = (j - 1) * NA_QR
        kbase = jnp.clip(r0 - NA_MAX_ROWS // 2, 0, n_rows - NA_KR)
        first_blk = jnp.clip(j - 1, 1, n_blk - NA_WIN_BLOCKS)
        off = pl.multiple_of(CTX_LEN + kbase * GRID_W - first_blk * ROW_TILE, GRID_W)
        for hh, cols in enumerate(heads):
            q, s_ctx = ctx_scores(cols)
            kl = kw_ref[pl.ds(off, NA_TK), cols]
            vl = vw_ref[pl.ds(off, NA_TK), cols]
            s_loc = lax.dot_general(q, kl, nt, preferred_element_type=f32) + bias_ref[hh]
            m = jnp.maximum(jnp.max(s_loc, axis=-1, keepdims=True), jnp.max(s_ctx, axis=-1, keepdims=True))
            p_loc = jnp.exp(s_loc - m)
            p_ctx = jnp.exp(s_ctx - m)
            l = jnp.sum(p_loc, axis=-1, keepdims=True) + jnp.sum(p_ctx, axis=-1, keepdims=True)
            o = (jnp.dot(p_loc.astype(bf16), vl, preferred_element_type=f32)
                 + jnp.dot(p_ctx.astype(bf16), vc_ref[:, cols].astype(bf16), preferred_element_type=f32))
            o_ref[:, cols] = (o / l).astype(o_ref.dtype)


def natten(p_main, bias_tiles, hp=NA_HP):
    n = p_main.shape[0]
    n_rows = (n - CTX_LEN) // GRID_W
    n_blocks = n_rows // NA_QR
    n_blk = n // ROW_TILE
    assert n_rows % NA_QR == 0 and n_rows >= NA_KR and CTX_LEN == NA_TQ == ROW_TILE
    assert NA_TK + GRID_W <= NA_WIN_BLOCKS * ROW_TILE and n_blk > NA_WIN_BLOCKS
    w = hp * HEAD_DIM
    gpc = GROUP_W // w

    def pat(j):
        return jnp.where(j <= 1, 0, jnp.where(j == n_blocks, 2, 1))

    def win(col, t):
        return pl.BlockSpec((ROW_TILE, w), lambda h, j: (jnp.clip(j - 1, 1, n_blk - NA_WIN_BLOCKS) + t, col * gpc + h))

    return pl.pallas_call(
        functools.partial(_natten_kernel, n_rows=n_rows, n_blk=n_blk, hp=hp),
        grid=(gpc, n_blocks + 1),
        in_specs=[pl.BlockSpec((NA_TQ, w), lambda h, j: (j, COL_Q * gpc + h)),
                  pl.BlockSpec((CTX_LEN, w), lambda h, j: (0, COL_K * gpc + h)),
                  pl.BlockSpec((CTX_LEN, w), lambda h, j: (0, COL_V * gpc + h)),
                  win(COL_K, 0), win(COL_K, 1), win(COL_K, 2), win(COL_V, 0), win(COL_V, 1), win(COL_V, 2),
                  pl.BlockSpec((hp, None, NA_TQ, NA_TK), lambda h, j: (h, pat(j), 0, 0))],
        out_specs=pl.BlockSpec((NA_TQ, w), lambda h, j: (j, h)),
        out_shape=jax.ShapeDtypeStruct((n, GROUP_W), jnp.bfloat16),
        scratch_shapes=[pltpu.VMEM((NA_WIN_BLOCKS * ROW_TILE, w), jnp.bfloat16),
                        pltpu.VMEM((NA_WIN_BLOCKS * ROW_TILE, w), jnp.bfloat16)],
        compiler_params=_cparams("arbitrary", "arbitrary"),
        name="natten",
    )(*([p_main] * 9), bias_tiles)


POOL_HALO = max(POOL_WINDOWS) // 2


def _pool_kernel(prev_ref, cur_ref, next_ref, w_ref, s_ref, o_ref, ext_ref, *, n_tok):
    i = pl.program_id(0)
    t0 = i * ROW_TILE
    seq_lo = jnp.where(t0 < CTX_LEN, 0, CTX_LEN)
    seq_hi = jnp.where(t0 < CTX_LEN, CTX_LEN, n_tok)
    ext_ref[0:POOL_HALO, :] = jnp.where(t0 - POOL_HALO >= seq_lo, prev_ref[...], 0.0)
    ext_ref[POOL_HALO:POOL_HALO + ROW_TILE, :] = cur_ref[...]
    ext_ref[POOL_HALO + ROW_TILE:, :] = jnp.where(t0 + ROW_TILE < seq_hi, next_ref[...], 0.0)
    t = t0 + lax.broadcasted_iota(jnp.int32, (ROW_TILE, 1), 0)
    for g, win in enumerate(POOL_WINDOWS):
        half = win // 2
        cs = slice(g * C_GW, (g + 1) * C_GW)
        acc = ext_ref[POOL_HALO - half:POOL_HALO - half + ROW_TILE, cs]
        for d in range(-half + 1, half):
            acc = acc + ext_ref[POOL_HALO + d:POOL_HALO + d + ROW_TILE, cs]
        cnt = (jnp.minimum(t + half, seq_hi) - jnp.maximum(t - half, seq_lo)).astype(jnp.float32)
        diff = acc / cnt - cur_ref[:, cs]
        y = jnp.dot(diff.astype(jnp.bfloat16), w_ref[g], preferred_element_type=jnp.float32)
        o_ref[:, cs] = (y * s_ref[:, cs]).astype(o_ref.dtype)


def pool_mix(p_main, w_pool, pool_scale):
    n = p_main.shape[0]
    hb = ROW_TILE // POOL_HALO
    n_hblk = n // POOL_HALO
    return pl.pallas_call(
        functools.partial(_pool_kernel, n_tok=n),
        grid=(n // ROW_TILE,),
        in_specs=[pl.BlockSpec((POOL_HALO, GROUP_W), lambda i: (jnp.maximum(i * hb - 1, 0), COL_C)),
                  pl.BlockSpec((ROW_TILE, GROUP_W), lambda i: (i, COL_C)),
                  pl.BlockSpec((POOL_HALO, GROUP_W), lambda i: (jnp.minimum((i + 1) * hb, n_hblk - 1), COL_C)),
                  pl.BlockSpec((C_GROUPS, C_GW, C_GW), lambda i: (0, 0, 0)),
                  pl.BlockSpec((1, GROUP_W), lambda i: (0, 0))],
        out_specs=pl.BlockSpec((ROW_TILE, GROUP_W), lambda i: (i, 0)),
        out_shape=jax.ShapeDtypeStruct((n, GROUP_W), jnp.bfloat16),
        scratch_shapes=[pltpu.VMEM((ROW_TILE + 2 * POOL_HALO, GROUP_W), jnp.float32)],
        compiler_params=_cparams("arbitrary"),
        name="pool_mix",
    )(p_main, p_main, p_main, w_pool, pool_scale.reshape(1, GROUP_W))


def chunk_masks(chunk, reverse):
    i = jnp.arange(ROW_TILE)[:, None]
    j = jnp.arange(ROW_TILE)[None, :]
    same = (i // chunk) == (j // chunk)
    return (same & ((j >= i) if reverse else (j <= i))).astype(jnp.bfloat16)


def _split3(x):
    hi = x.astype(jnp.bfloat16)
    r = x - hi.astype(jnp.float32)
    mid = r.astype(jnp.bfloat16)
    lo = (r - mid.astype(jnp.float32)).astype(jnp.bfloat16)
    return hi, mid, lo


def _dot3(m, x):
    hi, mid, lo = _split3(x)
    d = functools.partial(jnp.dot, preferred_element_type=jnp.float32)
    return d(m, hi) + (d(m, mid) + d(m, lo))


def _scan_block(n_blk, reverse):
    if reverse:
        return lambda j: jnp.where(j == 0, 0, n_blk - j)
    return lambda j: j


def _hgrn_head(q_ref, v_ref, z_ref, lb_ref, mask_ref, st, cols, reverse):
    bf16 = jnp.bfloat16
    f32 = jnp.float32
    T, C = ROW_TILE, A_CHUNK
    nt = (((1,), (1,)), ((), ()))
    tn = (((0,), (0,)), ((), ()))
    z = z_ref[:, cols]
    log_lb = lb_ref[0:1, cols]
    log1m_lb = lb_ref[1:2, cols]
    one_m_lb = lb_ref[2:3, cols]
    log_sig = jnp.minimum(z, 0.0) - jnp.log1p(jnp.exp(-jnp.abs(z)))
    bb = log1m_lb + log_sig
    mx = jnp.maximum(log_lb, bb)
    lf = mx + jnp.log1p(jnp.exp(-jnp.abs(log_lb - bb)))
    k = one_m_lb / (1.0 + jnp.exp(z))
    aq = q_ref[:, cols]
    q = aq / (1.0 + jnp.exp(-aq))
    v = v_ref[:, cols].astype(bf16)

    b = _dot3(mask_ref[...], lf)
    last = 0 if reverse else C - 1
    b_end = jnp.concatenate([jnp.broadcast_to(b[c * C + last:c * C + last + 1, :], (C, HEAD_DIM))
                             for c in range(T // C)], axis=0)
    q_dec = (q * jnp.exp(b)).astype(bf16)
    k_inv = (k * jnp.exp(-b)).astype(bf16)
    k_dec = (k * jnp.exp(b_end - b)).astype(bf16)
    dec = jnp.exp(b_end)

    scores = lax.dot_general(q_dec, k_inv, nt, preferred_element_type=f32)
    scores = jnp.where(mask_ref[...] > 0, scores, 0.0)
    o_intra = jnp.dot(scores.astype(bf16), v, preferred_element_type=f32)

    n_c = T // C
    rows = [slice(c * C, (c + 1) * C) for c in range(n_c)]
    deltas = [lax.dot_general(v[r], k_dec[r], tn, preferred_element_type=f32) for r in rows]
    entering = [None] * n_c
    for c in (range(n_c - 1, -1, -1) if reverse else range(n_c)):
        entering[c] = st
        st = st * dec[c * C:c * C + 1, :] + deltas[c]
    outs = [lax.dot_general(q_dec[r], entering[c].astype(bf16), nt, preferred_element_type=f32)
            for c, r in enumerate(rows)]
    return o_intra + jnp.concatenate(outs, axis=0), st


def _hgrn_kernel(q_ref, v_ref, z_ref, lb_ref, mask_ref, *rest, reverse, hp):
    if reverse:
        g_ref, of_ref, gn_ref, o_ref, st_ref = rest
    else:
        o_ref, st_ref = rest

    @pl.when(pl.program_id(1) == 0)
    def _():
        st_ref[...] = jnp.zeros_like(st_ref)

    for hh in range(hp):
        cols = slice(hh * HEAD_DIM, (hh + 1) * HEAD_DIM)
        o, st = _hgrn_head(q_ref, v_ref, z_ref, lb_ref, mask_ref, st_ref[hh], cols, reverse)
        st_ref[hh] = st
        if reverse:
            o = o + of_ref[:, cols]
            o = o * lax.rsqrt(jnp.mean(o * o, axis=-1, keepdims=True) + EPS) * gn_ref[:, cols]
            ag = g_ref[:, cols]
            o_ref[:, cols] = (o * (ag / (1.0 + jnp.exp(-ag)))).astype(o_ref.dtype)
        else:
            o_ref[:, cols] = o


def hgrn_scan(p_main, lb_rows, mask, reverse, o_fwd=None, gn=None, hp=SCAN_HP):
    n = p_main.shape[0]
    n_blk = n // ROW_TILE
    tb = _scan_block(n_blk, reverse)
    w = hp * HEAD_DIM
    gpc = GROUP_W // w
    blk = lambda c: pl.BlockSpec((ROW_TILE, w), lambda h, j: (tb(j), c * gpc + h))
    in_specs = [blk(COL_AQ), blk(COL_AI), blk(COL_AFB if reverse else COL_AFF),
                pl.BlockSpec((3, w), lambda h, j: (0, h)),
                pl.BlockSpec((ROW_TILE, ROW_TILE), lambda h, j: (0, 0))]
    args = [p_main, p_main, p_main, lb_rows, mask]
    if reverse:
        in_specs += [blk(COL_AG), pl.BlockSpec((ROW_TILE, w), lambda h, j: (tb(j), h)),
                     pl.BlockSpec((1, w), lambda h, j: (0, h))]
        args += [p_main, o_fwd, gn.reshape(1, GROUP_W)]
    return pl.pallas_call(
        functools.partial(_hgrn_kernel, reverse=reverse, hp=hp),
        grid=(gpc, n_blk),
        in_specs=in_specs,
        out_specs=pl.BlockSpec((ROW_TILE, w), lambda h, j: (tb(j), h)),
        out_shape=jax.ShapeDtypeStruct((n, GROUP_W), jnp.bfloat16 if reverse else jnp.float32),
        scratch_shapes=[pltpu.VMEM((hp, HEAD_DIM, HEAD_DIM), jnp.float32)],
        compiler_params=_cparams("arbitrary", "arbitrary"),
        name="hgrn_bwd" if reverse else "hgrn_fwd",
    )(*args)


def hgrn_mix(p_main, lb_f, lb_b, gn):
    rows = lambda lb: jnp.stack([jnp.log(lb), jnp.log1p(-lb), 1.0 - lb])
    o_f = hgrn_scan(p_main, rows(lb_f), chunk_masks(A_CHUNK, False), False)
    return hgrn_scan(p_main, rows(lb_b), chunk_masks(A_CHUNK, True), True, o_f, gn)


def hgrn_lower_bounds(p):
    cs = jnp.cumsum(jax.nn.softmax(p.astype(jnp.float32), axis=1), axis=1)
    return cs - cs[:, :1]


def rope_tables(n_tok):
    quarter = HEAD_DIM // 4
    pos = jnp.arange(n_tok - CTX_LEN)
    inv_freq = ROPE_BASE ** (-jnp.arange(quarter, dtype=jnp.float32) / quarter)
    lane = jnp.arange(HEAD_DIM)
    p = jnp.where(lane[None, :] < HEAD_DIM // 2, (pos // GRID_W)[:, None], (pos % GRID_W)[:, None]).astype(jnp.float32)
    ang = p * inv_freq[lane % quarter][None, :]
    sign = jnp.where((lane % (2 * quarter)) < quarter, -1.0, 1.0)
    cos = jnp.concatenate([jnp.ones((CTX_LEN, HEAD_DIM), jnp.float32), jnp.cos(ang)], axis=0)
    sin = jnp.concatenate([jnp.zeros((CTX_LEN, HEAD_DIM), jnp.float32), jnp.sin(ang) * sign], axis=0)
    return cos, sin


def _rope(x, cos, sin):
    quarter = HEAD_DIM // 4
    lane = lax.broadcasted_iota(jnp.int32, x.shape, 1)
    partner = jnp.where((lane % (2 * quarter)) < quarter,
                        pltpu.roll(x, HEAD_DIM - quarter, axis=1), pltpu.roll(x, quarter, axis=1))
    return x * cos + partner * sin


def _mlstm_head(q_ref, k_ref, v_ref, cos, sin, cols, b_bc, li_bc, b_row, li_row, state, reverse):
    bf16 = jnp.bfloat16
    f32 = jnp.float32
    T, C = ROW_TILE, D_CHUNK
    n_c = T // C
    nt = (((1,), (1,)), ((), ()))
    tn = (((0,), (0,)), ((), ()))
    d = functools.partial(jnp.dot, preferred_element_type=f32)
    q = _rope(q_ref[:, cols], cos, sin) * HEAD_DIM ** -0.5
    k = _rope(k_ref[:, cols], cos, sin)
    qb = q.astype(bf16)
    kb = k.astype(bf16)
    vb = v_ref[:, cols].astype(bf16)
    ii = lax.broadcasted_iota(jnp.int32, (C, C), 0)
    jj = lax.broadcasted_iota(jnp.int32, (C, C), 1)
    causal = (jj >= ii) if reverse else (jj <= ii)
    last = 0 if reverse else C - 1
    rows = [slice(c * C, (c + 1) * C) for c in range(n_c)]
    bends, m_locs, d_ss, d_ns = [], [], [], []
    for c, r in enumerate(rows):
        bend = b_bc[c * C + last:c * C + last + 1]
        gcol = bend - b_bc[r] + li_bc[r]
        m_loc = jnp.max(gcol, axis=0, keepdims=True)
        kw = k[r] * jnp.exp(gcol - m_loc)
        bends.append(bend)
        m_locs.append(m_loc)
        d_ss.append(lax.dot_general(kw.astype(bf16), vb[r], tn, preferred_element_type=f32))
        d_ns.append(jnp.sum(kw, axis=0, keepdims=True))
    s, nv, m = state
    entering = [None] * n_c
    for c in (range(n_c - 1, -1, -1) if reverse else range(n_c)):
        entering[c] = (s, nv, m)
        m_new = jnp.maximum(bends[c] + m, m_locs[c])
        a = jnp.exp(bends[c] + m - m_new)
        cc = jnp.exp(m_locs[c] - m_new)
        s = a * s + cc * d_ss[c]
        nv = a * nv + cc * d_ns[c]
        m = m_new
    outs = []
    for c, r in enumerate(rows):
        s_in, n_in, m_in = entering[c]
        dmat = jnp.where(causal, b_bc[r][:, 0:C] - b_row[:, r] + li_row[:, r], -jnp.inf)
        m_inter = b_bc[r] + m_in
        m_t = jnp.maximum(m_inter, jnp.max(dmat, axis=-1, keepdims=True))
        w = jnp.exp(dmat - m_t[:, 0:C]) * lax.dot_general(qb[r], kb[r], nt, preferred_element_type=f32)
        a_col = jnp.exp(m_inter - m_t)
        num = d(w.astype(bf16), vb[r]) + a_col * d(qb[r], s_in.astype(bf16))
        den = jnp.sum(w, axis=-1, keepdims=True) + a_col * jnp.sum(q[r] * n_in, axis=-1, keepdims=True)
        outs.append(num / jnp.maximum(jnp.abs(den), jnp.exp(-m_t)))
    return jnp.concatenate(outs, axis=0), (s, nv, m)


def _mlstm_kernel(q_ref, k_ref, v_ref, g_ref, cos_ref, sin_ref, mask_ref, maskt_ref, *rest, reverse, hp):
    if reverse:
        do_ref, of_ref, gn_ref, o_ref, s_ref, n_ref, m_ref, cumt_ref, lit_ref = rest
    else:
        o_ref, s_ref, n_ref, m_ref, cumt_ref, lit_ref = rest
    hg = pl.program_id(0)
    f32 = jnp.float32
    bf16 = jnp.bfloat16

    @pl.when(pl.program_id(1) == 0)
    def _():
        s_ref[...] = jnp.zeros_like(s_ref)
        n_ref[...] = jnp.zeros_like(n_ref)
        m_ref[...] = jnp.zeros_like(m_ref)

    g = g_ref[...]
    lsg = jnp.minimum(g, 0.0) - jnp.log1p(jnp.exp(-jnp.abs(g)))
    cum = _dot3(mask_ref[...], lsg)
    hi, mid, lo = _split3(lsg.T)
    d = functools.partial(jnp.dot, preferred_element_type=f32)
    cumt_ref[...] = d(hi, maskt_ref[...]) + (d(mid, maskt_ref[...]) + d(lo, maskt_ref[...]))
    lit_ref[...] = g.T
    cum_parts = _split3(cum)
    g_parts = _split3(g)
    sel_row = lax.broadcasted_iota(jnp.int32, (LANES, LANES), 0)

    def replicate(parts, col):
        sel = (sel_row == col).astype(bf16)
        return d(parts[0], sel) + (d(parts[1], sel) + d(parts[2], sel))

    cos = cos_ref[...]
    sin = sin_ref[...]
    for hh in range(hp):
        h = hg * hp + hh
        col_i = (GATE_I_B if reverse else GATE_I_F) * N_HEADS + h
        col_f = (GATE_F_B if reverse else GATE_F_F) * N_HEADS + h
        b_bc = replicate(cum_parts, col_f)
        li_bc = replicate(g_parts, col_i)
        b_row = cumt_ref[pl.ds(col_f, 1), :]
        li_row = lit_ref[pl.ds(col_i, 1), :]
        cols = slice(hh * HEAD_DIM, (hh + 1) * HEAD_DIM)
        state = (s_ref[hh], n_ref[hh, 0:1, :], m_ref[hh, 0:1, :])
        o, (s, nv, m) = _mlstm_head(q_ref, k_ref, v_ref, cos, sin, cols, b_bc, li_bc, b_row, li_row, state, reverse)
        s_ref[hh] = s
        n_ref[hh] = jnp.broadcast_to(nv, n_ref.shape[1:])
        m_ref[hh] = jnp.broadcast_to(m, m_ref.shape[1:])
        if reverse:
            o = o + of_ref[:, cols]
            o = o * lax.rsqrt(jnp.mean(o * o, axis=-1, keepdims=True) + EPS) * gn_ref[:, cols]
            o_ref[:, cols] = (o / (1.0 + jnp.exp(-do_ref[:, cols]))).astype(o_ref.dtype)
        else:
            o_ref[:, cols] = o


def mlstm_scan(p_main, p_gate, cos, sin, mask, reverse, o_fwd=None, gn=None, hp=SCAN_HP):
    n = p_main.shape[0]
    n_blk = n // ROW_TILE
    tb = _scan_block(n_blk, reverse)
    w = hp * HEAD_DIM
    gpc = GROUP_W // w
    blk = lambda c: pl.BlockSpec((ROW_TILE, w), lambda h, j: (tb(j), c * gpc + h))
    tok = pl.BlockSpec((ROW_TILE, LANES), lambda h, j: (tb(j), 0))
    sq = pl.BlockSpec((ROW_TILE, ROW_TILE), lambda h, j: (0, 0))
    in_specs = [blk(COL_DQ), blk(COL_DK), blk(COL_DV), tok, tok, tok, sq, sq]
    args = [p_main, p_main, p_main, p_gate, cos, sin, mask, mask.T]
    if reverse:
        in_specs += [blk(COL_DO), pl.BlockSpec((ROW_TILE, w), lambda h, j: (tb(j), h)),
                     pl.BlockSpec((1, w), lambda h, j: (0, h))]
        args += [p_main, o_fwd, gn.reshape(1, GROUP_W)]
    return pl.pallas_call(
        functools.partial(_mlstm_kernel, reverse=reverse, hp=hp),
        grid=(gpc, n_blk),
        in_specs=in_specs,
        out_specs=pl.BlockSpec((ROW_TILE, w), lambda h, j: (tb(j), h)),
        out_shape=jax.ShapeDtypeStruct((n, GROUP_W), jnp.bfloat16 if reverse else jnp.float32),
        scratch_shapes=[pltpu.VMEM((hp, HEAD_DIM, HEAD_DIM), jnp.float32), pltpu.VMEM((hp, 8, HEAD_DIM), jnp.float32),
                        pltpu.VMEM((hp, 8, LANES), jnp.float32), pltpu.VMEM((LANES, ROW_TILE), jnp.float32),
                        pltpu.VMEM((LANES, ROW_TILE), jnp.float32)],
        compiler_params=_cparams("arbitrary", "arbitrary"),
        name="mlstm_bwd" if reverse else "mlstm_fwd",
    )(*args)


def mlstm_mix(p_main, p_gate, gn):
    cos, sin = rope_tables(p_main.shape[0])
    o_f = mlstm_scan(p_main, p_gate, cos, sin, chunk_masks(D_CHUNK, False), False)
    return mlstm_scan(p_main, p_gate, cos, sin, chunk_masks(D_CHUNK, True), True, o_f, gn)


def kernel(x, c, ctx, c_ctx, ada_w, ada_b, norm1_w, norm2_w, mix_w_in, mix_b_in, hgrn_lb, hgrn_gn, natten_rpb,
           pool_w, pool_scale, mlstm_gn, mix_w_out, moe_wg, moe_bg, moe_we, moe_be, moe_w1, moe_w3, moe_w2, final_w):
    assert x.shape == (1, SEQ, D_MODEL) and ctx.shape == (1, CTX_LEN, D_MODEL)
    bf16 = jnp.bfloat16
    lbs = hgrn_lower_bounds(hgrn_lb)

    cond = jnp.zeros((16, D_MODEL), jnp.float32).at[0].set(jax.nn.silu(c_ctx)).at[1].set(jax.nn.silu(c[0]))
    mods = ada_modulation(cond.astype(bf16), ada_w, ada_b)[:, :2].reshape(DEPTH, 2, 6, D_MODEL)

    w_in_t = jnp.swapaxes(mix_w_in, 1, 2)
    w_in = cast_bf16_transposed(w_in_t, N_MAIN)
    w_out = cast_bf16(mix_w_out, 4 * LANES)

    xs = jnp.concatenate([ctx[0], x[0]], axis=0)
    for l in range(DEPTH):
        mod = mods[l]
        h = norm_modulate(xs, norm1_w[l], mod, 0)
        p_main, w1, w3, w2 = in_proj_cast(h, w_in, mix_b_in[l], l, moe_w1, moe_w3, moe_w2)
        w_gate_t = jnp.pad(w_in_t[l, N_MAIN:, :], ((0, LANES - N_GATE), (0, 0)))
        b_gate = jnp.pad(mix_b_in[l, N_MAIN:], (0, LANES - N_GATE))
        p_gate = matmul_bias_nt(h, w_gate_t, b_gate, MM_TM)

        out_a = hgrn_mix(p_main, lbs[0, l], lbs[1, l], hgrn_gn[l])
        out_b = natten(p_main, natten_bias_tiles(natten_rpb[l]))
        out_c = pool_mix(p_main, pool_w[l].astype(bf16), pool_scale[l])
        out_d = mlstm_mix(p_main, p_gate, mlstm_gn[l])
        xs = matmul_gated_residual((out_a, out_b, out_c, out_d), w_out, l, xs, mod, 2)

        w_router = jnp.pad(jnp.concatenate([moe_wg[l], moe_we[l]], axis=1),
                           ((0, 0), (0, LANES - N_GROUPS - N_EXPERTS)))
        h2, logits = norm_modulate_router(xs, norm2_w[l], mod, w_router)
        eid, top_w = moe_route(logits, moe_bg[l], moe_be[l])
        tile_expert, tile_valid, row_token, cpos = moe_schedule(eid)
        ys = moe_grouped(h2, tile_expert, tile_valid, row_token, w1, w3, w2)
        xs = moe_combine(cpos, ys, xs, top_w, mod, 5)

    return norm_modulate(xs, final_w, mods[0], None, out_dtype=jnp.float32, skip_rows=CTX_LEN)[None]
```

```python
import functools

import jax
import jax.numpy as jnp
from jax import lax
from jax.experimental import pallas as pl
from jax.experimental.pallas import tpu as pltpu

D_MODEL = 4096
SEQ = 8192
DEPTH = 2
GRID_W = 64
CTX_LEN = 256
N_TOK = CTX_LEN + SEQ
GROUP_W = 1024
HEAD_DIM = 128
N_HEADS = GROUP_W // HEAD_DIM
C_GROUPS = 4
C_GW = GROUP_W // C_GROUPS
POOL_WINDOWS = (2, 4, 8, 16)
NA_MAX_ROWS = 8
NA_COLS = 16
A_CHUNK = 32
D_CHUNK = 64
ROPE_BASE = 10000.0
N_GROUPS = 4
EXP_PER_GROUP = 8
N_EXPERTS = N_GROUPS * EXP_PER_GROUP
TOP_K = 2
D_FF = D_MODEL // 8
EPS = 1e-6
N_MAIN = 13 * GROUP_W
N_GATE = 4 * N_HEADS
LANES = 128
VMEM_LIMIT = 56 * 1024 * 1024

ROW_TILE = 256
MM_TM = 768
MM_TN = 512
MOE_TM = 256
MOE_TILES = (TOP_K * N_TOK + N_EXPERTS * (MOE_TM - 1)) // MOE_TM + 1
CMB_TB = 128
SCAN_HP = 8
DMA_UNROLL = 8
CAST_SLABS = 8
CT_TN, CT_TK = 1024, 1024
MOE_AHEAD = 3
MOE_SLOTS = MOE_AHEAD + 1

COL_AQ, COL_AI, COL_AG, COL_AFF, COL_AFB = 0, 1, 2, 3, 4
COL_Q, COL_K, COL_V, COL_C = 5, 6, 7, 8
COL_DQ, COL_DK, COL_DV, COL_DO = 9, 10, 11, 12
GATE_I_F, GATE_F_F, GATE_I_B, GATE_F_B = 0, 1, 2, 3
HPG = GROUP_W // HEAD_DIM


def _cparams(*sem):
    return pltpu.CompilerParams(dimension_semantics=sem, vmem_limit_bytes=VMEM_LIMIT)


def _ada_kernel(c_ref, w_ref, b_ref, o_ref):
    acc = jnp.dot(c_ref[...], w_ref[...].astype(jnp.bfloat16), preferred_element_type=jnp.float32)
    o_ref[...] = acc + b_ref[...]


def ada_modulation(cond, ada_w, ada_b):
    tn = 512
    n = ada_w.shape[-1]
    rows = cond.shape[0]
    return pl.pallas_call(
        _ada_kernel,
        grid=(DEPTH, n // tn),
        in_specs=[pl.BlockSpec((rows, D_MODEL), lambda l, j: (0, 0)),
                  pl.BlockSpec((None, D_MODEL, tn), lambda l, j: (l, 0, j)),
                  pl.BlockSpec((None, 1, tn), lambda l, j: (l, 0, j))],
        out_specs=pl.BlockSpec((None, rows, tn), lambda l, j: (l, 0, j)),
        out_shape=jax.ShapeDtypeStruct((DEPTH, rows, n), jnp.float32),
        compiler_params=_cparams("arbitrary", "arbitrary"),
        name="ada_modulation",
    )(cond, ada_w, ada_b.reshape(DEPTH, 1, n))


def _norm_kernel(x_ref, w_ref, mod_ref, o_ref, *, shift_row, scale_row):
    x = x_ref[...]
    y = x * lax.rsqrt(jnp.mean(x * x, axis=-1, keepdims=True) + EPS) * w_ref[...]
    if shift_row is not None:
        y = y * (1.0 + mod_ref[scale_row:scale_row + 1, :]) + mod_ref[shift_row:shift_row + 1, :]
    o_ref[...] = y.astype(o_ref.dtype)


def _norm_router_kernel(x_ref, w_ref, mod_ref, wr_ref, o_ref, lg_ref, *, shift_row, scale_row):
    x = x_ref[...]
    y = x * lax.rsqrt(jnp.mean(x * x, axis=-1, keepdims=True) + EPS) * w_ref[...]
    y = y * (1.0 + mod_ref[scale_row:scale_row + 1, :]) + mod_ref[shift_row:shift_row + 1, :]
    o_ref[...] = y.astype(o_ref.dtype)
    y_hi = y.astype(jnp.bfloat16)
    y_lo = (y - y_hi.astype(jnp.float32)).astype(jnp.bfloat16)
    w = wr_ref[...]
    w_hi = w.astype(jnp.bfloat16)
    w_lo = (w - w_hi.astype(jnp.float32)).astype(jnp.bfloat16)
    dot = functools.partial(jnp.dot, preferred_element_type=jnp.float32)
    lg_ref[...] = dot(y_hi, w_hi) + (dot(y_lo, w_hi) + dot(y_hi, w_lo))


def _tok_type(i):
    return jnp.where(i * ROW_TILE >= CTX_LEN, 1, 0)


def norm_modulate(x, w, mod, which, out_dtype=jnp.bfloat16, skip_rows=0):
    n = x.shape[0] - skip_rows
    off = skip_rows // ROW_TILE
    rows = (None, None) if which is None else (3 * which, 3 * which + 1)
    return pl.pallas_call(
        functools.partial(_norm_kernel, shift_row=rows[0], scale_row=rows[1]),
        grid=(n // ROW_TILE,),
        in_specs=[pl.BlockSpec((ROW_TILE, D_MODEL), lambda i: (i + off, 0)),
                  pl.BlockSpec((1, D_MODEL), lambda i: (0, 0)),
                  pl.BlockSpec((None, 6, D_MODEL), lambda i: (_tok_type(i + off), 0, 0))],
        out_specs=pl.BlockSpec((ROW_TILE, D_MODEL), lambda i: (i, 0)),
        out_shape=jax.ShapeDtypeStruct((n, D_MODEL), out_dtype),
        compiler_params=_cparams("arbitrary"),
        name="norm_modulate",
    )(x, w.reshape(1, D_MODEL), mod)


def norm_modulate_router(x, w, mod, w_router):
    n = x.shape[0]
    return pl.pallas_call(
        functools.partial(_norm_router_kernel, shift_row=3, scale_row=4),
        grid=(n // ROW_TILE,),
        in_specs=[pl.BlockSpec((ROW_TILE, D_MODEL), lambda i: (i, 0)),
                  pl.BlockSpec((1, D_MODEL), lambda i: (0, 0)),
                  pl.BlockSpec((None, 6, D_MODEL), lambda i: (_tok_type(i), 0, 0)),
                  pl.BlockSpec((D_MODEL, LANES), lambda i: (0, 0))],
        out_specs=[pl.BlockSpec((ROW_TILE, D_MODEL), lambda i: (i, 0)),
                   pl.BlockSpec((ROW_TILE, LANES), lambda i: (i, 0))],
        out_shape=[jax.ShapeDtypeStruct((n, D_MODEL), jnp.float32),
                   jax.ShapeDtypeStruct((n, LANES), jnp.float32)],
        compiler_params=_cparams("arbitrary"),
        name="norm_modulate_router",
    )(x, w.reshape(1, D_MODEL), mod, w_router)


def _mm_bias_nt_kernel(a_ref, bt_ref, bias_ref, o_ref):
    bt = bt_ref[...].astype(jnp.bfloat16)
    acc = lax.dot_general(a_ref[...], bt, (((1,), (1,)), ((), ())), preferred_element_type=jnp.float32)
    o_ref[...] = acc + bias_ref[...]


def _mm_resid_kernel(a0_ref, a1_ref, a2_ref, a3_ref, b_ref, x_ref, mod_ref, o_ref, *, gate_row):
    acc = None
    for g, a_ref in enumerate((a0_ref, a1_ref, a2_ref, a3_ref)):
        part = jnp.dot(a_ref[...], b_ref[g * GROUP_W:(g + 1) * GROUP_W, :], preferred_element_type=jnp.float32)
        acc = part if acc is None else acc + part
    tm = x_ref.shape[0]
    row = pl.program_id(0) * tm + lax.broadcasted_iota(jnp.int32, (tm, 1), 0)
    gate = jnp.where(row < CTX_LEN, mod_ref[0, gate_row:gate_row + 1, :], mod_ref[1, gate_row:gate_row + 1, :])
    o_ref[...] = x_ref[...] + gate * acc


def matmul_bias_nt(a, bt, bias, tm):
    m, k = a.shape
    n = bt.shape[0]
    return pl.pallas_call(
        _mm_bias_nt_kernel,
        grid=(m // tm,),
        in_specs=[pl.BlockSpec((tm, k), lambda i: (i, 0)),
                  pl.BlockSpec((n, k), lambda i: (0, 0)),
                  pl.BlockSpec((1, n), lambda i: (0, 0))],
        out_specs=pl.BlockSpec((tm, n), lambda i: (i, 0)),
        out_shape=jax.ShapeDtypeStruct((m, n), jnp.float32),
        compiler_params=_cparams("arbitrary"),
        name="matmul_bias_nt",
    )(a, bt, bias.reshape(1, n))


def matmul_gated_residual(parts, b, layer, x, mod, gate_row):
    m, n = x.shape
    tm, tn = MM_TM, MM_TN
    a_spec = pl.BlockSpec((tm, GROUP_W), lambda i, j: (i, 0))
    return pl.pallas_call(
        functools.partial(_mm_resid_kernel, gate_row=gate_row),
        grid=(m // tm, n // tn),
        in_specs=[a_spec, a_spec, a_spec, a_spec,
                  pl.BlockSpec((None, b.shape[1], tn), lambda i, j: (layer, 0, j)),
                  pl.BlockSpec((tm, tn), lambda i, j: (i, j)),
                  pl.BlockSpec((2, 6, tn), lambda i, j: (0, 0, j))],
        out_specs=pl.BlockSpec((tm, tn), lambda i, j: (i, j)),
        out_shape=jax.ShapeDtypeStruct((m, n), jnp.float32),
        compiler_params=_cparams("arbitrary", "arbitrary"),
        name="matmul_gated_residual",
    )(*parts, b, x, mod)


def _cast_kernel(x_ref, o_ref):
    o_ref[...] = x_ref[...].astype(o_ref.dtype)


def cast_bf16(w, rows):
    nl, r, c = w.shape
    spec = pl.BlockSpec((None, rows, c), lambda l, i: (l, i, 0))
    return pl.pallas_call(
        _cast_kernel,
        grid=(nl, r // rows),
        in_specs=[spec],
        out_specs=spec,
        out_shape=jax.ShapeDtypeStruct(w.shape, jnp.bfloat16),
        compiler_params=_cparams("arbitrary", "arbitrary"),
        name="cast_bf16",
    )(w)


def _cast_t_kernel(x_ref, o_ref):
    o_ref[...] = x_ref[...].T.astype(o_ref.dtype)


def cast_bf16_transposed(wt, n_cols):
    nl, _, k = wt.shape
    return pl.pallas_call(
        _cast_t_kernel,
        grid=(nl, k // CT_TK, n_cols // CT_TN),
        in_specs=[pl.BlockSpec((None, CT_TN, CT_TK), lambda l, i, j: (l, j, i))],
        out_specs=pl.BlockSpec((None, CT_TK, CT_TN), lambda l, i, j: (l, i, j)),
        out_shape=jax.ShapeDtypeStruct((nl, k, n_cols), jnp.bfloat16),
        compiler_params=_cparams("arbitrary", "arbitrary", "arbitrary"),
        name="cast_bf16_transposed",
    )(wt)


def _in_proj_cast_kernel(a_ref, b_ref, bias_ref, w1_ref, w3_ref, w2_ref, o_ref, w1o_ref, w3o_ref, w2o_ref):
    o_ref[...] = jnp.dot(a_ref[...], b_ref[...], preferred_element_type=jnp.float32) + bias_ref[...]
    w1o_ref[...] = w1_ref[...].astype(w1o_ref.dtype)
    w3o_ref[...] = w3_ref[...].astype(w3o_ref.dtype)
    w2o_ref[...] = w2_ref[...].astype(w2o_ref.dtype)


def in_proj_cast(a, b, bias, layer, w1, w3, w2):
    m, k = a.shape
    n, tm, tn = N_MAIN, MM_TM, MM_TN
    n_j = n // tn
    n_slabs = N_EXPERTS * CAST_SLABS
    assert (m // tm) * n_j >= n_slabs
    r13, r2 = D_MODEL // CAST_SLABS, D_FF // CAST_SLABS

    def slab(i, j):
        s = jnp.minimum(i * n_j + j, n_slabs - 1)
        return s // CAST_SLABS, s % CAST_SLABS

    in13 = pl.BlockSpec((None, None, r13, D_FF), lambda i, j: (layer, *slab(i, j), 0))
    in2 = pl.BlockSpec((None, None, r2, D_MODEL), lambda i, j: (layer, *slab(i, j), 0))
    out13 = pl.BlockSpec((None, r13, D_FF), lambda i, j: (*slab(i, j), 0))
    out2 = pl.BlockSpec((None, r2, D_MODEL), lambda i, j: (*slab(i, j), 0))
    bf16 = jnp.bfloat16
    return pl.pallas_call(
        _in_proj_cast_kernel,
        grid=(m // tm, n_j),
        in_specs=[pl.BlockSpec((tm, k), lambda i, j: (i, 0)),
                  pl.BlockSpec((None, k, tn), lambda i, j: (layer, 0, j)),
                  pl.BlockSpec((1, tn), lambda i, j: (0, j)),
                  in13, in13, in2],
        out_specs=[pl.BlockSpec((tm, tn), lambda i, j: (i, j)), out13, out13, out2],
        out_shape=[jax.ShapeDtypeStruct((m, n), jnp.float32),
                   jax.ShapeDtypeStruct((N_EXPERTS, D_MODEL, D_FF), bf16),
                   jax.ShapeDtypeStruct((N_EXPERTS, D_MODEL, D_FF), bf16),
                   jax.ShapeDtypeStruct((N_EXPERTS, D_FF, D_MODEL), bf16)],
        compiler_params=_cparams("arbitrary", "arbitrary"),
        name="in_proj_cast",
    )(a, b, bias.reshape(1, -1), w1, w3, w2)


def _moe_kernel(te_ref, tv_ref, rt_ref, h_hbm, w1_ref, w3_ref, w2_ref, y_ref, xg_ref, sem_ref):
    del te_ref
    i = pl.program_id(0)
    n_tiles = pl.num_programs(0)

    def row_copy(tile, slot, r):
        tok = rt_ref[tile * MOE_TM + r]
        return pltpu.make_async_copy(h_hbm.at[pl.ds(tok, 1), :], xg_ref.at[slot, pl.ds(r, 1), :], sem_ref.at[slot])

    def start_gather(tile, slot):
        @pl.when(tv_ref[tile] != 0)
        def _():
            def body(r, carry):
                row_copy(tile, slot, r).start()
                return carry
            lax.fori_loop(0, MOE_TM, body, 0, unroll=DMA_UNROLL)

    @pl.when(i == 0)
    def _():
        for t in range(MOE_AHEAD):
            start_gather(t, t)

    @pl.when(i + MOE_AHEAD < n_tiles)
    def _():
        start_gather(i + MOE_AHEAD, (i + MOE_AHEAD) % MOE_SLOTS)

    @pl.when(tv_ref[i] != 0)
    def _():
        slot = i % MOE_SLOTS

        def wait_body(r, carry):
            row_copy(i, slot, r).wait()
            return carry
        lax.fori_loop(0, MOE_TM, wait_body, 0, unroll=DMA_UNROLL)
        xg = xg_ref[slot].astype(jnp.bfloat16)
        a = jnp.dot(xg, w1_ref[...], preferred_element_type=jnp.float32)
        b = jnp.dot(xg, w3_ref[...], preferred_element_type=jnp.float32)
        act = (a * jax.nn.sigmoid(a)) * b
        y_ref[...] = jnp.dot(act.astype(jnp.bfloat16), w2_ref[...], preferred_element_type=jnp.float32)

    @pl.when(tv_ref[i] == 0)
    def _():
        y_ref[...] = jnp.zeros_like(y_ref)


def moe_grouped(h, tile_expert, tile_valid, row_token, w1, w3, w2):
    grid_spec = pltpu.PrefetchScalarGridSpec(
        num_scalar_prefetch=3,
        grid=(MOE_TILES,),
        in_specs=[pl.BlockSpec(memory_space=pl.ANY),
                  pl.BlockSpec((None, D_MODEL, D_FF), lambda i, te, tv, rt: (te[i], 0, 0)),
                  pl.BlockSpec((None, D_MODEL, D_FF), lambda i, te, tv, rt: (te[i], 0, 0)),
                  pl.BlockSpec((None, D_FF, D_MODEL), lambda i, te, tv, rt: (te[i], 0, 0))],
        out_specs=pl.BlockSpec((MOE_TM, D_MODEL), lambda i, te, tv, rt: (i, 0)),
        scratch_shapes=[pltpu.VMEM((MOE_SLOTS, MOE_TM, D_MODEL), jnp.float32),
                        pltpu.SemaphoreType.DMA((MOE_SLOTS,))],
    )
    return pl.pallas_call(
        _moe_kernel,
        grid_spec=grid_spec,
        out_shape=jax.ShapeDtypeStruct((MOE_TILES * MOE_TM, D_MODEL), jnp.float32),
        compiler_params=_cparams("arbitrary"),
        name="moe_grouped",
    )(tile_expert, tile_valid, row_token, h, w1, w3, w2)


def _combine_kernel(pos_ref, y_hbm, x_ref, w_ref, mod_ref, o_ref, buf_ref, sem_ref, *, gate_row):
    i = pl.program_id(0)
    n_steps = pl.num_programs(0)

    def row_copy(step, slot, r):
        src = pos_ref[step * (TOP_K * CMB_TB) + r]
        return pltpu.make_async_copy(y_hbm.at[pl.ds(src, 1), :], buf_ref.at[slot, pl.ds(r, 1), :], sem_ref.at[slot])

    def start_gather(step, slot):
        def body(r, carry):
            row_copy(step, slot, r).start()
            return carry
        lax.fori_loop(0, TOP_K * CMB_TB, body, 0, unroll=DMA_UNROLL)

    @pl.when(i == 0)
    def _():
        start_gather(0, 0)

    @pl.when(i + 1 < n_steps)
    def _():
        start_gather(i + 1, (i + 1) % 2)

    slot = i % 2

    def wait_body(r, carry):
        row_copy(i, slot, r).wait()
        return carry
    lax.fori_loop(0, TOP_K * CMB_TB, wait_body, 0, unroll=DMA_UNROLL)
    w = w_ref[...]
    moe = w[:, 0:1] * buf_ref[slot, 0:CMB_TB, :] + w[:, 1:2] * buf_ref[slot, CMB_TB:2 * CMB_TB, :]
    o_ref[...] = x_ref[...] + mod_ref[gate_row:gate_row + 1, :] * moe


def moe_combine(pos, ys, x, top_w, mod, gate_row):
    n = x.shape[0]
    grid_spec = pltpu.PrefetchScalarGridSpec(
        num_scalar_prefetch=1,
        grid=(n // CMB_TB,),
        in_specs=[pl.BlockSpec(memory_space=pl.ANY),
                  pl.BlockSpec((CMB_TB, D_MODEL), lambda i, pos: (i, 0)),
                  pl.BlockSpec((CMB_TB, TOP_K), lambda i, pos: (i, 0)),
                  pl.BlockSpec((None, 6, D_MODEL), lambda i, pos: (jnp.where(i * CMB_TB >= CTX_LEN, 1, 0), 0, 0))],
        out_specs=pl.BlockSpec((CMB_TB, D_MODEL), lambda i, pos: (i, 0)),
        scratch_shapes=[pltpu.VMEM((2, TOP_K * CMB_TB, D_MODEL), jnp.float32), pltpu.SemaphoreType.DMA((2,))],
    )
    return pl.pallas_call(
        functools.partial(_combine_kernel, gate_row=gate_row),
        grid_spec=grid_spec,
        out_shape=jax.ShapeDtypeStruct((n, D_MODEL), jnp.float32),
        compiler_params=_cparams("arbitrary"),
        name="moe_combine",
    )(pos, ys, x, top_w, mod)


def moe_route(logits, bg, be):
    n = logits.shape[0]
    g_logits = logits[:, :N_GROUPS] + bg
    grp = jnp.argmax(g_logits, axis=-1)
    g_sel = grp[:, None] == jnp.arange(N_GROUPS)[None, :]
    p_grp = jnp.sum(jnp.where(g_sel, jax.nn.softmax(g_logits, axis=-1), 0.0), axis=-1, keepdims=True)
    e_logits = (logits[:, N_GROUPS:N_GROUPS + N_EXPERTS] + be).reshape(n, N_GROUPS, EXP_PER_GROUP)
    e_in = jnp.sum(jnp.where(g_sel[:, :, None], e_logits, 0.0), axis=1)
    lane = jnp.arange(EXP_PER_GROUP)[None, :]
    i1 = jnp.argmax(e_in, axis=-1)
    v1 = jnp.max(e_in, axis=-1)
    rest = jnp.where(lane == i1[:, None], -jnp.inf, e_in)
    i2 = jnp.argmax(rest, axis=-1)
    v2 = jnp.max(rest, axis=-1)
    top_w = jax.nn.softmax(jnp.stack([v1, v2], axis=-1), axis=-1) * p_grp
    eid = grp[:, None] * EXP_PER_GROUP + jnp.stack([i1, i2], axis=-1)
    return eid.astype(jnp.int32), top_w


def moe_schedule(eid):
    n = eid.shape[0]
    flat_e = eid.reshape(-1)
    onehot = (flat_e[:, None] == jnp.arange(N_EXPERTS)[None, :]).astype(jnp.int32)
    rank = jnp.sum(onehot * (jnp.cumsum(onehot, axis=0) - 1), axis=1)
    counts = jnp.sum(onehot, axis=0)
    padded = ((counts + MOE_TM - 1) // MOE_TM) * MOE_TM
    pad_end = jnp.cumsum(padded)
    pad_start = pad_end - padded
    dest = (jnp.sum(onehot * pad_start[None, :], axis=1) + rank).astype(jnp.int32)
    n_rows = MOE_TILES * MOE_TM
    row_token = jnp.zeros((n_rows,), jnp.int32).at[dest].set(jnp.arange(TOP_K * n, dtype=jnp.int32) // TOP_K)
    tile_row0 = jnp.arange(MOE_TILES, dtype=jnp.int32) * MOE_TM
    n_valid = pad_end[-1] // MOE_TM
    tile_valid = (tile_row0 < pad_end[-1]).astype(jnp.int32)
    tile_expert = jnp.sum((tile_row0[:, None] >= pad_end[None, :]).astype(jnp.int32), axis=1)
    last_expert = jnp.sum(jnp.where(jnp.arange(MOE_TILES) == n_valid - 1, tile_expert, 0))
    tile_expert = jnp.where(tile_valid != 0, tile_expert, last_expert).astype(jnp.int32)
    pos = dest.reshape(n // CMB_TB, CMB_TB, TOP_K).transpose(0, 2, 1).reshape(-1)
    return tile_expert, tile_valid, row_token, pos


NA_QR = 4
NA_KR = NA_QR + NA_MAX_ROWS - 1
NA_TQ = NA_QR * GRID_W
NA_TK = NA_KR * GRID_W
NA_HP = 8
NA_WIN_BLOCKS = 3


def natten_bias_tiles(rpb):
    hi = lax.Precision.HIGHEST
    cols = jnp.arange(GRID_W)
    dc = jnp.clip(cols[None, :] - cols[:, None] + (NA_COLS - 1), 0, 2 * NA_COLS - 2)
    c0 = jnp.clip(cols - NA_COLS // 2, 0, GRID_W - NA_COLS)
    in_win = (cols[None, :] >= c0[:, None]) & (cols[None, :] < c0[:, None] + NA_COLS)
    oh_c = (dc[..., None] == jnp.arange(2 * NA_COLS - 1)).astype(jnp.float32)
    toep = jnp.einsum('qkc,hdc->hdqk', oh_c, rpb.astype(jnp.float32), precision=hi)
    rl = jnp.arange(NA_QR)[:, None]
    kl = jnp.arange(NA_KR)[None, :]
    half = NA_MAX_ROWS // 2
    pats = [(kl - rl + NA_MAX_ROWS - 1, (kl < NA_MAX_ROWS) & (rl >= 0)),
            (kl - rl + NA_MAX_ROWS - 1 - half, (kl - rl >= 0) & (kl - rl < NA_MAX_ROWS)),
            (kl - rl, (kl >= NA_KR - NA_MAX_ROWS) & (rl >= 0))]
    tiles = []
    for dr, valid in pats:
        oh_r = ((dr[..., None] == jnp.arange(2 * NA_MAX_ROWS - 1)) & valid[..., None]).astype(jnp.float32)
        t = jnp.einsum('rkd,hdqc->hrqkc', oh_r, toep, precision=hi)
        ok = valid[None, :, None, :, None] & in_win[None, None, :, None, :]
        tiles.append(jnp.where(ok, t, -jnp.inf).reshape(rpb.shape[0], NA_TQ, NA_TK))
    return jnp.stack(tiles, axis=1)


def _natten_kernel(q_ref, kc_ref, vc_ref, k0_ref, k1_ref, k2_ref, v0_ref, v1_ref, v2_ref, bias_ref, o_ref,
                   kw_ref, vw_ref, *, n_rows, n_blk, hp):
    j = pl.program_id(1)
    bf16 = jnp.bfloat16
    f32 = jnp.float32
    nt = (((1,), (1,)), ((), ()))
    heads = [slice(hh * HEAD_DIM, (hh + 1) * HEAD_DIM) for hh in range(hp)]

    def ctx_scores(cols):
        q = (q_ref[:, cols] * HEAD_DIM ** -0.5).astype(bf16)
        kc = kc_ref[:, cols].astype(bf16)
        return q, lax.dot_general(q, kc, nt, preferred_element_type=f32)

    @pl.when(j == 0)
    def _():
        for cols in heads:
            _, s_ctx = ctx_scores(cols)
            m = jnp.max(s_ctx, axis=-1, keepdims=True)
            p = jnp.exp(s_ctx - m)
            l = jnp.sum(p, axis=-1, keepdims=True)
            o = jnp.dot(p.astype(bf16), vc_ref[:, cols].astype(bf16), preferred_element_type=f32)
            o_ref[:, cols] = (o / l).astype(o_ref.dtype)

    @pl.when(j > 0)
    def _():
        for t, (k_ref, v_ref) in enumerate(((k0_ref, v0_ref), (k1_ref, v1_ref), (k2_ref, v2_ref))):
            kw_ref[t * ROW_TILE:(t + 1) * ROW_TILE, :] = k_ref[...].astype(bf16)
            vw_ref[t * ROW_TILE:(t + 1) * ROW_TILE, :] = v_ref[...].astype(bf16)
        r0 = (j - 1) * NA_QR
        kbase = jnp.clip(r0 - NA_MAX_ROWS // 2, 0, n_rows - NA_KR)
        first_blk = jnp.clip(j - 1, 1, n_blk - NA_WIN_BLOCKS)
        off = pl.multiple_of(CTX_LEN + kbase * GRID_W - first_blk * ROW_TILE, GRID_W)
        for hh, cols in enumerate(heads):
            q, s_ctx = ctx_scores(cols)
            kl = kw_ref[pl.ds(off, NA_TK), cols]
            vl = vw_ref[pl.ds(off, NA_TK), cols]
            s_loc = lax.dot_general(q, kl, nt, preferred_element_type=f32) + bias_ref[hh]
            m = jnp.maximum(jnp.max(s_loc, axis=-1, keepdims=True), jnp.max(s_ctx, axis=-1, keepdims=True))
            p_loc = jnp.exp(s_loc - m)
            p_ctx = jnp.exp(s_ctx - m)
            l = jnp.sum(p_loc, axis=-1, keepdims=True) + jnp.sum(p_ctx, axis=-1, keepdims=True)
            o = (jnp.dot(p_loc.astype(bf16), vl, preferred_element_type=f32)
                 + jnp.dot(p_ctx.astype(bf16), vc_ref[:, cols].astype(bf16), preferred_element_type=f32))
            o_ref[:, cols] = (o / l).astype(o_ref.dtype)


def natten(p_main, bias_tiles, hp=NA_HP):
    n = p_main.shape[0]
    n_rows = (n - CTX_LEN) // GRID_W
    n_blocks = n_rows // NA_QR
    n_blk = n // ROW_TILE
    assert n_rows % NA_QR == 0 and n_rows >= NA_KR and CTX_LEN == NA_TQ == ROW_TILE
    assert NA_TK + GRID_W <= NA_WIN_BLOCKS * ROW_TILE and n_blk > NA_WIN_BLOCKS
    w = hp * HEAD_DIM
    gpc = GROUP_W // w

    def pat(j):
        return jnp.where(j <= 1, 0, jnp.where(j == n_blocks, 2, 1))

    def win(col, t):
        return pl.BlockSpec((ROW_TILE, w), lambda h, j: (jnp.clip(j - 1, 1, n_blk - NA_WIN_BLOCKS) + t, col * gpc + h))

    return pl.pallas_call(
        functools.partial(_natten_kernel, n_rows=n_rows, n_blk=n_blk, hp=hp),
        grid=(gpc, n_blocks + 1),
        in_specs=[pl.BlockSpec((NA_TQ, w), lambda h, j: (j, COL_Q * gpc + h)),
                  pl.BlockSpec((CTX_LEN, w), lambda h, j: (0, COL_K * gpc + h)),
                  pl.BlockSpec((CTX_LEN, w), lambda h, j: (0, COL_V * gpc + h)),
                  win(COL_K, 0), win(COL_K, 1), win(COL_K, 2), win(COL_V, 0), win(COL_V, 1), win(COL_V, 2),
                  pl.BlockSpec((hp, None, NA_TQ, NA_TK), lambda h, j: (h, pat(j), 0, 0))],
        out_specs=pl.BlockSpec((NA_TQ, w), lambda h, j: (j, h)),
        out_shape=jax.ShapeDtypeStruct((n, GROUP_W), jnp.bfloat16),
        scratch_shapes=[pltpu.VMEM((NA_WIN_BLOCKS * ROW_TILE, w), jnp.bfloat16),
                        pltpu.VMEM((NA_WIN_BLOCKS * ROW_TILE, w), jnp.bfloat16)],
        compiler_params=_cparams("arbitrary", "arbitrary"),
        name="natten",
    )(*([p_main] * 9), bias_tiles)


POOL_HALO = max(POOL_WINDOWS) // 2


def _pool_kernel(prev_ref, cur_ref, next_ref, w_ref, s_ref, o_ref, ext_ref, *, n_tok):
    i = pl.program_id(0)
    t0 = i * ROW_TILE
    seq_lo = jnp.where(t0 < CTX_LEN, 0, CTX_LEN)
    seq_hi = jnp.where(t0 < CTX_LEN, CTX_LEN, n_tok)
    ext_ref[0:POOL_HALO, :] = jnp.where(t0 - POOL_HALO >= seq_lo, prev_ref[...], 0.0)
    ext_ref[POOL_HALO:POOL_HALO + ROW_TILE, :] = cur_ref[...]
    ext_ref[POOL_HALO + ROW_TILE:, :] = jnp.where(t0 + ROW_TILE < seq_hi, next_ref[...], 0.0)
    t = t0 + lax.broadcasted_iota(jnp.int32, (ROW_TILE, 1), 0)
    for g, win in enumerate(POOL_WINDOWS):
        half = win // 2
        cs = slice(g * C_GW, (g + 1) * C_GW)
        acc = ext_ref[POOL_HALO - half:POOL_HALO - half + ROW_TILE, cs]
        for d in range(-half + 1, half):
            acc = acc + ext_ref[POOL_HALO + d:POOL_HALO + d + ROW_TILE, cs]
        cnt = (jnp.minimum(t + half, seq_hi) - jnp.maximum(t - half, seq_lo)).astype(jnp.float32)
        diff = acc / cnt - cur_ref[:, cs]
        y = jnp.dot(diff.astype(jnp.bfloat16), w_ref[g], preferred_element_type=jnp.float32)
        o_ref[:, cs] = (y * s_ref[:, cs]).astype(o_ref.dtype)


def pool_mix(p_main, w_pool, pool_scale):
    n = p_main.shape[0]
    hb = ROW_TILE // POOL_HALO
    n_hblk = n // POOL_HALO
    return pl.pallas_call(
        functools.partial(_pool_kernel, n_tok=n),
        grid=(n // ROW_TILE,),
        in_specs=[pl.BlockSpec((POOL_HALO, GROUP_W), lambda i: (jnp.maximum(i * hb - 1, 0), COL_C)),
                  pl.BlockSpec((ROW_TILE, GROUP_W), lambda i: (i, COL_C)),
                  pl.BlockSpec((POOL_HALO, GROUP_W), lambda i: (jnp.minimum((i + 1) * hb, n_hblk - 1), COL_C)),
                  pl.BlockSpec((C_GROUPS, C_GW, C_GW), lambda i: (0, 0, 0)),
                  pl.BlockSpec((1, GROUP_W), lambda i: (0, 0))],
        out_specs=pl.BlockSpec((ROW_TILE, GROUP_W), lambda i: (i, 0)),
        out_shape=jax.ShapeDtypeStruct((n, GROUP_W), jnp.bfloat16),
        scratch_shapes=[pltpu.VMEM((ROW_TILE + 2 * POOL_HALO, GROUP_W), jnp.float32)],
        compiler_params=_cparams("arbitrary"),
        name="pool_mix",
    )(p_main, p_main, p_main, w_pool, pool_scale.reshape(1, GROUP_W))


def chunk_masks(chunk, reverse):
    i = jnp.arange(ROW_TILE)[:, None]
    j = jnp.arange(ROW_TILE)[None, :]
    same = (i // chunk) == (j // chunk)
    return (same & ((j >= i) if reverse else (j <= i))).astype(jnp.bfloat16)


def _split3(x):
    hi = x.astype(jnp.bfloat16)
    r = x - hi.astype(jnp.float32)
    mid = r.astype(jnp.bfloat16)
    lo = (r - mid.astype(jnp.float32)).astype(jnp.bfloat16)
    return hi, mid, lo


def _dot3(m, x):
    hi, mid, lo = _split3(x)
    d = functools.partial(jnp.dot, preferred_element_type=jnp.float32)
    return d(m, hi) + (d(m, mid) + d(m, lo))


def _scan_block(n_blk, reverse):
    if reverse:
        return lambda j: jnp.where(j == 0, 0, n_blk - j)
    return lambda j: j


def _hgrn_head(q_ref, v_ref, z_ref, lb_ref, mask_ref, st, cols, reverse):
    bf16 = jnp.bfloat16
    f32 = jnp.float32
    T, C = ROW_TILE, A_CHUNK
    nt = (((1,), (1,)), ((), ()))
    tn = (((0,), (0,)), ((), ()))
    z = z_ref[:, cols]
    log_lb = lb_ref[0:1, cols]
    log1m_lb = lb_ref[1:2, cols]
    one_m_lb = lb_ref[2:3, cols]
    log_sig = jnp.minimum(z, 0.0) - jnp.log1p(jnp.exp(-jnp.abs(z)))
    bb = log1m_lb + log_sig
    mx = jnp.maximum(log_lb, bb)
    lf = mx + jnp.log1p(jnp.exp(-jnp.abs(log_lb - bb)))
    k = one_m_lb / (1.0 + jnp.exp(z))
    aq = q_ref[:, cols]
    q = aq / (1.0 + jnp.exp(-aq))
    v = v_ref[:, cols].astype(bf16)

    b = _dot3(mask_ref[...], lf)
    last = 0 if reverse else C - 1
    b_end = jnp.concatenate([jnp.broadcast_to(b[c * C + last:c * C + last + 1, :], (C, HEAD_DIM))
                             for c in range(T // C)], axis=0)
    q_dec = (q * jnp.exp(b)).astype(bf16)
    k_inv = (k * jnp.exp(-b)).astype(bf16)
    k_dec = (k * jnp.exp(b_end - b)).astype(bf16)
    dec = jnp.exp(b_end)

    scores = lax.dot_general(q_dec, k_inv, nt, preferred_element_type=f32)
    scores = jnp.where(mask_ref[...] > 0, scores, 0.0)
    o_intra = jnp.dot(scores.astype(bf16), v, preferred_element_type=f32)

    n_c = T // C
    rows = [slice(c * C, (c + 1) * C) for c in range(n_c)]
    deltas = [lax.dot_general(v[r], k_dec[r], tn, preferred_element_type=f32) for r in rows]
    entering = [None] * n_c
    for c in (range(n_c - 1, -1, -1) if reverse else range(n_c)):
        entering[c] = st
        st = st * dec[c * C:c * C + 1, :] + deltas[c]
    outs = [lax.dot_general(q_dec[r], entering[c].astype(bf16), nt, preferred_element_type=f32)
            for c, r in enumerate(rows)]
    return o_intra + jnp.concatenate(outs, axis=0), st


def _hgrn_kernel(q_ref, v_ref, z_ref, lb_ref, mask_ref, *rest, reverse, hp):
    if reverse:
        g_ref, of_ref, gn_ref, o_ref, st_ref = rest
    else:
        o_ref, st_ref = rest

    @pl.when(pl.program_id(1) == 0)
    def _():
        st_ref[...] = jnp.zeros_like(st_ref)

    for hh in range(hp):
        cols = slice(hh * HEAD_DIM, (hh + 1) * HEAD_DIM)
        o, st = _hgrn_head(q_ref, v_ref, z_ref, lb_ref, mask_ref, st_ref[hh], cols, reverse)
        st_ref[hh] = st
        if reverse:
            o = o + of_ref[:, cols]
            o = o * lax.rsqrt(jnp.mean(o * o, axis=-1, keepdims=True) + EPS) * gn_ref[:, cols]
            ag = g_ref[:, cols]
            o_ref[:, cols] = (o * (ag / (1.0 + jnp.exp(-ag)))).astype(o_ref.dtype)
        else:
            o_ref[:, cols] = o


def hgrn_scan(p_main, lb_rows, mask, reverse, o_fwd=None, gn=None, hp=SCAN_HP):
    n = p_main.shape[0]
    n_blk = n // ROW_TILE
    tb = _scan_block(n_blk, reverse)
    w = hp * HEAD_DIM
    gpc = GROUP_W // w
    blk = lambda c: pl.BlockSpec((ROW_TILE, w), lambda h, j: (tb(j), c * gpc + h))
    in_specs = [blk(COL_AQ), blk(COL_AI), blk(COL_AFB if reverse else COL_AFF),
                pl.BlockSpec((3, w), lambda h, j: (0, h)),
                pl.BlockSpec((ROW_TILE, ROW_TILE), lambda h, j: (0, 0))]
    args = [p_main, p_main, p_main, lb_rows, mask]
    if reverse:
        in_specs += [blk(COL_AG), pl.BlockSpec((ROW_TILE, w), lambda h, j: (tb(j), h)),
                     pl.BlockSpec((1, w), lambda h, j: (0, h))]
        args += [p_main, o_fwd, gn.reshape(1, GROUP_W)]
    return pl.pallas_call(
        functools.partial(_hgrn_kernel, reverse=reverse, hp=hp),
        grid=(gpc, n_blk),
        in_specs=in_specs,
        out_specs=pl.BlockSpec((ROW_TILE, w), lambda h, j: (tb(j), h)),
        out_shape=jax.ShapeDtypeStruct((n, GROUP_W), jnp.bfloat16 if reverse else jnp.float32),
        scratch_shapes=[pltpu.VMEM((hp, HEAD_DIM, HEAD_DIM), jnp.float32)],
        compiler_params=_cparams("arbitrary", "arbitrary"),
        name="hgrn_bwd" if reverse else "hgrn_fwd",
    )(*args)


def hgrn_mix(p_main, lb_f, lb_b, gn):
    rows = lambda lb: jnp.stack([jnp.log(lb), jnp.log1p(-lb), 1.0 - lb])
    o_f = hgrn_scan(p_main, rows(lb_f), chunk_masks(A_CHUNK, False), False)
    return hgrn_scan(p_main, rows(lb_b), chunk_masks(A_CHUNK, True), True, o_f, gn)


def hgrn_lower_bounds(p):
    cs = jnp.cumsum(jax.nn.softmax(p.astype(jnp.float32), axis=1), axis=1)
    return cs - cs[:, :1]


def rope_tables(n_tok):
    quarter = HEAD_DIM // 4
    pos = jnp.arange(n_tok - CTX_LEN)
    inv_freq = ROPE_BASE ** (-jnp.arange(quarter, dtype=jnp.float32) / quarter)
    lane = jnp.arange(HEAD_DIM)
    p = jnp.where(lane[None, :] < HEAD_DIM // 2, (pos // GRID_W)[:, None], (pos % GRID_W)[:, None]).astype(jnp.float32)
    ang = p * inv_freq[lane % quarter][None, :]
    sign = jnp.where((lane % (2 * quarter)) < quarter, -1.0, 1.0)
    cos = jnp.concatenate([jnp.ones((CTX_LEN, HEAD_DIM), jnp.float32), jnp.cos(ang)], axis=0)
    sin = jnp.concatenate([jnp.zeros((CTX_LEN, HEAD_DIM), jnp.float32), jnp.sin(ang) * sign], axis=0)
    return cos, sin


def _rope(x, cos, sin):
    quarter = HEAD_DIM // 4
    lane = lax.broadcasted_iota(jnp.int32, x.shape, 1)
    partner = jnp.where((lane % (2 * quarter)) < quarter,
                        pltpu.roll(x, HEAD_DIM - quarter, axis=1), pltpu.roll(x, quarter, axis=1))
    return x * cos + partner * sin


def _mlstm_head(q_ref, k_ref, v_ref, cos, sin, cols, b_bc, li_bc, b_row, li_row, state, reverse):
    bf16 = jnp.bfloat16
    f32 = jnp.float32
    T, C = ROW_TILE, D_CHUNK
    n_c = T // C
    nt = (((1,), (1,)), ((), ()))
    tn = (((0,), (0,)), ((), ()))
    d = functools.partial(jnp.dot, preferred_element_type=f32)
    q = _rope(q_ref[:, cols], cos, sin) * HEAD_DIM ** -0.5
    k = _rope(k_ref[:, cols], cos, sin)
    qb = q.astype(bf16)
    kb = k.astype(bf16)
    vb = v_ref[:, cols].astype(bf16)
    ii = lax.broadcasted_iota(jnp.int32, (C, C), 0)
    jj = lax.broadcasted_iota(jnp.int32, (C, C), 1)
    causal = (jj >= ii) if reverse else (jj <= ii)
    last = 0 if reverse else C - 1
    rows = [slice(c * C, (c + 1) * C) for c in range(n_c)]
    bends, m_locs, d_ss, d_ns = [], [], [], []
    for c, r in enumerate(rows):
        bend = b_bc[c * C + last:c * C + last + 1]
        gcol = bend - b_bc[r] + li_bc[r]
        m_loc = jnp.max(gcol, axis=0, keepdims=True)
        kw = k[r] * jnp.exp(gcol - m_loc)
        bends.append(bend)
        m_locs.append(m_loc)
        d_ss.append(lax.dot_general(kw.astype(bf16), vb[r], tn, preferred_element_type=f32))
        d_ns.append(jnp.sum(kw, axis=0, keepdims=True))
    s, nv, m = state
    entering = [None] * n_c
    for c in (range(n_c - 1, -1, -1) if reverse else range(n_c)):
        entering[c] = (s, nv, m)
        m_new = jnp.maximum(bends[c] + m, m_locs[c])
        a = jnp.exp(bends[c] + m - m_new)
        cc = jnp.exp(m_locs[c] - m_new)
        s = a * s + cc * d_ss[c]
        nv = a * nv + cc * d_ns[c]
        m = m_new
    outs = []
    for c, r in enumerate(rows):
        s_in, n_in, m_in = entering[c]
        dmat = jnp.where(causal, b_bc[r][:, 0:C] - b_row[:, r] + li_row[:, r], -jnp.inf)
        m_inter = b_bc[r] + m_in
        m_t = jnp.maximum(m_inter, jnp.max(dmat, axis=-1, keepdims=True))
        w = jnp.exp(dmat - m_t[:, 0:C]) * lax.dot_general(qb[r], kb[r], nt, preferred_element_type=f32)
        a_col = jnp.exp(m_inter - m_t)
        num = d(w.astype(bf16), vb[r]) + a_col * d(qb[r], s_in.astype(bf16))
        den = jnp.sum(w, axis=-1, keepdims=True) + a_col * jnp.sum(q[r] * n_in, axis=-1, keepdims=True)
        outs.append(num / jnp.maximum(jnp.abs(den), jnp.exp(-m_t)))
    return jnp.concatenate(outs, axis=0), (s, nv, m)


def _mlstm_kernel(q_ref, k_ref, v_ref, g_ref, cos_ref, sin_ref, mask_ref, maskt_ref, *rest, reverse, hp):
    if reverse:
        do_ref, of_ref, gn_ref, o_ref, s_ref, n_ref, m_ref, cumt_ref, lit_ref = rest
    else:
        o_ref, s_ref, n_ref, m_ref, cumt_ref, lit_ref = rest
    hg = pl.program_id(0)
    f32 = jnp.float32
    bf16 = jnp.bfloat16

    @pl.when(pl.program_id(1) == 0)
    def _():
        s_ref[...] = jnp.zeros_like(s_ref)
        n_ref[...] = jnp.zeros_like(n_ref)
        m_ref[...] = jnp.zeros_like(m_ref)

    g = g_ref[...]
    lsg = jnp.minimum(g, 0.0) - jnp.log1p(jnp.exp(-jnp.abs(g)))
    cum = _dot3(mask_ref[...], lsg)
    hi, mid, lo = _split3(lsg.T)
    d = functools.partial(jnp.dot, preferred_element_type=f32)
    cumt_ref[...] = d(hi, maskt_ref[...]) + (d(mid, maskt_ref[...]) + d(lo, maskt_ref[...]))
    lit_ref[...] = g.T
    cum_parts = _split3(cum)
    g_parts = _split3(g)
    sel_row = lax.broadcasted_iota(jnp.int32, (LANES, LANES), 0)

    def replicate(parts, col):
        sel = (sel_row == col).astype(bf16)
        return d(parts[0], sel) + (d(parts[1], sel) + d(parts[2], sel))

    cos = cos_ref[...]
    sin = sin_ref[...]
    for hh in range(hp):
        h = hg * hp + hh
        col_i = (GATE_I_B if reverse else GATE_I_F) * N_HEADS + h
        col_f = (GATE_F_B if reverse else GATE_F_F) * N_HEADS + h
        b_bc = replicate(cum_parts, col_f)
        li_bc = replicate(g_parts, col_i)
        b_row = cumt_ref[pl.ds(col_f, 1), :]
        li_row = lit_ref[pl.ds(col_i, 1), :]
        cols = slice(hh * HEAD_DIM, (hh + 1) * HEAD_DIM)
        state = (s_ref[hh], n_ref[hh, 0:1, :], m_ref[hh, 0:1, :])
        o, (s, nv, m) = _mlstm_head(q_ref, k_ref, v_ref, cos, sin, cols, b_bc, li_bc, b_row, li_row, state, reverse)
        s_ref[hh] = s
        n_ref[hh] = jnp.broadcast_to(nv, n_ref.shape[1:])
        m_ref[hh] = jnp.broadcast_to(m, m_ref.shape[1:])
        if reverse:
            o = o + of_ref[:, cols]
            o = o * lax.rsqrt(jnp.mean(o * o, axis=-1, keepdims=True) + EPS) * gn_ref[:, cols]
            o_ref[:, cols] = (o / (1.0 + jnp.exp(-do_ref[:, cols]))).astype(o_ref.dtype)
        else:
            o_ref[:, cols] = o


def mlstm_scan(p_main, p_gate, cos, sin, mask, reverse, o_fwd=None, gn=None, hp=SCAN_HP):
    n = p_main.shape[0]
    n_blk = n // ROW_TILE
    tb = _scan_block(n_blk, reverse)
    w = hp * HEAD_DIM
    gpc = GROUP_W // w
    blk = lambda c: pl.BlockSpec((ROW_TILE, w), lambda h, j: (tb(j), c * gpc + h))
    tok = pl.BlockSpec((ROW_TILE, LANES), lambda h, j: (tb(j), 0))
    sq = pl.BlockSpec((ROW_TILE, ROW_TILE), lambda h, j: (0, 0))
    in_specs = [blk(COL_DQ), blk(COL_DK), blk(COL_DV), tok, tok, tok, sq, sq]
    args = [p_main, p_main, p_main, p_gate, cos, sin, mask, mask.T]
    if reverse:
        in_specs += [blk(COL_DO), pl.BlockSpec((ROW_TILE, w), lambda h, j: (tb(j), h)),
                     pl.BlockSpec((1, w), lambda h, j: (0, h))]
        args += [p_main, o_fwd, gn.reshape(1, GROUP_W)]
    return pl.pallas_call(
        functools.partial(_mlstm_kernel, reverse=reverse, hp=hp),
        grid=(gpc, n_blk),
        in_specs=in_specs,
        out_specs=pl.BlockSpec((ROW_TILE, w), lambda h, j: (tb(j), h)),
        out_shape=jax.ShapeDtypeStruct((n, GROUP_W), jnp.bfloat16 if reverse else jnp.float32),
        scratch_shapes=[pltpu.VMEM((hp, HEAD_DIM, HEAD_DIM), jnp.float32), pltpu.VMEM((hp, 8, HEAD_DIM), jnp.float32),
                        pltpu.VMEM((hp, 8, LANES), jnp.float32), pltpu.VMEM((LANES, ROW_TILE), jnp.float32),
                        pltpu.VMEM((LANES, ROW_TILE), jnp.float32)],
        compiler_params=_cparams("arbitrary", "arbitrary"),
        name="mlstm_bwd" if reverse else "mlstm_fwd",
    )(*args)


def mlstm_mix(p_main, p_gate, gn):
    cos, sin = rope_tables(p_main.shape[0])
    o_f = mlstm_scan(p_main, p_gate, cos, sin, chunk_masks(D_CHUNK, False), False)
    return mlstm_scan(p_main, p_gate, cos, sin, chunk_masks(D_CHUNK, True), True, o_f, gn)


def kernel(x, c, ctx, c_ctx, ada_w, ada_b, norm1_w, norm2_w, mix_w_in, mix_b_in, hgrn_lb, hgrn_gn, natten_rpb,
           pool_w, pool_scale, mlstm_gn, mix_w_out, moe_wg, moe_bg, moe_we, moe_be, moe_w1, moe_w3, moe_w2, final_w):
    assert x.shape == (1, SEQ, D_MODEL) and ctx.shape == (1, CTX_LEN, D_MODEL)
    bf16 = jnp.bfloat16
    lbs = hgrn_lower_bounds(hgrn_lb)

    cond = jnp.zeros((16, D_MODEL), jnp.float32).at[0].set(jax.nn.silu(c_ctx)).at[1].set(jax.nn.silu(c[0]))
    mods = ada_modulation(cond.astype(bf16), ada_w, ada_b)[:, :2].reshape(DEPTH, 2, 6, D_MODEL)

    w_in_t = jnp.swapaxes(mix_w_in, 1, 2)
    w_in = cast_bf16_transposed(w_in_t, N_MAIN)
    w_out = cast_bf16(mix_w_out, 4 * LANES)

    xs = jnp.concatenate([ctx[0], x[0]], axis=0)
    for l in range(DEPTH):
        mod = mods[l]
        h = norm_modulate(xs, norm1_w[l], mod, 0)
        p_main, w1, w3, w2 = in_proj_cast(h, w_in, mix_b_in[l], l, moe_w1, moe_w3, moe_w2)
        w_gate_t = jnp.pad(w_in_t[l, N_MAIN:, :], ((0, LANES - N_GATE), (0, 0)))
        b_gate = jnp.pad(mix_b_in[l, N_MAIN:], (0, LANES - N_GATE))
        p_gate = matmul_bias_nt(h, w_gate_t, b_gate, MM_TM)

        out_a = hgrn_mix(p_main, lbs[0, l], lbs[1, l], hgrn_gn[l])
        out_b = natten(p_main, natten_bias_tiles(natten_rpb[l]))
        out_c = pool_mix(p_main, pool_w[l].astype(bf16), pool_scale[l])
        out_d = mlstm_mix(p_main, p_gate, mlstm_gn[l])
        xs = matmul_gated_residual((out_a, out_b, out_c, out_d), w_out, l, xs, mod, 2)

        w_router = jnp.pad(jnp.concatenate([moe_wg[l], moe_we[l]], axis=1),
                           ((0, 0), (0, LANES - N_GROUPS - N_EXPERTS)))
        h2, logits = norm_modulate_router(xs, norm2_w[l], mod, w_router)
        eid, top_w = moe_route(logits, moe_bg[l], moe_be[l])
        tile_expert, tile_valid, row_token, cpos = moe_schedule(eid)
        ys = moe_grouped(h2, tile_expert, tile_valid, row_token, w1, w3, w2)
        xs = moe_combine(cpos, ys, xs, top_w, mod, 5)

    return norm_modulate(xs, final_w, mods[0], None, out_dtype=jnp.float32, skip_rows=CTX_LEN)[None]
```

```python
import functools

import jax
import jax.numpy as jnp
from jax import lax
from jax.experimental import pallas as pl
from jax.experimental.pallas import tpu as pltpu

D_MODEL = 4096
SEQ = 8192
DEPTH = 2
GRID_W = 64
CTX_LEN = 256
N_TOK = CTX_LEN + SEQ
GROUP_W = 1024
HEAD_DIM = 128
N_HEADS = GROUP_W // HEAD_DIM
C_GROUPS = 4
C_GW = GROUP_W // C_GROUPS
POOL_WINDOWS = (2, 4, 8, 16)
NA_MAX_ROWS = 8
NA_COLS = 16
A_CHUNK = 32
D_CHUNK = 64
ROPE_BASE = 10000.0
N_GROUPS = 4
EXP_PER_GROUP = 8
N_EXPERTS = N_GROUPS * EXP_PER_GROUP
TOP_K = 2
D_FF = D_MODEL // 8
EPS = 1e-6
N_MAIN = 13 * GROUP_W
N_GATE = 4 * N_HEADS
LANES = 128
VMEM_LIMIT = 56 * 1024 * 1024

ROW_TILE = 256
MM_TM = 768
MM_TN = 512
MOE_TM = 256
MOE_TILES = (TOP_K * N_TOK + N_EXPERTS * (MOE_TM - 1)) // MOE_TM + 1
CMB_TB = 128
SCAN_HP = 8
DMA_UNROLL = 8
CAST_SLABS = 8
CT_TN, CT_TK = 1024, 1024
MOE_AHEAD = 3
MOE_SLOTS = MOE_AHEAD + 1

COL_AQ, COL_AI, COL_AG, COL_AFF, COL_AFB = 0, 1, 2, 3, 4
COL_Q, COL_K, COL_V, COL_C = 5, 6, 7, 8
COL_DQ, COL_DK, COL_DV, COL_DO = 9, 10, 11, 12
GATE_I_F, GATE_F_F, GATE_I_B, GATE_F_B = 0, 1, 2, 3
HPG = GROUP_W // HEAD_DIM


def _cparams(*sem):
    return pltpu.CompilerParams(dimension_semantics=sem, vmem_limit_bytes=VMEM_LIMIT)


def _ada_kernel(c_ref, w_ref, b_ref, o_ref):
    acc = jnp.dot(c_ref[...], w_ref[...].astype(jnp.bfloat16), preferred_element_type=jnp.float32)
    o_ref[...] = acc + b_ref[...]


def ada_modulation(cond, ada_w, ada_b):
    tn = 512
    n = ada_w.shape[-1]
    rows = cond.shape[0]
    return pl.pallas_call(
        _ada_kernel,
        grid=(DEPTH, n // tn),
        in_specs=[pl.BlockSpec((rows, D_MODEL), lambda l, j: (0, 0)),
                  pl.BlockSpec((None, D_MODEL, tn), lambda l, j: (l, 0, j)),
                  pl.BlockSpec((None, 1, tn), lambda l, j: (l, 0, j))],
        out_specs=pl.BlockSpec((None, rows, tn), lambda l, j: (l, 0, j)),
        out_shape=jax.ShapeDtypeStruct((DEPTH, rows, n), jnp.float32),
        compiler_params=_cparams("arbitrary", "arbitrary"),
        name="ada_modulation",
    )(cond, ada_w, ada_b.reshape(DEPTH, 1, n))


def _norm_kernel(x_ref, w_ref, mod_ref, o_ref, *, shift_row, scale_row):
    x = x_ref[...]
    y = x * lax.rsqrt(jnp.mean(x * x, axis=-1, keepdims=True) + EPS) * w_ref[...]
    if shift_row is not None:
        y = y * (1.0 + mod_ref[scale_row:scale_row + 1, :]) + mod_ref[shift_row:shift_row + 1, :]
    o_ref[...] = y.astype(o_ref.dtype)


def _norm_router_kernel(x_ref, w_ref, mod_ref, wr_ref, o_ref, lg_ref, *, shift_row, scale_row):
    x = x_ref[...]
    y = x * lax.rsqrt(jnp.mean(x * x, axis=-1, keepdims=True) + EPS) * w_ref[...]
    y = y * (1.0 + mod_ref[scale_row:scale_row + 1, :]) + mod_ref[shift_row:shift_row + 1, :]
    o_ref[...] = y.astype(o_ref.dtype)
    y_hi = y.astype(jnp.bfloat16)
    y_lo = (y - y_hi.astype(jnp.float32)).astype(jnp.bfloat16)
    w = wr_ref[...]
    w_hi = w.astype(jnp.bfloat16)
    w_lo = (w - w_hi.astype(jnp.float32)).astype(jnp.bfloat16)
    dot = functools.partial(jnp.dot, preferred_element_type=jnp.float32)
    lg_ref[...] = dot(y_hi, w_hi) + (dot(y_lo, w_hi) + dot(y_hi, w_lo))


def _tok_type(i):
    return jnp.where(i * ROW_TILE >= CTX_LEN, 1, 0)


def norm_modulate(x, w, mod, which, out_dtype=jnp.bfloat16, skip_rows=0):
    n = x.shape[0] - skip_rows
    off = skip_rows // ROW_TILE
    rows = (None, None) if which is None else (3 * which, 3 * which + 1)
    return pl.pallas_call(
        functools.partial(_norm_kernel, shift_row=rows[0], scale_row=rows[1]),
        grid=(n // ROW_TILE,),
        in_specs=[pl.BlockSpec((ROW_TILE, D_MODEL), lambda i: (i + off, 0)),
                  pl.BlockSpec((1, D_MODEL), lambda i: (0, 0)),
                  pl.BlockSpec((None, 6, D_MODEL), lambda i: (_tok_type(i + off), 0, 0))],
        out_specs=pl.BlockSpec((ROW_TILE, D_MODEL), lambda i: (i, 0)),
        out_shape=jax.ShapeDtypeStruct((n, D_MODEL), out_dtype),
        compiler_params=_cparams("arbitrary"),
        name="norm_modulate",
    )(x, w.reshape(1, D_MODEL), mod)


def norm_modulate_router(x, w, mod, w_router):
    n = x.shape[0]
    return pl.pallas_call(
        functools.partial(_norm_router_kernel, shift_row=3, scale_row=4),
        grid=(n // ROW_TILE,),
        in_specs=[pl.BlockSpec((ROW_TILE, D_MODEL), lambda i: (i, 0)),
                  pl.BlockSpec((1, D_MODEL), lambda i: (0, 0)),
                  pl.BlockSpec((None, 6, D_MODEL), lambda i: (_tok_type(i), 0, 0)),
                  pl.BlockSpec((D_MODEL, LANES), lambda i: (0, 0))],
        out_specs=[pl.BlockSpec((ROW_TILE, D_MODEL), lambda i: (i, 0)),
                   pl.BlockSpec((ROW_TILE, LANES), lambda i: (i, 0))],
        out_shape=[jax.ShapeDtypeStruct((n, D_MODEL), jnp.float32),
                   jax.ShapeDtypeStruct((n, LANES), jnp.float32)],
        compiler_params=_cparams("arbitrary"),
        name="norm_modulate_router",
    )(x, w.reshape(1, D_MODEL), mod, w_router)


def _mm_bias_nt_kernel(a_ref, bt_ref, bias_ref, o_ref):
    bt = bt_ref[...].astype(jnp.bfloat16)
    acc = lax.dot_general(a_ref[...], bt, (((1,), (1,)), ((), ())), preferred_element_type=jnp.float32)
    o_ref[...] = acc + bias_ref[...]


def _mm_resid_kernel(a0_ref, a1_ref, a2_ref, a3_ref, b_ref, x_ref, mod_ref, o_ref, *, gate_row):
    acc = None
    for g, a_ref in enumerate((a0_ref, a1_ref, a2_ref, a3_ref)):
        part = jnp.dot(a_ref[...], b_ref[g * GROUP_W:(g + 1) * GROUP_W, :], preferred_element_type=jnp.float32)
        acc = part if acc is None else acc + part
    tm = x_ref.shape[0]
    row = pl.program_id(0) * tm + lax.broadcasted_iota(jnp.int32, (tm, 1), 0)
    gate = jnp.where(row < CTX_LEN, mod_ref[0, gate_row:gate_row + 1, :], mod_ref[1, gate_row:gate_row + 1, :])
    o_ref[...] = x_ref[...] + gate * acc


def matmul_bias_nt(a, bt, bias, tm):
    m, k = a.shape
    n = bt.shape[0]
    return pl.pallas_call(
        _mm_bias_nt_kernel,
        grid=(m // tm,),
        in_specs=[pl.BlockSpec((tm, k), lambda i: (i, 0)),
                  pl.BlockSpec((n, k), lambda i: (0, 0)),
                  pl.BlockSpec((1, n), lambda i: (0, 0))],
        out_specs=pl.BlockSpec((tm, n), lambda i: (i, 0)),
        out_shape=jax.ShapeDtypeStruct((m, n), jnp.float32),
        compiler_params=_cparams("arbitrary"),
        name="matmul_bias_nt",
    )(a, bt, bias.reshape(1, n))


def matmul_gated_residual(parts, b, layer, x, mod, gate_row):
    m, n = x.shape
    tm, tn = MM_TM, MM_TN
    a_spec = pl.BlockSpec((tm, GROUP_W), lambda i, j: (i, 0))
    return pl.pallas_call(
        functools.partial(_mm_resid_kernel, gate_row=gate_row),
        grid=(m // tm, n // tn),
        in_specs=[a_spec, a_spec, a_spec, a_spec,
                  pl.BlockSpec((None, b.shape[1], tn), lambda i, j: (layer, 0, j)),
                  pl.BlockSpec((tm, tn), lambda i, j: (i, j)),
                  pl.BlockSpec((2, 6, tn), lambda i, j: (0, 0, j))],
        out_specs=pl.BlockSpec((tm, tn), lambda i, j: (i, j)),
        out_shape=jax.ShapeDtypeStruct((m, n), jnp.float32),
        compiler_params=_cparams("arbitrary", "arbitrary"),
        name="matmul_gated_residual",
    )(*parts, b, x, mod)


def _cast_kernel(x_ref, o_ref):
    o_ref[...] = x_ref[...].astype(o_ref.dtype)


def cast_bf16(w, rows):
    nl, r, c = w.shape
    spec = pl.BlockSpec((None, rows, c), lambda l, i: (l, i, 0))
    return pl.pallas_call(
        _cast_kernel,
        grid=(nl, r // rows),
        in_specs=[spec],
        out_specs=spec,
        out_shape=jax.ShapeDtypeStruct(w.shape, jnp.bfloat16),
        compiler_params=_cparams("arbitrary", "arbitrary"),
        name="cast_bf16",
    )(w)


def _cast_t_kernel(x_ref, o_ref):
    o_ref[...] = x_ref[...].T.astype(o_ref.dtype)


def cast_bf16_transposed(wt, n_cols):
    nl, _, k = wt.shape
    return pl.pallas_call(
        _cast_t_kernel,
        grid=(nl, k // CT_TK, n_cols // CT_TN),
        in_specs=[pl.BlockSpec((None, CT_TN, CT_TK), lambda l, i, j: (l, j, i))],
        out_specs=pl.BlockSpec((None, CT_TK, CT_TN), lambda l, i, j: (l, i, j)),
        out_shape=jax.ShapeDtypeStruct((nl, k, n_cols), jnp.bfloat16),
        compiler_params=_cparams("arbitrary", "arbitrary", "arbitrary"),
        name="cast_bf16_transposed",
    )(wt)


def _in_proj_cast_kernel(a_ref, b_ref, bias_ref, w1_ref, w3_ref, w2_ref, o_ref, w1o_ref, w3o_ref, w2o_ref):
    o_ref[...] = jnp.dot(a_ref[...], b_ref[...], preferred_element_type=jnp.float32) + bias_ref[...]
    w1o_ref[...] = w1_ref[...].astype(w1o_ref.dtype)
    w3o_ref[...] = w3_ref[...].astype(w3o_ref.dtype)
    w2o_ref[...] = w2_ref[...].astype(w2o_ref.dtype)


def in_proj_cast(a, b, bias, layer, w1, w3, w2):
    m, k = a.shape
    n, tm, tn = N_MAIN, MM_TM, MM_TN
    n_j = n // tn
    n_slabs = N_EXPERTS * CAST_SLABS
    assert (m // tm) * n_j >= n_slabs
    r13, r2 = D_MODEL // CAST_SLABS, D_FF // CAST_SLABS

    def slab(i, j):
        s = jnp.minimum(i * n_j + j, n_slabs - 1)
        return s // CAST_SLABS, s % CAST_SLABS

    in13 = pl.BlockSpec((None, None, r13, D_FF), lambda i, j: (layer, *slab(i, j), 0))
    in2 = pl.BlockSpec((None, None, r2, D_MODEL), lambda i, j: (layer, *slab(i, j), 0))
    out13 = pl.BlockSpec((None, r13, D_FF), lambda i, j: (*slab(i, j), 0))
    out2 = pl.BlockSpec((None, r2, D_MODEL), lambda i, j: (*slab(i, j), 0))
    bf16 = jnp.bfloat16
    return pl.pallas_call(
        _in_proj_cast_kernel,
        grid=(m // tm, n_j),
        in_specs=[pl.BlockSpec((tm, k), lambda i, j: (i, 0)),
                  pl.BlockSpec((None, k, tn), lambda i, j: (layer, 0, j)),
                  pl.BlockSpec((1, tn), lambda i, j: (0, j)),
                  in13, in13, in2],
        out_specs=[pl.BlockSpec((tm, tn), lambda i, j: (i, j)), out13, out13, out2],
        out_shape=[jax.ShapeDtypeStruct((m, n), jnp.float32),
                   jax.ShapeDtypeStruct((N_EXPERTS, D_MODEL, D_FF), bf16),
                   jax.ShapeDtypeStruct((N_EXPERTS, D_MODEL, D_FF), bf16),
                   jax.ShapeDtypeStruct((N_EXPERTS, D_FF, D_MODEL), bf16)],
        compiler_params=_cparams("arbitrary", "arbitrary"),
        name="in_proj_cast",
    )(a, b, bias.reshape(1, -1), w1, w3, w2)


def _moe_kernel(te_ref, tv_ref, rt_ref, h_hbm, w1_ref, w3_ref, w2_ref, y_ref, xg_ref, sem_ref):
    del te_ref
    i = pl.program_id(0)
    n_tiles = pl.num_programs(0)

    def row_copy(tile, slot, r):
        tok = rt_ref[tile * MOE_TM + r]
        return pltpu.make_async_copy(h_hbm.at[pl.ds(tok, 1), :], xg_ref.at[slot, pl.ds(r, 1), :], sem_ref.at[slot])

    def start_gather(tile, slot):
        @pl.when(tv_ref[tile] != 0)
        def _():
            def body(r, carry):
                row_copy(tile, slot, r).start(priority=1)
                return carry
            lax.fori_loop(0, MOE_TM, body, 0, unroll=DMA_UNROLL)

    @pl.when(i == 0)
    def _():
        for t in range(MOE_AHEAD):
            start_gather(t, t)

    @pl.when(i + MOE_AHEAD < n_tiles)
    def _():
        start_gather(i + MOE_AHEAD, (i + MOE_AHEAD) % MOE_SLOTS)

    @pl.when(tv_ref[i] != 0)
    def _():
        slot = i % MOE_SLOTS

        def wait_body(r, carry):
            row_copy(i, slot, r).wait()
            return carry
        lax.fori_loop(0, MOE_TM, wait_body, 0, unroll=DMA_UNROLL)
        xg = xg_ref[slot].astype(jnp.bfloat16)
        a = jnp.dot(xg, w1_ref[...], preferred_element_type=jnp.float32)
        b = jnp.dot(xg, w3_ref[...], preferred_element_type=jnp.float32)
        act = (a * jax.nn.sigmoid(a)) * b
        y_ref[...] = jnp.dot(act.astype(jnp.bfloat16), w2_ref[...], preferred_element_type=jnp.float32)

    @pl.when(tv_ref[i] == 0)
    def _():
        y_ref[...] = jnp.zeros_like(y_ref)


def moe_grouped(h, tile_expert, tile_valid, row_token, w1, w3, w2):
    grid_spec = pltpu.PrefetchScalarGridSpec(
        num_scalar_prefetch=3,
        grid=(MOE_TILES,),
        in_specs=[pl.BlockSpec(memory_space=pl.ANY),
                  pl.BlockSpec((None, D_MODEL, D_FF), lambda i, te, tv, rt: (te[i], 0, 0)),
                  pl.BlockSpec((None, D_MODEL, D_FF), lambda i, te, tv, rt: (te[i], 0, 0)),
                  pl.BlockSpec((None, D_FF, D_MODEL), lambda i, te, tv, rt: (te[i], 0, 0))],
        out_specs=pl.BlockSpec((MOE_TM, D_MODEL), lambda i, te, tv, rt: (i, 0)),
        scratch_shapes=[pltpu.VMEM((MOE_SLOTS, MOE_TM, D_MODEL), jnp.float32),
                        pltpu.SemaphoreType.DMA((MOE_SLOTS,))],
    )
    return pl.pallas_call(
        _moe_kernel,
        grid_spec=grid_spec,
        out_shape=jax.ShapeDtypeStruct((MOE_TILES * MOE_TM, D_MODEL), jnp.float32),
        compiler_params=_cparams("arbitrary"),
        name="moe_grouped",
    )(tile_expert, tile_valid, row_token, h, w1, w3, w2)


def _combine_kernel(pos_ref, y_hbm, x_ref, w_ref, mod_ref, o_ref, buf_ref, sem_ref, *, gate_row):
    i = pl.program_id(0)
    n_steps = pl.num_programs(0)

    def row_copy(step, slot, r):
        src = pos_ref[step * (TOP_K * CMB_TB) + r]
        return pltpu.make_async_copy(y_hbm.at[pl.ds(src, 1), :], buf_ref.at[slot, pl.ds(r, 1), :], sem_ref.at[slot])

    def start_gather(step, slot):
        def body(r, carry):
            row_copy(step, slot, r).start()
            return carry
        lax.fori_loop(0, TOP_K * CMB_TB, body, 0, unroll=DMA_UNROLL)

    @pl.when(i == 0)
    def _():
        start_gather(0, 0)

    @pl.when(i + 1 < n_steps)
    def _():
        start_gather(i + 1, (i + 1) % 2)

    slot = i % 2

    def wait_body(r, carry):
        row_copy(i, slot, r).wait()
        return carry
    lax.fori_loop(0, TOP_K * CMB_TB, wait_body, 0, unroll=DMA_UNROLL)
    w = w_ref[...]
    moe = w[:, 0:1] * buf_ref[slot, 0:CMB_TB, :] + w[:, 1:2] * buf_ref[slot, CMB_TB:2 * CMB_TB, :]
    o_ref[...] = x_ref[...] + mod_ref[gate_row:gate_row + 1, :] * moe


def moe_combine(pos, ys, x, top_w, mod, gate_row):
    n = x.shape[0]
    grid_spec = pltpu.PrefetchScalarGridSpec(
        num_scalar_prefetch=1,
        grid=(n // CMB_TB,),
        in_specs=[pl.BlockSpec(memory_space=pl.ANY),
                  pl.BlockSpec((CMB_TB, D_MODEL), lambda i, pos: (i, 0)),
                  pl.BlockSpec((CMB_TB, TOP_K), lambda i, pos: (i, 0)),
                  pl.BlockSpec((None, 6, D_MODEL), lambda i, pos: (jnp.where(i * CMB_TB >= CTX_LEN, 1, 0), 0, 0))],
        out_specs=pl.BlockSpec((CMB_TB, D_MODEL), lambda i, pos: (i, 0)),
        scratch_shapes=[pltpu.VMEM((2, TOP_K * CMB_TB, D_MODEL), jnp.float32), pltpu.SemaphoreType.DMA((2,))],
    )
    return pl.pallas_call(
        functools.partial(_combine_kernel, gate_row=gate_row),
        grid_spec=grid_spec,
        out_shape=jax.ShapeDtypeStruct((n, D_MODEL), jnp.float32),
        compiler_params=_cparams("arbitrary"),
        name="moe_combine",
    )(pos, ys, x, top_w, mod)


def moe_route(logits, bg, be):
    n = logits.shape[0]
    g_logits = logits[:, :N_GROUPS] + bg
    grp = jnp.argmax(g_logits, axis=-1)
    g_sel = grp[:, None] == jnp.arange(N_GROUPS)[None, :]
    p_grp = jnp.sum(jnp.where(g_sel, jax.nn.softmax(g_logits, axis=-1), 0.0), axis=-1, keepdims=True)
    e_logits = (logits[:, N_GROUPS:N_GROUPS + N_EXPERTS] + be).reshape(n, N_GROUPS, EXP_PER_GROUP)
    e_in = jnp.sum(jnp.where(g_sel[:, :, None], e_logits, 0.0), axis=1)
    lane = jnp.arange(EXP_PER_GROUP)[None, :]
    i1 = jnp.argmax(e_in, axis=-1)
    v1 = jnp.max(e_in, axis=-1)
    rest = jnp.where(lane == i1[:, None], -jnp.inf, e_in)
    i2 = jnp.argmax(rest, axis=-1)
    v2 = jnp.max(rest, axis=-1)
    top_w = jax.nn.softmax(jnp.stack([v1, v2], axis=-1), axis=-1) * p_grp
    eid = grp[:, None] * EXP_PER_GROUP + jnp.stack([i1, i2], axis=-1)
    return eid.astype(jnp.int32), top_w


def moe_schedule(eid):
    n = eid.shape[0]
    flat_e = eid.reshape(-1)
    onehot = (flat_e[:, None] == jnp.arange(N_EXPERTS)[None, :]).astype(jnp.int32)
    rank = jnp.sum(onehot * (jnp.cumsum(onehot, axis=0) - 1), axis=1)
    counts = jnp.sum(onehot, axis=0)
    padded = ((counts + MOE_TM - 1) // MOE_TM) * MOE_TM
    pad_end = jnp.cumsum(padded)
    pad_start = pad_end - padded
    dest = (jnp.sum(onehot * pad_start[None, :], axis=1) + rank).astype(jnp.int32)
    n_rows = MOE_TILES * MOE_TM
    row_token = jnp.zeros((n_rows,), jnp.int32).at[dest].set(jnp.arange(TOP_K * n, dtype=jnp.int32) // TOP_K)
    tile_row0 = jnp.arange(MOE_TILES, dtype=jnp.int32) * MOE_TM
    n_valid = pad_end[-1] // MOE_TM
    tile_valid = (tile_row0 < pad_end[-1]).astype(jnp.int32)
    tile_expert = jnp.sum((tile_row0[:, None] >= pad_end[None, :]).astype(jnp.int32), axis=1)
    last_expert = jnp.sum(jnp.where(jnp.arange(MOE_TILES) == n_valid - 1, tile_expert, 0))
    tile_expert = jnp.where(tile_valid != 0, tile_expert, last_expert).astype(jnp.int32)
    pos = dest.reshape(n // CMB_TB, CMB_TB, TOP_K).transpose(0, 2, 1).reshape(-1)
    return tile_expert, tile_valid, row_token, pos


NA_QR = 4
NA_KR = NA_QR + NA_MAX_ROWS - 1
NA_TQ = NA_QR * GRID_W
NA_TK = NA_KR * GRID_W
NA_HP = 8
NA_WIN_BLOCKS = 3


def natten_bias_tiles(rpb):
    hi = lax.Precision.HIGHEST
    cols = jnp.arange(GRID_W)
    dc = jnp.clip(cols[None, :] - cols[:, None] + (NA_COLS - 1), 0, 2 * NA_COLS - 2)
    c0 = jnp.clip(cols - NA_COLS // 2, 0, GRID_W - NA_COLS)
    in_win = (cols[None, :] >= c0[:, None]) & (cols[None, :] < c0[:, None] + NA_COLS)
    oh_c = (dc[..., None] == jnp.arange(2 * NA_COLS - 1)).astype(jnp.float32)
    toep = jnp.einsum('qkc,hdc->hdqk', oh_c, rpb.astype(jnp.float32), precision=hi)
    rl = jnp.arange(NA_QR)[:, None]
    kl = jnp.arange(NA_KR)[None, :]
    half = NA_MAX_ROWS // 2
    pats = [(kl - rl + NA_MAX_ROWS - 1, (kl < NA_MAX_ROWS) & (rl >= 0)),
            (kl - rl + NA_MAX_ROWS - 1 - half, (kl - rl >= 0) & (kl - rl < NA_MAX_ROWS)),
            (kl - rl, (kl >= NA_KR - NA_MAX_ROWS) & (rl >= 0))]
    tiles = []
    for dr, valid in pats:
        oh_r = ((dr[..., None] == jnp.arange(2 * NA_MAX_ROWS - 1)) & valid[..., None]).astype(jnp.float32)
        t = jnp.einsum('rkd,hdqc->hrqkc', oh_r, toep, precision=hi)
        ok = valid[None, :, None, :, None] & in_win[None, None, :, None, :]
        tiles.append(jnp.where(ok, t, -jnp.inf).reshape(rpb.shape[0], NA_TQ, NA_TK))
    return jnp.stack(tiles, axis=1)


def _natten_kernel(q_ref, kc_ref, vc_ref, k0_ref, k1_ref, k2_ref, v0_ref, v1_ref, v2_ref, bias_ref, o_ref,
                   kw_ref, vw_ref, *, n_rows, n_blk, hp):
    j = pl.program_id(1)
    bf16 = jnp.bfloat16
    f32 = jnp.float32
    nt = (((1,), (1,)), ((), ()))
    heads = [slice(hh * HEAD_DIM, (hh + 1) * HEAD_DIM) for hh in range(hp)]

    def ctx_scores(cols):
        q = (q_ref[:, cols] * HEAD_DIM ** -0.5).astype(bf16)
        kc = kc_ref[:, cols].astype(bf16)
        return q, lax.dot_general(q, kc, nt, preferred_element_type=f32)

    @pl.when(j == 0)
    def _():
        for cols in heads:
            _, s_ctx = ctx_scores(cols)
            m = jnp.max(s_ctx, axis=-1, keepdims=True)
            p = jnp.exp(s_ctx - m)
            l = jnp.sum(p, axis=-1, keepdims=True)
            o = jnp.dot(p.astype(bf16), vc_ref[:, cols].astype(bf16), preferred_element_type=f32)
            o_ref[:, cols] = (o / l).astype(o_ref.dtype)

    @pl.when(j > 0)
    def _():
        for t, (k_ref, v_ref) in enumerate(((k0_ref, v0_ref), (k1_ref, v1_ref), (k2_ref, v2_ref))):
            kw_ref[t * ROW_TILE:(t + 1) * ROW_TILE, :] = k_ref[...].astype(bf16)
            vw_ref[t * ROW_TILE:(t + 1) * ROW_TILE, :] = v_ref[...].astype(bf16)
        r0 = (j - 1) * NA_QR
        kbase = jnp.clip(r0 - NA_MAX_ROWS // 2, 0, n_rows - NA_KR)
        first_blk = jnp.clip(j - 1, 1, n_blk - NA_WIN_BLOCKS)
        off = pl.multiple_of(CTX_LEN + kbase * GRID_W - first_blk * ROW_TILE, GRID_W)
        for hh, cols in enumerate(heads):
            q, s_ctx = ctx_scores(cols)
            kl = kw_ref[pl.ds(off, NA_TK), cols]
            vl = vw_ref[pl.ds(off, NA_TK), cols]
            s_loc = lax.dot_general(q, kl, nt, preferred_element_type=f32) + bias_ref[hh]
            m = jnp.maximum(jnp.max(s_loc, axis=-1, keepdims=True), jnp.max(s_ctx, axis=-1, keepdims=True))
            p_loc = jnp.exp(s_loc - m)
            p_ctx = jnp.exp(s_ctx - m)
            l = jnp.sum(p_loc, axis=-1, keepdims=True) + jnp.sum(p_ctx, axis=-1, keepdims=True)
            o = (jnp.dot(p_loc.astype(bf16), vl, preferred_element_type=f32)
                 + jnp.dot(p_ctx.astype(bf16), vc_ref[:, cols].astype(bf16), preferred_element_type=f32))
            o_ref[:, cols] = (o / l).astype(o_ref.dtype)


def natten(p_main, bias_tiles, hp=NA_HP):
    n = p_main.shape[0]
    n_rows = (n - CTX_LEN) // GRID_W
    n_blocks = n_rows // NA_QR
    n_blk = n // ROW_TILE
    assert n_rows % NA_QR == 0 and n_rows >= NA_KR and CTX_LEN == NA_TQ == ROW_TILE
    assert NA_TK + GRID_W <= NA_WIN_BLOCKS * ROW_TILE and n_blk > NA_WIN_BLOCKS
    w = hp * HEAD_DIM
    gpc = GROUP_W // w

    def pat(j):
        return jnp.where(j <= 1, 0, jnp.where(j == n_blocks, 2, 1))

    def win(col, t):
        return pl.BlockSpec((ROW_TILE, w), lambda h, j: (jnp.clip(j - 1, 1, n_blk - NA_WIN_BLOCKS) + t, col * gpc + h))

    return pl.pallas_call(
        functools.partial(_natten_kernel, n_rows=n_rows, n_blk=n_blk, hp=hp),
        grid=(gpc, n_blocks + 1),
        in_specs=[pl.BlockSpec((NA_TQ, w), lambda h, j: (j, COL_Q * gpc + h)),
                  pl.BlockSpec((CTX_LEN, w), lambda h, j: (0, COL_K * gpc + h)),
                  pl.BlockSpec((CTX_LEN, w), lambda h, j: (0, COL_V * gpc + h)),
                  win(COL_K, 0), win(COL_K, 1), win(COL_K, 2), win(COL_V, 0), win(COL_V, 1), win(COL_V, 2),
                  pl.BlockSpec((hp, None, NA_TQ, NA_TK), lambda h, j: (h, pat(j), 0, 0))],
        out_specs=pl.BlockSpec((NA_TQ, w), lambda h, j: (j, h)),
        out_shape=jax.ShapeDtypeStruct((n, GROUP_W), jnp.bfloat16),
        scratch_shapes=[pltpu.VMEM((NA_WIN_BLOCKS * ROW_TILE, w), jnp.bfloat16),
                        pltpu.VMEM((NA_WIN_BLOCKS * ROW_TILE, w), jnp.bfloat16)],
        compiler_params=_cparams("arbitrary", "arbitrary"),
        name="natten",
    )(*([p_main] * 9), bias_tiles)


POOL_HALO = max(POOL_WINDOWS) // 2


def _pool_kernel(prev_ref, cur_ref, next_ref, w_ref, s_ref, o_ref, ext_ref, *, n_tok):
    i = pl.program_id(0)
    t0 = i * ROW_TILE
    seq_lo = jnp.where(t0 < CTX_LEN, 0, CTX_LEN)
    seq_hi = jnp.where(t0 < CTX_LEN, CTX_LEN, n_tok)
    ext_ref[0:POOL_HALO, :] = jnp.where(t0 - POOL_HALO >= seq_lo, prev_ref[...], 0.0)
    ext_ref[POOL_HALO:POOL_HALO + ROW_TILE, :] = cur_ref[...]
    ext_ref[POOL_HALO + ROW_TILE:, :] = jnp.where(t0 + ROW_TILE < seq_hi, next_ref[...], 0.0)
    t = t0 + lax.broadcasted_iota(jnp.int32, (ROW_TILE, 1), 0)
    for g, win in enumerate(POOL_WINDOWS):
        half = win // 2
        cs = slice(g * C_GW, (g + 1) * C_GW)
        acc = ext_ref[POOL_HALO - half:POOL_HALO - half + ROW_TILE, cs]
        for d in range(-half + 1, half):
            acc = acc + ext_ref[POOL_HALO + d:POOL_HALO + d + ROW_TILE, cs]
        cnt = (jnp.minimum(t + half, seq_hi) - jnp.maximum(t - half, seq_lo)).astype(jnp.float32)
        diff = acc / cnt - cur_ref[:, cs]
        y = jnp.dot(diff.astype(jnp.bfloat16), w_ref[g], preferred_element_type=jnp.float32)
        o_ref[:, cs] = (y * s_ref[:, cs]).astype(o_ref.dtype)


def pool_mix(p_main, w_pool, pool_scale):
    n = p_main.shape[0]
    hb = ROW_TILE // POOL_HALO
    n_hblk = n // POOL_HALO
    return pl.pallas_call(
        functools.partial(_pool_kernel, n_tok=n),
        grid=(n // ROW_TILE,),
        in_specs=[pl.BlockSpec((POOL_HALO, GROUP_W), lambda i: (jnp.maximum(i * hb - 1, 0), COL_C)),
                  pl.BlockSpec((ROW_TILE, GROUP_W), lambda i: (i, COL_C)),
                  pl.BlockSpec((POOL_HALO, GROUP_W), lambda i: (jnp.minimum((i + 1) * hb, n_hblk - 1), COL_C)),
                  pl.BlockSpec((C_GROUPS, C_GW, C_GW), lambda i: (0, 0, 0)),
                  pl.BlockSpec((1, GROUP_W), lambda i: (0, 0))],
        out_specs=pl.BlockSpec((ROW_TILE, GROUP_W), lambda i: (i, 0)),
        out_shape=jax.ShapeDtypeStruct((n, GROUP_W), jnp.bfloat16),
        scratch_shapes=[pltpu.VMEM((ROW_TILE + 2 * POOL_HALO, GROUP_W), jnp.float32)],
        compiler_params=_cparams("arbitrary"),
        name="pool_mix",
    )(p_main, p_main, p_main, w_pool, pool_scale.reshape(1, GROUP_W))


def chunk_masks(chunk, reverse):
    i = jnp.arange(ROW_TILE)[:, None]
    j = jnp.arange(ROW_TILE)[None, :]
    same = (i // chunk) == (j // chunk)
    return (same & ((j >= i) if reverse else (j <= i))).astype(jnp.bfloat16)


def _split3(x):
    hi = x.astype(jnp.bfloat16)
    r = x - hi.astype(jnp.float32)
    mid = r.astype(jnp.bfloat16)
    lo = (r - mid.astype(jnp.float32)).astype(jnp.bfloat16)
    return hi, mid, lo


def _dot3(m, x):
    hi, mid, lo = _split3(x)
    d = functools.partial(jnp.dot, preferred_element_type=jnp.float32)
    return d(m, hi) + (d(m, mid) + d(m, lo))


def _scan_block(n_blk, reverse):
    if reverse:
        return lambda j: jnp.where(j == 0, 0, n_blk - j)
    return lambda j: j


def _hgrn_head(q_ref, v_ref, z_ref, lb_ref, mask_ref, st, cols, reverse):
    bf16 = jnp.bfloat16
    f32 = jnp.float32
    T, C = ROW_TILE, A_CHUNK
    nt = (((1,), (1,)), ((), ()))
    tn = (((0,), (0,)), ((), ()))
    z = z_ref[:, cols]
    log_lb = lb_ref[0:1, cols]
    log1m_lb = lb_ref[1:2, cols]
    one_m_lb = lb_ref[2:3, cols]
    log_sig = jnp.minimum(z, 0.0) - jnp.log1p(jnp.exp(-jnp.abs(z)))
    bb = log1m_lb + log_sig
    mx = jnp.maximum(log_lb, bb)
    lf = mx + jnp.log1p(jnp.exp(-jnp.abs(log_lb - bb)))
    k = one_m_lb / (1.0 + jnp.exp(z))
    aq = q_ref[:, cols]
    q = aq / (1.0 + jnp.exp(-aq))
    v = v_ref[:, cols].astype(bf16)

    b = _dot3(mask_ref[...], lf)
    last = 0 if reverse else C - 1
    b_end = jnp.concatenate([jnp.broadcast_to(b[c * C + last:c * C + last + 1, :], (C, HEAD_DIM))
                             for c in range(T // C)], axis=0)
    q_dec = (q * jnp.exp(b)).astype(bf16)
    k_inv = (k * jnp.exp(-b)).astype(bf16)
    k_dec = (k * jnp.exp(b_end - b)).astype(bf16)
    dec = jnp.exp(b_end)

    scores = lax.dot_general(q_dec, k_inv, nt, preferred_element_type=f32)
    scores = jnp.where(mask_ref[...] > 0, scores, 0.0)
    o_intra = jnp.dot(scores.astype(bf16), v, preferred_element_type=f32)

    n_c = T // C
    rows = [slice(c * C, (c + 1) * C) for c in range(n_c)]
    deltas = [lax.dot_general(v[r], k_dec[r], tn, preferred_element_type=f32) for r in rows]
    entering = [None] * n_c
    for c in (range(n_c - 1, -1, -1) if reverse else range(n_c)):
        entering[c] = st
        st = st * dec[c * C:c * C + 1, :] + deltas[c]
    outs = [lax.dot_general(q_dec[r], entering[c].astype(bf16), nt, preferred_element_type=f32)
            for c, r in enumerate(rows)]
    return o_intra + jnp.concatenate(outs, axis=0), st


def _hgrn_kernel(q_ref, v_ref, z_ref, lb_ref, mask_ref, *rest, reverse, hp):
    if reverse:
        g_ref, of_ref, gn_ref, o_ref, st_ref = rest
    else:
        o_ref, st_ref = rest

    @pl.when(pl.program_id(1) == 0)
    def _():
        st_ref[...] = jnp.zeros_like(st_ref)

    for hh in range(hp):
        cols = slice(hh * HEAD_DIM, (hh + 1) * HEAD_DIM)
        o, st = _hgrn_head(q_ref, v_ref, z_ref, lb_ref, mask_ref, st_ref[hh], cols, reverse)
        st_ref[hh] = st
        if reverse:
            o = o + of_ref[:, cols]
            o = o * lax.rsqrt(jnp.mean(o * o, axis=-1, keepdims=True) + EPS) * gn_ref[:, cols]
            ag = g_ref[:, cols]
            o_ref[:, cols] = (o * (ag / (1.0 + jnp.exp(-ag)))).astype(o_ref.dtype)
        else:
            o_ref[:, cols] = o


def hgrn_scan(p_main, lb_rows, mask, reverse, o_fwd=None, gn=None, hp=SCAN_HP):
    n = p_main.shape[0]
    n_blk = n // ROW_TILE
    tb = _scan_block(n_blk, reverse)
    w = hp * HEAD_DIM
    gpc = GROUP_W // w
    blk = lambda c: pl.BlockSpec((ROW_TILE, w), lambda h, j: (tb(j), c * gpc + h))
    in_specs = [blk(COL_AQ), blk(COL_AI), blk(COL_AFB if reverse else COL_AFF),
                pl.BlockSpec((3, w), lambda h, j: (0, h)),
                pl.BlockSpec((ROW_TILE, ROW_TILE), lambda h, j: (0, 0))]
    args = [p_main, p_main, p_main, lb_rows, mask]
    if reverse:
        in_specs += [blk(COL_AG), pl.BlockSpec((ROW_TILE, w), lambda h, j: (tb(j), h)),
                     pl.BlockSpec((1, w), lambda h, j: (0, h))]
        args += [p_main, o_fwd, gn.reshape(1, GROUP_W)]
    return pl.pallas_call(
        functools.partial(_hgrn_kernel, reverse=reverse, hp=hp),
        grid=(gpc, n_blk),
        in_specs=in_specs,
        out_specs=pl.BlockSpec((ROW_TILE, w), lambda h, j: (tb(j), h)),
        out_shape=jax.ShapeDtypeStruct((n, GROUP_W), jnp.bfloat16 if reverse else jnp.float32),
        scratch_shapes=[pltpu.VMEM((hp, HEAD_DIM, HEAD_DIM), jnp.float32)],
        compiler_params=_cparams("arbitrary", "arbitrary"),
        name="hgrn_bwd" if reverse else "hgrn_fwd",
    )(*args)


def hgrn_mix(p_main, lb_f, lb_b, gn):
    rows = lambda lb: jnp.stack([jnp.log(lb), jnp.log1p(-lb), 1.0 - lb])
    o_f = hgrn_scan(p_main, rows(lb_f), chunk_masks(A_CHUNK, False), False)
    return hgrn_scan(p_main, rows(lb_b), chunk_masks(A_CHUNK, True), True, o_f, gn)


def hgrn_lower_bounds(p):
    cs = jnp.cumsum(jax.nn.softmax(p.astype(jnp.float32), axis=1), axis=1)
    return cs - cs[:, :1]


def rope_tables(n_tok):
    quarter = HEAD_DIM // 4
    pos = jnp.arange(n_tok - CTX_LEN)
    inv_freq = ROPE_BASE ** (-jnp.arange(quarter, dtype=jnp.float32) / quarter)
    lane = jnp.arange(HEAD_DIM)
    p = jnp.where(lane[None, :] < HEAD_DIM // 2, (pos // GRID_W)[:, None], (pos % GRID_W)[:, None]).astype(jnp.float32)
    ang = p * inv_freq[lane % quarter][None, :]
    sign = jnp.where((lane % (2 * quarter)) < quarter, -1.0, 1.0)
    cos = jnp.concatenate([jnp.ones((CTX_LEN, HEAD_DIM), jnp.float32), jnp.cos(ang)], axis=0)
    sin = jnp.concatenate([jnp.zeros((CTX_LEN, HEAD_DIM), jnp.float32), jnp.sin(ang) * sign], axis=0)
    return cos, sin


def _rope(x, cos, sin):
    quarter = HEAD_DIM // 4
    lane = lax.broadcasted_iota(jnp.int32, x.shape, 1)
    partner = jnp.where((lane % (2 * quarter)) < quarter,
                        pltpu.roll(x, HEAD_DIM - quarter, axis=1), pltpu.roll(x, quarter, axis=1))
    return x * cos + partner * sin


def _mlstm_head(q_ref, k_ref, v_ref, cos, sin, cols, b_bc, li_bc, b_row, li_row, state, reverse):
    bf16 = jnp.bfloat16
    f32 = jnp.float32
    T, C = ROW_TILE, D_CHUNK
    n_c = T // C
    nt = (((1,), (1,)), ((), ()))
    tn = (((0,), (0,)), ((), ()))
    d = functools.partial(jnp.dot, preferred_element_type=f32)
    q = _rope(q_ref[:, cols], cos, sin) * HEAD_DIM ** -0.5
    k = _rope(k_ref[:, cols], cos, sin)
    qb = q.astype(bf16)
    kb = k.astype(bf16)
    vb = v_ref[:, cols].astype(bf16)
    ii = lax.broadcasted_iota(jnp.int32, (C, C), 0)
    jj = lax.broadcasted_iota(jnp.int32, (C, C), 1)
    causal = (jj >= ii) if reverse else (jj <= ii)
    last = 0 if reverse else C - 1
    rows = [slice(c * C, (c + 1) * C) for c in range(n_c)]
    bends, m_locs, d_ss, d_ns = [], [], [], []
    for c, r in enumerate(rows):
        bend = b_bc[c * C + last:c * C + last + 1]
        gcol = bend - b_bc[r] + li_bc[r]
        m_loc = jnp.max(gcol, axis=0, keepdims=True)
        kw = k[r] * jnp.exp(gcol - m_loc)
        bends.append(bend)
        m_locs.append(m_loc)
        d_ss.append(lax.dot_general(kw.astype(bf16), vb[r], tn, preferred_element_type=f32))
        d_ns.append(jnp.sum(kw, axis=0, keepdims=True))
    s, nv, m = state
    entering = [None] * n_c
    for c in (range(n_c - 1, -1, -1) if reverse else range(n_c)):
        entering[c] = (s, nv, m)
        m_new = jnp.maximum(bends[c] + m, m_locs[c])
        a = jnp.exp(bends[c] + m - m_new)
        cc = jnp.exp(m_locs[c] - m_new)
        s = a * s + cc * d_ss[c]
        nv = a * nv + cc * d_ns[c]
        m = m_new
    outs = []
    for c, r in enumerate(rows):
        s_in, n_in, m_in = entering[c]
        dmat = jnp.where(causal, b_bc[r][:, 0:C] - b_row[:, r] + li_row[:, r], -jnp.inf)
        m_inter = b_bc[r] + m_in
        m_t = jnp.maximum(m_inter, jnp.max(dmat, axis=-1, keepdims=True))
        w = jnp.exp(dmat - m_t[:, 0:C]) * lax.dot_general(qb[r], kb[r], nt, preferred_element_type=f32)
        a_col = jnp.exp(m_inter - m_t)
        num = d(w.astype(bf16), vb[r]) + a_col * d(qb[r], s_in.astype(bf16))
        den = jnp.sum(w, axis=-1, keepdims=True) + a_col * jnp.sum(q[r] * n_in, axis=-1, keepdims=True)
        outs.append(num / jnp.maximum(jnp.abs(den), jnp.exp(-m_t)))
    return jnp.concatenate(outs, axis=0), (s, nv, m)


def _mlstm_kernel(q_ref, k_ref, v_ref, g_ref, cos_ref, sin_ref, mask_ref, maskt_ref, *rest, reverse, hp):
    if reverse:
        do_ref, of_ref, gn_ref, o_ref, s_ref, n_ref, m_ref, cumt_ref, lit_ref = rest
    else:
        o_ref, s_ref, n_ref, m_ref, cumt_ref, lit_ref = rest
    hg = pl.program_id(0)
    f32 = jnp.float32
    bf16 = jnp.bfloat16

    @pl.when(pl.program_id(1) == 0)
    def _():
        s_ref[...] = jnp.zeros_like(s_ref)
        n_ref[...] = jnp.zeros_like(n_ref)
        m_ref[...] = jnp.zeros_like(m_ref)

    g = g_ref[...]
    lsg = jnp.minimum(g, 0.0) - jnp.log1p(jnp.exp(-jnp.abs(g)))
    cum = _dot3(mask_ref[...], lsg)
    hi, mid, lo = _split3(lsg.T)
    d = functools.partial(jnp.dot, preferred_element_type=f32)
    cumt_ref[...] = d(hi, maskt_ref[...]) + (d(mid, maskt_ref[...]) + d(lo, maskt_ref[...]))
    lit_ref[...] = g.T
    cum_parts = _split3(cum)
    g_parts = _split3(g)
    sel_row = lax.broadcasted_iota(jnp.int32, (LANES, LANES), 0)

    def replicate(parts, col):
        sel = (sel_row == col).astype(bf16)
        return d(parts[0], sel) + (d(parts[1], sel) + d(parts[2], sel))

    cos = cos_ref[...]
    sin = sin_ref[...]
    for hh in range(hp):
        h = hg * hp + hh
        col_i = (GATE_I_B if reverse else GATE_I_F) * N_HEADS + h
        col_f = (GATE_F_B if reverse else GATE_F_F) * N_HEADS + h
        b_bc = replicate(cum_parts, col_f)
        li_bc = replicate(g_parts, col_i)
        b_row = cumt_ref[pl.ds(col_f, 1), :]
        li_row = lit_ref[pl.ds(col_i, 1), :]
        cols = slice(hh * HEAD_DIM, (hh + 1) * HEAD_DIM)
        state = (s_ref[hh], n_ref[hh, 0:1, :], m_ref[hh, 0:1, :])
        o, (s, nv, m) = _mlstm_head(q_ref, k_ref, v_ref, cos, sin, cols, b_bc, li_bc, b_row, li_row, state, reverse)
        s_ref[hh] = s
        n_ref[hh] = jnp.broadcast_to(nv, n_ref.shape[1:])
        m_ref[hh] = jnp.broadcast_to(m, m_ref.shape[1:])
        if reverse:
            o = o + of_ref[:, cols]
            o = o * lax.rsqrt(jnp.mean(o * o, axis=-1, keepdims=True) + EPS) * gn_ref[:, cols]
            o_ref[:, cols] = (o / (1.0 + jnp.exp(-do_ref[:, cols]))).astype(o_ref.dtype)
        else:
            o_ref[:, cols] = o


def mlstm_scan(p_main, p_gate, cos, sin, mask, reverse, o_fwd=None, gn=None, hp=SCAN_HP):
    n = p_main.shape[0]
    n_blk = n // ROW_TILE
    tb = _scan_block(n_blk, reverse)
    w = hp * HEAD_DIM
    gpc = GROUP_W // w
    blk = lambda c: pl.BlockSpec((ROW_TILE, w), lambda h, j: (tb(j), c * gpc + h))
    tok = pl.BlockSpec((ROW_TILE, LANES), lambda h, j: (tb(j), 0))
    sq = pl.BlockSpec((ROW_TILE, ROW_TILE), lambda h, j: (0, 0))
    in_specs = [blk(COL_DQ), blk(COL_DK), blk(COL_DV), tok, tok, tok, sq, sq]
    args = [p_main, p_main, p_main, p_gate, cos, sin, mask, mask.T]
    if reverse:
        in_specs += [blk(COL_DO), pl.BlockSpec((ROW_TILE, w), lambda h, j: (tb(j), h)),
                     pl.BlockSpec((1, w), lambda h, j: (0, h))]
        args += [p_main, o_fwd, gn.reshape(1, GROUP_W)]
    return pl.pallas_call(
        functools.partial(_mlstm_kernel, reverse=reverse, hp=hp),
        grid=(gpc, n_blk),
        in_specs=in_specs,
        out_specs=pl.BlockSpec((ROW_TILE, w), lambda h, j: (tb(j), h)),
        out_shape=jax.ShapeDtypeStruct((n, GROUP_W), jnp.bfloat16 if reverse else jnp.float32),
        scratch_shapes=[pltpu.VMEM((hp, HEAD_DIM, HEAD_DIM), jnp.float32), pltpu.VMEM((hp, 8, HEAD_DIM), jnp.float32),
                        pltpu.VMEM((hp, 8, LANES), jnp.float32), pltpu.VMEM((LANES, ROW_TILE), jnp.float32),
                        pltpu.VMEM((LANES, ROW_TILE), jnp.float32)],
        compiler_params=_cparams("arbitrary", "arbitrary"),
        name="mlstm_bwd" if reverse else "mlstm_fwd",
    )(*args)


def mlstm_mix(p_main, p_gate, gn):
    cos, sin = rope_tables(p_main.shape[0])
    o_f = mlstm_scan(p_main, p_gate, cos, sin, chunk_masks(D_CHUNK, False), False)
    return mlstm_scan(p_main, p_gate, cos, sin, chunk_masks(D_CHUNK, True), True, o_f, gn)


def kernel(x, c, ctx, c_ctx, ada_w, ada_b, norm1_w, norm2_w, mix_w_in, mix_b_in, hgrn_lb, hgrn_gn, natten_rpb,
           pool_w, pool_scale, mlstm_gn, mix_w_out, moe_wg, moe_bg, moe_we, moe_be, moe_w1, moe_w3, moe_w2, final_w):
    assert x.shape == (1, SEQ, D_MODEL) and ctx.shape == (1, CTX_LEN, D_MODEL)
    bf16 = jnp.bfloat16
    lbs = hgrn_lower_bounds(hgrn_lb)

    cond = jnp.zeros((16, D_MODEL), jnp.float32).at[0].set(jax.nn.silu(c_ctx)).at[1].set(jax.nn.silu(c[0]))
    mods = ada_modulation(cond.astype(bf16), ada_w, ada_b)[:, :2].reshape(DEPTH, 2, 6, D_MODEL)

    w_in_t = jnp.swapaxes(mix_w_in, 1, 2)
    w_in = cast_bf16_transposed(w_in_t, N_MAIN)
    w_out = cast_bf16(mix_w_out, 4 * LANES)

    xs = jnp.concatenate([ctx[0], x[0]], axis=0)
    for l in range(DEPTH):
        mod = mods[l]
        h = norm_modulate(xs, norm1_w[l], mod, 0)
        p_main, w1, w3, w2 = in_proj_cast(h, w_in, mix_b_in[l], l, moe_w1, moe_w3, moe_w2)
        w_gate_t = jnp.pad(w_in_t[l, N_MAIN:, :], ((0, LANES - N_GATE), (0, 0)))
        b_gate = jnp.pad(mix_b_in[l, N_MAIN:], (0, LANES - N_GATE))
        p_gate = matmul_bias_nt(h, w_gate_t, b_gate, MM_TM)

        out_a = hgrn_mix(p_main, lbs[0, l], lbs[1, l], hgrn_gn[l])
        out_b = natten(p_main, natten_bias_tiles(natten_rpb[l]))
        out_c = pool_mix(p_main, pool_w[l].astype(bf16), pool_scale[l])
        out_d = mlstm_mix(p_main, p_gate, mlstm_gn[l])
        xs = matmul_gated_residual((out_a, out_b, out_c, out_d), w_out, l, xs, mod, 2)

        w_router = jnp.pad(jnp.concatenate([moe_wg[l], moe_we[l]], axis=1),
                           ((0, 0), (0, LANES - N_GROUPS - N_EXPERTS)))
        h2, logits = norm_modulate_router(xs, norm2_w[l], mod, w_router)
        eid, top_w = moe_route(logits, moe_bg[l], moe_be[l])
        tile_expert, tile_valid, row_token, cpos = moe_schedule(eid)
        ys = moe_grouped(h2, tile_expert, tile_valid, row_token, w1, w3, w2)
        xs = moe_combine(cpos, ys, xs, top_w, mod, 5)

    return norm_modulate(xs, final_w, mods[0], None, out_dtype=jnp.float32, skip_rows=CTX_LEN)[None]
```
